```python
import math
import jax, jax.numpy as jnp
from jax import lax
import numpy as np

D_MODEL = 1024
BATCH = 8
SEQ = 4096
DEPTH = 1

CHUNK = 64
N_META = 16
Q_BLOCK = 128
MIX_WIDTH = D_MODEL
EPS = 1e-6
NEG = -1e30

DA_HEADS = 4
DA_WIDTH = MIX_WIDTH // 2
DA_V_DIM = DA_WIDTH // DA_HEADS
DA_HEAD_DIM = DA_V_DIM // 2
DA_QK = DA_HEADS * 2 * DA_HEAD_DIM

MLA_HEADS = 4
MLA_WIDTH = MIX_WIDTH - DA_WIDTH
MLA_V = MLA_WIDTH // MLA_HEADS
MLA_NOPE = 128
MLA_ROPE = 64
MLA_Q_RANK = D_MODEL // 4
MLA_KV_RANK = D_MODEL // 8
ROPE_THETA = 10000.0

IN_SIZES = (DA_QK, DA_QK, DA_WIDTH, MLA_Q_RANK, MLA_KV_RANK, MLA_ROPE)
IN_WIDTH = sum(IN_SIZES)
IN_SPLITS = tuple(int(s) for s in np.cumsum(IN_SIZES)[:-1])

D_FF = 4 * D_MODEL

kernel_name = "hymba_diffattn_mla_sandwich_block"


def _rms(x, g):
    xf = x.astype(jnp.float32)
    y = xf * lax.rsqrt(jnp.mean(xf * xf, axis=-1, keepdims=True) + EPS)
    return (y * g.astype(jnp.float32)).astype(x.dtype)


def _chunk_ids(pos):
    return jnp.where(pos < N_META, 0, (pos - N_META) // CHUNK + 1)


def _chunk_mask(qpos, kpos):
    return _chunk_ids(kpos)[None, :] <= _chunk_ids(qpos)[:, None]


def _sweep_query_blocks(fn, qs, L):
    n_blk = -(-L // Q_BLOCK)
    Lp = n_blk * Q_BLOCK

    def prep(q):
        b, h, _, d = q.shape
        q = jnp.pad(q, ((0, 0), (0, 0), (0, Lp - L), (0, 0)))
        return jnp.moveaxis(q.reshape(b, h, n_blk, Q_BLOCK, d), 2, 0)

    qpos = jnp.arange(Lp, dtype=jnp.int32).reshape(n_blk, Q_BLOCK)
    out = lax.map(lambda a: fn(a[0], *a[1]), (qpos, tuple(prep(q) for q in qs)))
    nb, b, h, qb, dv = out.shape
    out = jnp.moveaxis(out, 0, 2).reshape(b, h, Lp, dv)
    return out[:, :, :L]


def _rope_tables(L):
    inv_freq = 1.0 / (ROPE_THETA ** (jnp.arange(0, MLA_ROPE, 2, dtype=jnp.float32) / MLA_ROPE))
    ang = jnp.arange(L, dtype=jnp.float32)[:, None] * inv_freq[None, :]
    return jnp.cos(ang), jnp.sin(ang)


def _rope(x, cos, sin):
    half = x.shape[-1] // 2
    x1, x2 = x[..., :half], x[..., half:]
    c = cos.astype(x.dtype)
    s = sin.astype(x.dtype)
    return jnp.concatenate([x1 * c - x2 * s, x2 * c + x1 * s], axis=-1)


def _diff_attention(q, k, v, lq1, lk1, lq2, lk2, g_sub, lambda_init):
    B, L, _ = q.shape
    q = q.reshape(B, L, DA_HEADS, 2, DA_HEAD_DIM).transpose(0, 2, 1, 3, 4)
    k = k.reshape(B, L, DA_HEADS, 2, DA_HEAD_DIM).transpose(0, 2, 1, 3, 4)
    q1, q2 = q[..., 0, :], q[..., 1, :]
    k1, k2 = k[..., 0, :], k[..., 1, :]
    v = v.reshape(B, L, DA_HEADS, DA_V_DIM).transpose(0, 2, 1, 3)

    f32 = jnp.float32
    lam = (jnp.exp(jnp.sum(lq1.astype(f32) * lk1.astype(f32)))
           - jnp.exp(jnp.sum(lq2.astype(f32) * lk2.astype(f32))) + lambda_init)
    slopes = 2.0 ** (-8.0 * jnp.arange(1, DA_HEADS + 1, dtype=f32) / DA_HEADS)
    scale = DA_HEAD_DIM ** -0.5
    kpos = jnp.arange(L, dtype=jnp.int32)

    def blk(qp, q1b, q2b):
        mask = _chunk_mask(qp, kpos)
        dist = jnp.abs(qp[:, None] - kpos[None, :]).astype(f32)
        bias = jnp.where(mask[None], -slopes[:, None, None] * dist[None], NEG)
        s1 = jnp.einsum('bhqd,bhkd->bhqk', q1b, k1).astype(f32) * scale + bias
        s2 = jnp.einsum('bhqd,bhkd->bhqk', q2b, k2).astype(f32) * scale + bias
        a = jax.nn.softmax(s1, axis=-1) - lam * jax.nn.softmax(s2, axis=-1)
        return jnp.einsum('bhqk,bhkv->bhqv', a.astype(v.dtype), v)

    o = _sweep_query_blocks(blk, (q1, q2), L)
    o = _rms(o, g_sub) * (1.0 - lambda_init)
    return o.transpose(0, 2, 1, 3).reshape(B, L, DA_WIDTH)


def _mla(cq, ckv, kr, g_q, w_q_up, g_kv, w_kv_up, cos, sin):
    B, L, _ = cq.shape
    q = (_rms(cq, g_q) @ w_q_up).reshape(B, L, MLA_HEADS, MLA_NOPE + MLA_ROPE)
    q_nope = q[..., :MLA_NOPE]
    q_rope = _rope(q[..., MLA_NOPE:], cos[:, None, :], sin[:, None, :])
    kv = (_rms(ckv, g_kv) @ w_kv_up).reshape(B, L, MLA_HEADS, MLA_NOPE + MLA_V)
    k_nope, v = kv[..., :MLA_NOPE], kv[..., MLA_NOPE:]
    k_rope = _rope(kr, cos, sin)

    q = jnp.concatenate([q_nope, q_rope], axis=-1).transpose(0, 2, 1, 3)
    k = jnp.concatenate(
        [k_nope, jnp.broadcast_to(k_rope[:, :, None, :], (B, L, MLA_HEADS, MLA_ROPE))],
        axis=-1).transpose(0, 2, 1, 3)
    v = v.transpose(0, 2, 1, 3)
    scale = (MLA_NOPE + MLA_ROPE) ** -0.5
    kpos = jnp.arange(L, dtype=jnp.int32)

    def blk(qp, qb):
        bias = jnp.where(_chunk_mask(qp, kpos), 0.0, NEG).astype(jnp.float32)
        s = jnp.einsum('bhqd,bhkd->bhqk', qb, k).astype(jnp.float32) * scale + bias
        p = jax.nn.softmax(s, axis=-1)
        return jnp.einsum('bhqk,bhkv->bhqv', p.astype(v.dtype), v)

    o = _sweep_query_blocks(blk, (q,), L)
    return o.transpose(0, 2, 1, 3).reshape(B, L, MLA_WIDTH)


def setup_inputs(seed: int = 0) -> dict:
    key = jax.random.key(seed)
    ks = jax.random.split(key, 24)
    f32 = jnp.float32

    def nrm(k, shape, scale):
        return jax.random.normal(k, shape, f32) * scale

    def gain(k, shape):
        return 1.0 + 0.02 * jax.random.normal(k, shape, f32)

    return {
        "x": nrm(ks[0], (BATCH, SEQ, D_MODEL), 1.0),
        "meta_tokens": nrm(ks[1], (N_META, D_MODEL), 1.0),
        "g_attn_pre": gain(ks[2], (DEPTH, D_MODEL)),
        "w_in": nrm(ks[3], (DEPTH, D_MODEL, IN_WIDTH), D_MODEL ** -0.5),
        "da_lambda_q1": nrm(ks[4], (DEPTH, DA_HEAD_DIM), 0.1),
        "da_lambda_k1": nrm(ks[5], (DEPTH, DA_HEAD_DIM), 0.1),
        "da_lambda_q2": nrm(ks[6], (DEPTH, DA_HEAD_DIM), 0.1),
        "da_lambda_k2": nrm(ks[7], (DEPTH, DA_HEAD_DIM), 0.1),
        "g_da_sub": gain(ks[8], (DEPTH, DA_V_DIM)),
        "g_mla_q": gain(ks[9], (DEPTH, MLA_Q_RANK)),
        "w_mla_q_up": nrm(ks[10], (DEPTH, MLA_Q_RANK, MLA_HEADS * (MLA_NOPE + MLA_ROPE)), MLA_Q_RANK ** -0.5),
        "g_mla_kv": gain(ks[11], (DEPTH, MLA_KV_RANK)),
        "w_mla_kv_up": nrm(ks[12], (DEPTH, MLA_KV_RANK, MLA_HEADS * (MLA_NOPE + MLA_V)), MLA_KV_RANK ** -0.5),
        "w_o": nrm(ks[13], (DEPTH, MIX_WIDTH, D_MODEL), MIX_WIDTH ** -0.5),
        "g_attn_post": gain(ks[14], (DEPTH, D_MODEL)),
        "g_mlp_pre": gain(ks[15], (DEPTH, D_MODEL)),
        "w_ff1": nrm(ks[16], (DEPTH, D_MODEL, D_FF), D_MODEL ** -0.5),
        "w_ff2": nrm(ks[17], (DEPTH, D_FF, D_MODEL), D_FF ** -0.5),
        "g_mlp_post": gain(ks[18], (DEPTH, D_MODEL)),
    }


def reference(x, meta_tokens, g_attn_pre, w_in, da_lambda_q1, da_lambda_k1, da_lambda_q2,
              da_lambda_k2, g_da_sub, g_mla_q, w_mla_q_up, g_mla_kv, w_mla_kv_up, w_o,
              g_attn_post, g_mlp_pre, w_ff1, w_ff2, g_mlp_post):
    B = x.shape[0]
    meta = jnp.broadcast_to(meta_tokens.astype(x.dtype)[None], (B, N_META, D_MODEL))
    h = jnp.concatenate([meta, x], axis=1)
    L = h.shape[1]
    cos, sin = _rope_tables(L)

    for l in range(DEPTH):
        lambda_init = 0.8 - 0.6 * math.exp(-0.3 * l)
        u = _rms(h, g_attn_pre[l])
        proj = u @ w_in[l]
        q_da, k_da, v_da, cq, ckv, kr = jnp.split(proj, IN_SPLITS, axis=-1)
        y_da = _diff_attention(q_da, k_da, v_da, da_lambda_q1[l], da_lambda_k1[l],
                               da_lambda_q2[l], da_lambda_k2[l], g_da_sub[l], lambda_init)
        y_mla = _mla(cq, ckv, kr, g_mla_q[l], w_mla_q_up[l], g_mla_kv[l], w_mla_kv_up[l], cos, sin)
        mix = jnp.concatenate([y_da, y_mla], axis=-1) @ w_o[l]
        h = h + _rms(mix, g_attn_post[l])
        u = _rms(h, g_mlp_pre[l])
        f = jnp.square(jax.nn.relu(u @ w_ff1[l])) @ w_ff2[l]
        h = h + _rms(f, g_mlp_post[l])

    return h[:, N_META:]
```

```python
import functools
import math

import jax
import jax.numpy as jnp
import numpy as np
from jax import lax
from jax.experimental import pallas as pl
from jax.experimental.pallas import tpu as pltpu

F32 = jnp.float32
BF16 = jnp.bfloat16

D_MODEL = 1024
N_META = 16
CHUNK = 64
EPS = 1e-6
NEG = -1e30
ROPE_THETA = 10000.0

DA_HEADS = 4
DA_HEAD_DIM = 64
DA_V_DIM = 128
DA_WIDTH = DA_HEADS * DA_V_DIM
MLA_HEADS = 4
MLA_NOPE = 128
MLA_ROPE = 64
MLA_V = 128
MLA_WIDTH = MLA_HEADS * MLA_V
MLA_Q_RANK = 256
MLA_KV_RANK = 128
D_FF = 4 * D_MODEL
LAMBDA_INIT = 0.8 - 0.6 * math.exp(-0.3 * 0)

LANES = 128
KEY_W = 2 * LANES
VMEM_LIMIT = 56 * 1024 * 1024

PROJ_TM = 512
ATT_TQ = 512
ATT_TK = 512
OUT_TM = 512
FF_CHUNK = 1024
META_PAD = 128

C_QDA, C_KDA, C_VDA, C_CQ, C_CKV, C_KR, C_KRS, C_END = (
    0, 512, 1024, 1536, 1792, 1920, 2048, 2176)
Q_UP_W = 3 * LANES


def _rms_scale(x):
    return x * lax.rsqrt(jnp.mean(x * x, axis=-1, keepdims=True) + EPS)


def _dot(a, b):
    return jnp.dot(a, b, preferred_element_type=F32)


def _dot_nt(a, b):
    return lax.dot_general(a, b, (((1,), (1,)), ((), ())), preferred_element_type=F32)


def _proj_kernel(x_ref, g_ref, win_ref, gq_ref, wq_ref, gkv_ref, wkv_ref,
                 ct_ref, st_ref, posf_ref,
                 qda_ref, kda_ref, vda_ref, qm_ref, km_ref, vm_ref):
    x = x_ref[...]
    u = (_rms_scale(x) * g_ref[...]).astype(BF16)
    p = _dot(u, win_ref[...])
    ct = ct_ref[...]
    st = st_ref[...]

    qda_ref[...] = (p[:, C_QDA:C_KDA] * (DA_HEAD_DIM ** -0.5)).astype(BF16)
    posf = posf_ref[...]
    for h in range(DA_HEADS):
        kda_ref[:, KEY_W * h:KEY_W * h + LANES] = (
            p[:, C_KDA + LANES * h:C_KDA + LANES * (h + 1)].astype(BF16))
        kda_ref[:, KEY_W * h + LANES:KEY_W * (h + 1)] = posf
    vda_ref[...] = p[:, C_VDA:C_CQ].astype(BF16)

    k_rope = (p[:, C_KR:C_KRS] * ct + p[:, C_KRS:C_END] * st).astype(BF16)
    cq = (_rms_scale(p[:, C_CQ:C_CKV]) * gq_ref[...]).astype(BF16)
    qu = _dot(cq, wq_ref[...])
    ckv = (_rms_scale(p[:, C_CKV:C_KR]) * gkv_ref[...]).astype(BF16)
    kv = _dot(ckv, wkv_ref[...])
    scale = (MLA_NOPE + MLA_ROPE) ** -0.5
    for h in range(MLA_HEADS):
        b = Q_UP_W * h
        qm_ref[:, KEY_W * h:KEY_W * h + LANES] = (qu[:, b:b + LANES] * scale).astype(BF16)
        q_rope = qu[:, b + LANES:b + 2 * LANES] * ct + qu[:, b + 2 * LANES:b + 3 * LANES] * st
        qm_ref[:, KEY_W * h + LANES:KEY_W * (h + 1)] = (q_rope * scale).astype(BF16)
        km_ref[:, KEY_W * h:KEY_W * h + LANES] = kv[:, KEY_W * h:KEY_W * h + LANES].astype(BF16)
        km_ref[:, KEY_W * h + LANES:KEY_W * (h + 1)] = k_rope
        vm_ref[:, LANES * h:LANES * (h + 1)] = (
            kv[:, KEY_W * h + LANES:KEY_W * (h + 1)].astype(BF16))


def _proj_call(x2d, tm, n_tab_blocks, g_pre, w_in, g_q, w_q, g_kv, w_kv, ct, st, posf, name):
    rows = x2d.shape[0]
    grid = (rows // tm,)
    row = lambda i: (i, 0)
    const = lambda i: (0, 0)
    tab = lambda i: (i % n_tab_blocks, 0)
    widths = (DA_WIDTH, DA_HEADS * KEY_W, DA_WIDTH, MLA_HEADS * KEY_W, MLA_HEADS * KEY_W, MLA_WIDTH)
    return pl.pallas_call(
        _proj_kernel,
        grid=grid,
        in_specs=[
            pl.BlockSpec((tm, D_MODEL), row),
            pl.BlockSpec((1, D_MODEL), const),
            pl.BlockSpec((D_MODEL, C_END), const),
            pl.BlockSpec((1, MLA_Q_RANK), const),
            pl.BlockSpec((MLA_Q_RANK, MLA_HEADS * Q_UP_W), const),
            pl.BlockSpec((1, MLA_KV_RANK), const),
            pl.BlockSpec((MLA_KV_RANK, MLA_HEADS * KEY_W), const),
            pl.BlockSpec((tm, LANES), tab),
            pl.BlockSpec((tm, LANES), tab),
            pl.BlockSpec((tm, LANES), tab),
        ],
        out_specs=[pl.BlockSpec((tm, w), row) for w in widths],
        out_shape=[jax.ShapeDtypeStruct((rows, w), BF16) for w in widths],
        compiler_params=pltpu.CompilerParams(
            dimension_semantics=("arbitrary",), vmem_limit_bytes=VMEM_LIMIT),
        name=name,
    )(x2d, g_pre, w_in, g_q, w_q, g_kv, w_kv, ct, st, posf)


def _online_update(s, v, m_ref, l_ref, acc_ref):
    m_prev = m_ref[...]
    m_new = jnp.maximum(m_prev, jnp.max(s, axis=-1, keepdims=True))
    alpha = jnp.exp(m_prev - m_new)
    p = jnp.exp(s - m_new)
    l_ref[...] = alpha * l_ref[...] + jnp.sum(p, axis=-1, keepdims=True)
    acc_ref[...] = alpha * acc_ref[...] + _dot(p.astype(BF16), v)
    m_ref[...] = m_new


def _init_from(s, v, m_ref, l_ref, acc_ref):
    m = jnp.max(s, axis=-1, keepdims=True)
    p = jnp.exp(s - m)
    m_ref[...] = m
    l_ref[...] = jnp.sum(p, axis=-1, keepdims=True)
    acc_ref[...] = _dot(p.astype(BF16), v)


def _meta_mask(tq):
    return lax.broadcasted_iota(jnp.int32, (tq, META_PAD), 1) < N_META


def _diag_iotas(tq, tk):
    iq = lax.broadcasted_iota(jnp.int32, (tq, tk), 0)
    ik = lax.broadcasted_iota(jnp.int32, (tq, tk), 1)
    return iq, ik


def _da_kernel(q_ref, k_ref, v_ref, kmeta_ref, vmeta_ref, lam_ref, gsub_ref, y_ref,
               qa_ref, qb_ref, m1_ref, l1_ref, a1_ref, m2_ref, l2_ref, a2_ref):
    qi = pl.program_id(1)
    tq, tk = ATT_TQ, ATT_TK
    lane = lax.broadcasted_iota(jnp.int32, (tq, LANES), 1)
    lp = lam_ref[...]
    lam = (jnp.exp(jnp.sum(lp[0:1] * lp[1:2], axis=-1, keepdims=True))
           - jnp.exp(jnp.sum(lp[2:3] * lp[3:4], axis=-1, keepdims=True)) + LAMBDA_INIT)

    for h in range(DA_HEADS):
        slope = 2.0 ** (-8.0 * (h + 1) / DA_HEADS)
        q = q_ref[0, :, LANES * h:LANES * (h + 1)]
        pf = jnp.where(lane < 2, slope, 0.0).astype(BF16)
        qa_ref[:, 0:LANES] = jnp.where(lane < DA_HEAD_DIM, q, jnp.zeros_like(q))
        qa_ref[:, LANES:KEY_W] = pf
        qb_ref[:, 0:LANES] = jnp.where(lane >= DA_HEAD_DIM, q, jnp.zeros_like(q))
        qb_ref[:, LANES:KEY_W] = pf
        qa = qa_ref[...]
        qb = qb_ref[...]
        kcols = pl.ds(KEY_W * h, KEY_W)
        vcols = pl.ds(LANES * h, LANES)

        km = kmeta_ref[:, kcols]
        vm = vmeta_ref[:, vcols]
        mm = _meta_mask(tq)
        _init_from(jnp.where(mm, _dot_nt(qa, km), NEG), vm, m1_ref, l1_ref, a1_ref)
        _init_from(jnp.where(mm, _dot_nt(qb, km), NEG), vm, m2_ref, l2_ref, a2_ref)

        def body(j, carry):
            rows = pl.ds(pl.multiple_of(j * tk, tk), tk)
            kb = k_ref[0, rows, kcols]
            vb = v_ref[0, rows, vcols]
            _online_update(_dot_nt(qa, kb), vb, m1_ref, l1_ref, a1_ref)
            _online_update(_dot_nt(qb, kb), vb, m2_ref, l2_ref, a2_ref)
            return carry

        lax.fori_loop(0, qi, body, 0)

        rows = pl.ds(pl.multiple_of(qi * tk, tk), tk)
        kb = k_ref[0, rows, kcols]
        vb = v_ref[0, rows, vcols]
        iq, ik = _diag_iotas(tq, tk)
        visible = (ik // CHUNK) <= (iq // CHUNK)
        corr = (2.0 * slope) * jnp.minimum(iq - ik, 0).astype(F32)
        _online_update(jnp.where(visible, _dot_nt(qa, kb) + corr, NEG), vb, m1_ref, l1_ref, a1_ref)
        _online_update(jnp.where(visible, _dot_nt(qb, kb) + corr, NEG), vb, m2_ref, l2_ref, a2_ref)

        o = a1_ref[...] / l1_ref[...] - lam * (a2_ref[...] / l2_ref[...])
        o = _rms_scale(o) * gsub_ref[...] * (1.0 - LAMBDA_INIT)
        y_ref[0, :, LANES * h:LANES * (h + 1)] = o.astype(BF16)


def _da_call(q, k, v, kmeta, vmeta, lam_params, g_sub):
    nb, s, _ = q.shape
    tq = ATT_TQ
    stat = pltpu.VMEM((tq, 1), F32)
    acc = pltpu.VMEM((tq, DA_V_DIM), F32)
    return pl.pallas_call(
        _da_kernel,
        grid=(nb, s // tq),
        in_specs=[
            pl.BlockSpec((1, tq, DA_WIDTH), lambda b, i: (b, i, 0)),
            pl.BlockSpec((1, s, DA_HEADS * KEY_W), lambda b, i: (b, 0, 0)),
            pl.BlockSpec((1, s, DA_WIDTH), lambda b, i: (b, 0, 0)),
            pl.BlockSpec((META_PAD, DA_HEADS * KEY_W), lambda b, i: (0, 0)),
            pl.BlockSpec((META_PAD, DA_WIDTH), lambda b, i: (0, 0)),
            pl.BlockSpec((8, LANES), lambda b, i: (0, 0)),
            pl.BlockSpec((1, DA_V_DIM), lambda b, i: (0, 0)),
        ],
        out_specs=pl.BlockSpec((1, tq, DA_WIDTH), lambda b, i: (b, i, 0)),
        out_shape=jax.ShapeDtypeStruct((nb, s, DA_WIDTH), BF16),
        scratch_shapes=[pltpu.VMEM((tq, KEY_W), BF16), pltpu.VMEM((tq, KEY_W), BF16),
                        stat, stat, acc, stat, stat, acc],
        compiler_params=pltpu.CompilerParams(
            dimension_semantics=("arbitrary", "arbitrary"), vmem_limit_bytes=VMEM_LIMIT),
        name="diff_attention",
    )(q, k, v, kmeta, vmeta, lam_params, g_sub)


def _mla_kernel(q_ref, k_ref, v_ref, kmeta_ref, vmeta_ref, y_ref, m_ref, l_ref, a_ref):
    qi = pl.program_id(1)
    tq, tk = ATT_TQ, ATT_TK
    for h in range(MLA_HEADS):
        kcols = pl.ds(KEY_W * h, KEY_W)
        vcols = pl.ds(LANES * h, LANES)
        q = q_ref[0, :, kcols]

        s = jnp.where(_meta_mask(tq), _dot_nt(q, kmeta_ref[:, kcols]), NEG)
        _init_from(s, vmeta_ref[:, vcols], m_ref, l_ref, a_ref)

        def body(j, carry):
            rows = pl.ds(pl.multiple_of(j * tk, tk), tk)
            _online_update(_dot_nt(q, k_ref[0, rows, kcols]), v_ref[0, rows, vcols],
                           m_ref, l_ref, a_ref)
            return carry

        lax.fori_loop(0, qi, body, 0)

        rows = pl.ds(pl.multiple_of(qi * tk, tk), tk)
        iq, ik = _diag_iotas(tq, tk)
        visible = (ik // CHUNK) <= (iq // CHUNK)
        s = jnp.where(visible, _dot_nt(q, k_ref[0, rows, kcols]), NEG)
        _online_update(s, v_ref[0, rows, vcols], m_ref, l_ref, a_ref)

        y_ref[0, :, LANES * h:LANES * (h + 1)] = (a_ref[...] / l_ref[...]).astype(BF16)


def _mla_call(q, k, v, kmeta, vmeta):
    nb, s, _ = q.shape
    tq = ATT_TQ
    return pl.pallas_call(
        _mla_kernel,
        grid=(nb, s // tq),
        in_specs=[
            pl.BlockSpec((1, tq, MLA_HEADS * KEY_W), lambda b, i: (b, i, 0)),
            pl.BlockSpec((1, s, MLA_HEADS * KEY_W), lambda b, i: (b, 0, 0)),
            pl.BlockSpec((1, s, MLA_WIDTH), lambda b, i: (b, 0, 0)),
            pl.BlockSpec((META_PAD, MLA_HEADS * KEY_W), lambda b, i: (0, 0)),
            pl.BlockSpec((META_PAD, MLA_WIDTH), lambda b, i: (0, 0)),
        ],
        out_specs=pl.BlockSpec((1, tq, MLA_WIDTH), lambda b, i: (b, i, 0)),
        out_shape=jax.ShapeDtypeStruct((nb, s, MLA_WIDTH), BF16),
        scratch_shapes=[pltpu.VMEM((tq, 1), F32), pltpu.VMEM((tq, 1), F32),
                        pltpu.VMEM((tq, MLA_V), F32)],
        compiler_params=pltpu.CompilerParams(
            dimension_semantics=("arbitrary", "arbitrary"), vmem_limit_bytes=VMEM_LIMIT),
        name="mla_attention",
    )(q, k, v, kmeta, vmeta)


def _out_kernel(x_ref, yda_ref, ymla_ref, wo_ref, gpost_ref, gpre_ref, w1_ref, w2_ref,
                gmlp_ref, o_ref):
    mix = (_dot(yda_ref[...], wo_ref[0:DA_WIDTH, :])
           + _dot(ymla_ref[...], wo_ref[DA_WIDTH:DA_WIDTH + MLA_WIDTH, :]))
    h1 = x_ref[...] + _rms_scale(mix) * gpost_ref[...]
    u = (_rms_scale(h1) * gpre_ref[...]).astype(BF16)
    f = None
    for c in range(D_FF // FF_CHUNK):
        cols = slice(FF_CHUNK * c, FF_CHUNK * (c + 1))
        hid = jnp.square(jnp.maximum(_dot(u, w1_ref[:, cols]), 0.0)).astype(BF16)
        part = _dot(hid, w2_ref[cols, :])
        f = part if f is None else f + part
    o_ref[...] = h1 + _rms_scale(f) * gmlp_ref[...]


def _out_call(x2d, yda, ymla, w_o, g_post, g_pre, w1, w2, g_mlp):
    rows = x2d.shape[0]
    tm = OUT_TM
    row = lambda i: (i, 0)
    const = lambda i: (0, 0)
    single = pl.Buffered(1)
    return pl.pallas_call(
        _out_kernel,
        grid=(rows // tm,),
        in_specs=[
            pl.BlockSpec((tm, D_MODEL), row),
            pl.BlockSpec((tm, DA_WIDTH), row),
            pl.BlockSpec((tm, MLA_WIDTH), row),
            pl.BlockSpec((D_MODEL, D_MODEL), const, pipeline_mode=single),
            pl.BlockSpec((1, D_MODEL), const),
            pl.BlockSpec((1, D_MODEL), const),
            pl.BlockSpec((D_MODEL, D_FF), const, pipeline_mode=single),
            pl.BlockSpec((D_FF, D_MODEL), const, pipeline_mode=single),
            pl.BlockSpec((1, D_MODEL), const),
        ],
        out_specs=pl.BlockSpec((tm, D_MODEL), row),
        out_shape=jax.ShapeDtypeStruct((rows, D_MODEL), F32),
        compiler_params=pltpu.CompilerParams(
            dimension_semantics=("arbitrary",), vmem_limit_bytes=VMEM_LIMIT),
        name="out_mlp",
    )(x2d, yda, ymla, w_o, g_post, g_pre, w1, w2, g_mlp)


def _position_tables(n_pos):
    inv_freq = 1.0 / (ROPE_THETA ** (jnp.arange(0, MLA_ROPE, 2, dtype=F32) / MLA_ROPE))
    pos = jnp.arange(n_pos, dtype=F32)
    ang = pos[:, None] * inv_freq[None, :]
    cos, sin = jnp.cos(ang), jnp.sin(ang)
    zeros = jnp.zeros((n_pos, LANES - MLA_ROPE), F32)
    ct = jnp.concatenate([cos, cos, zeros], axis=1)
    st = jnp.concatenate([-sin, sin, zeros], axis=1)
    ipos = jnp.arange(n_pos, dtype=jnp.int32)
    posf = jnp.zeros((n_pos, LANES), F32)
    posf = posf.at[:, 0].set(((ipos // CHUNK) * CHUNK).astype(F32))
    posf = posf.at[:, 1].set((ipos % CHUNK).astype(F32))
    return ct, st, posf.astype(BF16)


def _swap_halves(w):
    half = w.shape[-1] // 2
    return jnp.concatenate([w[..., half:], w[..., :half]], axis=-1)


def kernel(x, meta_tokens, g_attn_pre, w_in, da_lambda_q1, da_lambda_k1, da_lambda_q2,
           da_lambda_k2, g_da_sub, g_mla_q, w_mla_q_up, g_mla_kv, w_mla_kv_up, w_o,
           g_attn_post, g_mlp_pre, w_ff1, w_ff2, g_mlp_post):
    nb, seq, d = x.shape
    assert d == D_MODEL and w_in.shape[0] == 1, "single-layer block only"
    assert seq % ATT_TQ == 0 and seq % PROJ_TM == 0 and ATT_TQ == ATT_TK

    w_in0 = w_in[0]
    kr = w_in0[:, C_KR:C_KR + MLA_ROPE]
    zpad = jnp.zeros((D_MODEL, LANES - MLA_ROPE), F32)
    w_in_ext = jnp.concatenate(
        [w_in0[:, :C_KR], kr, zpad, _swap_halves(kr), zpad], axis=1).astype(BF16)
    wq = w_mla_q_up[0].reshape(MLA_Q_RANK, MLA_HEADS, MLA_NOPE + MLA_ROPE)
    wq_rope = wq[..., MLA_NOPE:]
    zq = jnp.zeros((MLA_Q_RANK, MLA_HEADS, LANES - MLA_ROPE), F32)
    wq_ext = jnp.concatenate(
        [wq[..., :MLA_NOPE], wq_rope, zq, _swap_halves(wq_rope), zq], axis=-1
    ).reshape(MLA_Q_RANK, MLA_HEADS * Q_UP_W).astype(BF16)
    wkv = w_mla_kv_up[0].astype(BF16)

    ct, st, posf = _position_tables(N_META + seq)
    lam_params = jnp.zeros((8, LANES), F32)
    for r, vec in enumerate((da_lambda_q1, da_lambda_k1, da_lambda_q2, da_lambda_k2)):
        lam_params = lam_params.at[r, :DA_HEAD_DIM].set(vec[0].astype(F32))

    proj = functools.partial(
        _proj_call, g_pre=g_attn_pre, w_in=w_in_ext, g_q=g_mla_q, w_q=wq_ext,
        g_kv=g_mla_kv, w_kv=wkv)
    x2d = x.reshape(nb * seq, D_MODEL)
    qda, kda, vda, qm, km, vm = proj(
        x2d, PROJ_TM, seq // PROJ_TM, ct=ct[N_META:], st=st[N_META:], posf=posf[N_META:],
        name="proj_tokens")
    _, kda_meta, vda_meta, _, km_meta, vm_meta = proj(
        meta_tokens.astype(F32), N_META, 1, ct=ct[:N_META], st=st[:N_META], posf=posf[:N_META],
        name="proj_meta")

    def pad_meta(a):
        return jnp.pad(a, ((0, META_PAD - N_META), (0, 0)))

    def per_batch(a):
        return a.reshape(nb, seq, a.shape[-1])

    yda = _da_call(per_batch(qda), per_batch(kda), per_batch(vda),
                   pad_meta(kda_meta), pad_meta(vda_meta), lam_params, g_da_sub)
    ymla = _mla_call(per_batch(qm), per_batch(km), per_batch(vm),
                     pad_meta(km_meta), pad_meta(vm_meta))

    out = _out_call(x2d, yda.reshape(nb * seq, DA_WIDTH), ymla.reshape(nb * seq, MLA_WIDTH),
                    w_o[0].astype(BF16), g_attn_post, g_mlp_pre,
                    w_ff1[0].astype(BF16), w_ff2[0].astype(BF16), g_mlp_post)
    return out.reshape(nb, seq, D_MODEL)
```

```python
import functools
import math

import jax
import jax.numpy as jnp
import ml_dtypes
import numpy as np
from jax import lax
from jax.experimental import pallas as pl
from jax.experimental.pallas import tpu as pltpu

F32 = jnp.float32
BF16 = jnp.bfloat16

D_MODEL = 1024
N_META = 16
CHUNK = 64
EPS = 1e-6
NEG = -1e30
ROPE_THETA = 10000.0
LOG2E = math.log2(math.e)

DA_HEADS = 4
DA_HEAD_DIM = 64
DA_V_DIM = 128
DA_WIDTH = DA_HEADS * DA_V_DIM
MLA_HEADS = 4
MLA_NOPE = 128
MLA_ROPE = 64
MLA_V = 128
MLA_WIDTH = MLA_HEADS * MLA_V
MLA_Q_RANK = 256
MLA_KV_RANK = 128
D_FF = 4 * D_MODEL
LAMBDA_INIT = 0.8 - 0.6 * math.exp(-0.3 * 0)

LANES = 128
BF16_SUBLANES = 16
KEY_W = 2 * LANES
VT_ROWS = DA_V_DIM + BF16_SUBLANES
VMEM_LIMIT = 56 * 1024 * 1024

PROJ_TM = 512
ATT_TQ = 512
ATT_TK = 512
OUT_TM = 512
FF_CHUNK = 1024
META_PAD = 128

C_QDA, C_KDA, C_CQ, C_CKV, C_KR, C_KRS, C_END = (0, 512, 1024, 1280, 1408, 1536, 1664)
Q_UP_W = 3 * LANES
ALIBI_PIECES = 3


def _rms_scale(x):
    return x * lax.rsqrt(jnp.mean(x * x, axis=-1, keepdims=True) + EPS)


def _dot(a, b):
    return jnp.dot(a, b, preferred_element_type=F32)


def _dot_nt(a, b):
    return lax.dot_general(a, b, (((1,), (1,)), ((), ())), preferred_element_type=F32)


def _alibi_pieces(h):
    c = np.float32(2.0 ** (-8.0 * (h + 1) / DA_HEADS)) * np.float32(LOG2E)
    pieces, rest = [], c
    for _ in range(ALIBI_PIECES):
        piece = np.float32(ml_dtypes.bfloat16(rest))
        pieces.append(float(piece))
        rest = np.float32(rest - piece)
    assert rest == 0.0
    return float(c), pieces


def _proj_kernel(x_ref, g_ref, win_ref, wvt_ref, gq_ref, wq_ref, gkv_ref, wkv_ref, wkvt_ref,
                 ct_ref, st_ref, posf_ref,
                 qda_ref, kda_ref, vtda_ref, qm_ref, km_ref, vtm_ref):
    x = x_ref[...]
    u = (_rms_scale(x) * g_ref[...]).astype(BF16)
    p = _dot(u, win_ref[...])
    ct = ct_ref[...]
    st = st_ref[...]
    ones = jnp.ones((BF16_SUBLANES, x.shape[0]), BF16)

    qda_ref[...] = (p[:, C_QDA:C_KDA] * (DA_HEAD_DIM ** -0.5 * LOG2E)).astype(BF16)
    posf = posf_ref[...]
    vt = _dot_nt(wvt_ref[...], u)
    for h in range(DA_HEADS):
        kda_ref[:, KEY_W * h:KEY_W * h + LANES] = (
            p[:, C_KDA + LANES * h:C_KDA + LANES * (h + 1)].astype(BF16))
        kda_ref[:, KEY_W * h + LANES:KEY_W * (h + 1)] = posf
        vtda_ref[VT_ROWS * h:VT_ROWS * h + DA_V_DIM, :] = (
            vt[DA_V_DIM * h:DA_V_DIM * (h + 1), :].astype(BF16))
        vtda_ref[VT_ROWS * h + DA_V_DIM:VT_ROWS * (h + 1), :] = ones

    k_rope = (p[:, C_KR:C_KRS] * ct + p[:, C_KRS:C_END] * st).astype(BF16)
    cq = (_rms_scale(p[:, C_CQ:C_CKV]) * gq_ref[...]).astype(BF16)
    qu = _dot(cq, wq_ref[...])
    ckv = (_rms_scale(p[:, C_CKV:C_KR]) * gkv_ref[...]).astype(BF16)
    kn = _dot(ckv, wkv_ref[...])
    vtm = _dot_nt(wkvt_ref[...], ckv)
    scale = (MLA_NOPE + MLA_ROPE) ** -0.5 * LOG2E
    for h in range(MLA_HEADS):
        b = Q_UP_W * h
        qm_ref[:, KEY_W * h:KEY_W * h + LANES] = (qu[:, b:b + LANES] * scale).astype(BF16)
        q_rope = qu[:, b + LANES:b + 2 * LANES] * ct + qu[:, b + 2 * LANES:b + 3 * LANES] * st
        qm_ref[:, KEY_W * h + LANES:KEY_W * (h + 1)] = (q_rope * scale).astype(BF16)
        km_ref[:, KEY_W * h:KEY_W * h + LANES] = kn[:, LANES * h:LANES * (h + 1)].astype(BF16)
        km_ref[:, KEY_W * h + LANES:KEY_W * (h + 1)] = k_rope
        vtm_ref[VT_ROWS * h:VT_ROWS * h + MLA_V, :] = vtm[MLA_V * h:MLA_V * (h + 1), :].astype(BF16)
        vtm_ref[VT_ROWS * h + MLA_V:VT_ROWS * (h + 1), :] = ones


def _proj_call(x2d, tm, n_tab_blocks, g_pre, w_in, w_vt, g_q, w_q, g_kv, w_kv, w_kvt,
               ct, st, posf, name):
    rows = x2d.shape[0]
    grid = (rows // tm,)
    row = lambda i: (i, 0)
    col = lambda i: (0, i)
    const = lambda i: (0, 0)
    tab = lambda i: (i % n_tab_blocks, 0)
    row_out = lambda w: (pl.BlockSpec((tm, w), row), jax.ShapeDtypeStruct((rows, w), BF16))
    col_out = lambda r: (pl.BlockSpec((r, tm), col), jax.ShapeDtypeStruct((r, rows), BF16))
    outs = [row_out(DA_WIDTH), row_out(DA_HEADS * KEY_W), col_out(DA_HEADS * VT_ROWS),
            row_out(MLA_HEADS * KEY_W), row_out(MLA_HEADS * KEY_W), col_out(MLA_HEADS * VT_ROWS)]
    return pl.pallas_call(
        _proj_kernel,
        grid=grid,
        in_specs=[
            pl.BlockSpec((tm, D_MODEL), row),
            pl.BlockSpec((1, D_MODEL), const),
            pl.BlockSpec((D_MODEL, C_END), const),
            pl.BlockSpec((DA_WIDTH, D_MODEL), const),
            pl.BlockSpec((1, MLA_Q_RANK), const),
            pl.BlockSpec((MLA_Q_RANK, MLA_HEADS * Q_UP_W), const),
            pl.BlockSpec((1, MLA_KV_RANK), const),
            pl.BlockSpec((MLA_KV_RANK, MLA_HEADS * MLA_NOPE), const),
            pl.BlockSpec((MLA_WIDTH, MLA_KV_RANK), const),
            pl.BlockSpec((tm, LANES), tab),
            pl.BlockSpec((tm, LANES), tab),
            pl.BlockSpec((tm, LANES), tab),
        ],
        out_specs=[o[0] for o in outs],
        out_shape=[o[1] for o in outs],
        compiler_params=pltpu.CompilerParams(
            dimension_semantics=("arbitrary",), vmem_limit_bytes=VMEM_LIMIT),
        name=name,
    )(x2d, g_pre, w_in, w_vt, g_q, w_q, g_kv, w_kv, w_kvt, ct, st, posf)


def _softmax_step(s_t, vt, m_ref, acc_ref, first):
    m_blk = jnp.max(s_t, axis=0, keepdims=True)
    if first:
        m_new = m_blk
    else:
        m_prev = m_ref[...]
        m_new = jnp.maximum(m_prev, m_blk)
    p_t = jnp.exp2((s_t - m_new).astype(BF16))
    pv = _dot(vt, p_t)
    if first:
        acc_ref[...] = pv
    else:
        acc_ref[...] = jnp.exp2(m_prev - m_new) * acc_ref[...] + pv
    m_ref[...] = m_new


def _meta_rows_valid(tq):
    return lax.broadcasted_iota(jnp.int32, (META_PAD, tq), 0) < N_META


def _diag_iotas(tk, tq):
    ik = lax.broadcasted_iota(jnp.int32, (tk, tq), 0)
    iq = lax.broadcasted_iota(jnp.int32, (tk, tq), 1)
    return ik, iq


def _normalised(acc_ref, dv):
    return acc_ref[0:dv, :] / acc_ref[dv:dv + 1, :]


def _da_kernel(q_ref, k_ref, vt_ref, kmeta_ref, vtmeta_ref, lam_ref, gsub_ref, y_ref,
               qa_ref, qb_ref, m1_ref, a1_ref, m2_ref, a2_ref):
    qi = pl.program_id(1)
    tq, tk = ATT_TQ, ATT_TK
    lane = lax.broadcasted_iota(jnp.int32, (tq, LANES), 1)
    lp = lam_ref[...]
    lam = (jnp.exp(jnp.sum(lp[0:1] * lp[1:2], axis=-1, keepdims=True))
           - jnp.exp(jnp.sum(lp[2:3] * lp[3:4], axis=-1, keepdims=True)) + LAMBDA_INIT)

    for h in range(DA_HEADS):
        c, pieces = _alibi_pieces(h)
        q = q_ref[0, :, LANES * h:LANES * (h + 1)]
        pf = jnp.zeros((tq, LANES), F32)
        for i, piece in enumerate(pieces):
            pf = jnp.where((lane == 2 * i) | (lane == 2 * i + 1), piece, pf)
        pf = pf.astype(BF16)
        qa_ref[:, 0:LANES] = jnp.where(lane < DA_HEAD_DIM, q, jnp.zeros_like(q))
        qa_ref[:, LANES:KEY_W] = pf
        qb_ref[:, 0:LANES] = jnp.where(lane >= DA_HEAD_DIM, q, jnp.zeros_like(q))
        qb_ref[:, LANES:KEY_W] = pf
        qa = qa_ref[...]
        qb = qb_ref[...]
        kcols = pl.ds(KEY_W * h, KEY_W)
        vrows = pl.ds(VT_ROWS * h, VT_ROWS)

        km = kmeta_ref[:, kcols]
        vtm = vtmeta_ref[vrows, :]
        valid = _meta_rows_valid(tq)
        _softmax_step(jnp.where(valid, _dot_nt(km, qa), NEG), vtm, m1_ref, a1_ref, True)
        _softmax_step(jnp.where(valid, _dot_nt(km, qb), NEG), vtm, m2_ref, a2_ref, True)

        def body(j, carry):
            keys = pl.ds(pl.multiple_of(j * tk, tk), tk)
            kb = k_ref[0, keys, kcols]
            vtb = vt_ref[vrows, keys]
            _softmax_step(_dot_nt(kb, qa), vtb, m1_ref, a1_ref, False)
            _softmax_step(_dot_nt(kb, qb), vtb, m2_ref, a2_ref, False)
            return carry

        lax.fori_loop(0, qi, body, 0)

        keys = pl.ds(pl.multiple_of(qi * tk, tk), tk)
        kb = k_ref[0, keys, kcols]
        vtb = vt_ref[vrows, keys]
        ik, iq = _diag_iotas(tk, tq)
        visible = (ik // CHUNK) <= (iq // CHUNK)
        corr = (2.0 * c) * jnp.minimum(iq - ik, 0).astype(F32)
        _softmax_step(jnp.where(visible, _dot_nt(kb, qa) + corr, NEG), vtb, m1_ref, a1_ref, False)
        _softmax_step(jnp.where(visible, _dot_nt(kb, qb) + corr, NEG), vtb, m2_ref, a2_ref, False)

        o = _normalised(a1_ref, DA_V_DIM) - lam * _normalised(a2_ref, DA_V_DIM)
        o = o * lax.rsqrt(jnp.mean(o * o, axis=0, keepdims=True) + EPS)
        o = o * gsub_ref[...] * (1.0 - LAMBDA_INIT)
        y_ref[0, :, LANES * h:LANES * (h + 1)] = o.T.astype(BF16)


def _da_call(q, k, vt, kmeta, vtmeta, lam_params, g_sub_col):
    nb, s, _ = q.shape
    tq = ATT_TQ
    stat = pltpu.VMEM((1, tq), F32)
    acc = pltpu.VMEM((VT_ROWS, tq), F32)
    return pl.pallas_call(
        _da_kernel,
        grid=(nb, s // tq),
        in_specs=[
            pl.BlockSpec((1, tq, DA_WIDTH), lambda b, i: (b, i, 0)),
            pl.BlockSpec((1, s, DA_HEADS * KEY_W), lambda b, i: (b, 0, 0)),
            pl.BlockSpec((DA_HEADS * VT_ROWS, s), lambda b, i: (0, b)),
            pl.BlockSpec((META_PAD, DA_HEADS * KEY_W), lambda b, i: (0, 0)),
            pl.BlockSpec((DA_HEADS * VT_ROWS, META_PAD), lambda b, i: (0, 0)),
            pl.BlockSpec((8, LANES), lambda b, i: (0, 0)),
            pl.BlockSpec((DA_V_DIM, 1), lambda b, i: (0, 0)),
        ],
        out_specs=pl.BlockSpec((1, tq, DA_WIDTH), lambda b, i: (b, i, 0)),
        out_shape=jax.ShapeDtypeStruct((nb, s, DA_WIDTH), BF16),
        scratch_shapes=[pltpu.VMEM((tq, KEY_W), BF16), pltpu.VMEM((tq, KEY_W), BF16),
                        stat, acc, stat, acc],
        compiler_params=pltpu.CompilerParams(
            dimension_semantics=("arbitrary", "arbitrary"), vmem_limit_bytes=VMEM_LIMIT),
        name="diff_attention",
    )(q, k, vt, kmeta, vtmeta, lam_params, g_sub_col)


def _mla_kernel(q_ref, k_ref, vt_ref, kmeta_ref, vtmeta_ref, y_ref, m_ref, a_ref):
    qi = pl.program_id(1)
    tq, tk = ATT_TQ, ATT_TK
    for h in range(MLA_HEADS):
        kcols = pl.ds(KEY_W * h, KEY_W)
        vrows = pl.ds(VT_ROWS * h, VT_ROWS)
        q = q_ref[0, :, kcols]

        s_t = jnp.where(_meta_rows_valid(tq), _dot_nt(kmeta_ref[:, kcols], q), NEG)
        _softmax_step(s_t, vtmeta_ref[vrows, :], m_ref, a_ref, True)

        def body(j, carry):
            keys = pl.ds(pl.multiple_of(j * tk, tk), tk)
            _softmax_step(_dot_nt(k_ref[0, keys, kcols], q), vt_ref[vrows, keys],
                          m_ref, a_ref, False)
            return carry

        lax.fori_loop(0, qi, body, 0)

        keys = pl.ds(pl.multiple_of(qi * tk, tk), tk)
        ik, iq = _diag_iotas(tk, tq)
        visible = (ik // CHUNK) <= (iq // CHUNK)
        s_t = jnp.where(visible, _dot_nt(k_ref[0, keys, kcols], q), NEG)
        _softmax_step(s_t, vt_ref[vrows, keys], m_ref, a_ref, False)

        y_ref[0, :, LANES * h:LANES * (h + 1)] = _normalised(a_ref, MLA_V).T.astype(BF16)


def _mla_call(q, k, vt, kmeta, vtmeta):
    nb, s, _ = q.shape
    tq = ATT_TQ
    return pl.pallas_call(
        _mla_kernel,
        grid=(nb, s // tq),
        in_specs=[
            pl.BlockSpec((1, tq, MLA_HEADS * KEY_W), lambda b, i: (b, i, 0)),
            pl.BlockSpec((1, s, MLA_HEADS * KEY_W), lambda b, i: (b, 0, 0)),
            pl.BlockSpec((MLA_HEADS * VT_ROWS, s), lambda b, i: (0, b)),
            pl.BlockSpec((META_PAD, MLA_HEADS * KEY_W), lambda b, i: (0, 0)),
            pl.BlockSpec((MLA_HEADS * VT_ROWS, META_PAD), lambda b, i: (0, 0)),
        ],
        out_specs=pl.BlockSpec((1, tq, MLA_WIDTH), lambda b, i: (b, i, 0)),
        out_shape=jax.ShapeDtypeStruct((nb, s, MLA_WIDTH), BF16),
        scratch_shapes=[pltpu.VMEM((1, tq), F32), pltpu.VMEM((VT_ROWS, tq), F32)],
        compiler_params=pltpu.CompilerParams(
            dimension_semantics=("arbitrary", "arbitrary"), vmem_limit_bytes=VMEM_LIMIT),
        name="mla_attention",
    )(q, k, vt, kmeta, vtmeta)


def _out_kernel(x_ref, yda_ref, ymla_ref, wo_ref, gpost_ref, gpre_ref, w1_ref, w2_ref,
                gmlp_ref, o_ref):
    mix = (_dot(yda_ref[...], wo_ref[0:DA_WIDTH, :])
           + _dot(ymla_ref[...], wo_ref[DA_WIDTH:DA_WIDTH + MLA_WIDTH, :]))
    h1 = x_ref[...] + _rms_scale(mix) * gpost_ref[...]
    u = (_rms_scale(h1) * gpre_ref[...]).astype(BF16)
    f = None
    for c in range(D_FF // FF_CHUNK):
        cols = slice(FF_CHUNK * c, FF_CHUNK * (c + 1))
        hid = jnp.square(jnp.maximum(_dot(u, w1_ref[:, cols]), 0.0)).astype(BF16)
        part = _dot(hid, w2_ref[cols, :])
        f = part if f is None else f + part
    o_ref[...] = h1 + _rms_scale(f) * gmlp_ref[...]


def _out_call(x2d, yda, ymla, w_o, g_post, g_pre, w1, w2, g_mlp):
    rows = x2d.shape[0]
    tm = OUT_TM
    row = lambda i: (i, 0)
    const = lambda i: (0, 0)
    single = pl.Buffered(1)
    return pl.pallas_call(
        _out_kernel,
        grid=(rows // tm,),
        in_specs=[
            pl.BlockSpec((tm, D_MODEL), row),
            pl.BlockSpec((tm, DA_WIDTH), row),
            pl.BlockSpec((tm, MLA_WIDTH), row),
            pl.BlockSpec((D_MODEL, D_MODEL), const, pipeline_mode=single),
            pl.BlockSpec((1, D_MODEL), const),
            pl.BlockSpec((1, D_MODEL), const),
            pl.BlockSpec((D_MODEL, D_FF), const, pipeline_mode=single),
            pl.BlockSpec((D_FF, D_MODEL), const, pipeline_mode=single),
            pl.BlockSpec((1, D_MODEL), const),
        ],
        out_specs=pl.BlockSpec((tm, D_MODEL), row),
        out_shape=jax.ShapeDtypeStruct((rows, D_MODEL), F32),
        compiler_params=pltpu.CompilerParams(
            dimension_semantics=("arbitrary",), vmem_limit_bytes=VMEM_LIMIT),
        name="out_mlp",
    )(x2d, yda, ymla, w_o, g_post, g_pre, w1, w2, g_mlp)


def _position_tables(n_pos):
    inv_freq = 1.0 / (ROPE_THETA ** (jnp.arange(0, MLA_ROPE, 2, dtype=F32) / MLA_ROPE))
    pos = jnp.arange(n_pos, dtype=F32)
    ang = pos[:, None] * inv_freq[None, :]
    cos, sin = jnp.cos(ang), jnp.sin(ang)
    zeros = jnp.zeros((n_pos, LANES - MLA_ROPE), F32)
    ct = jnp.concatenate([cos, cos, zeros], axis=1)
    st = jnp.concatenate([-sin, sin, zeros], axis=1)
    ipos = jnp.arange(n_pos, dtype=jnp.int32)
    hi = ((ipos // CHUNK) * CHUNK).astype(F32)[:, None]
    lo = (ipos % CHUNK).astype(F32)[:, None]
    posf = jnp.concatenate([hi, lo] * ALIBI_PIECES
                           + [jnp.zeros((n_pos, LANES - 2 * ALIBI_PIECES), F32)], axis=1)
    return ct, st, posf.astype(BF16)


def _swap_halves(w):
    half = w.shape[-1] // 2
    return jnp.concatenate([w[..., half:], w[..., :half]], axis=-1)


def kernel(x, meta_tokens, g_attn_pre, w_in, da_lambda_q1, da_lambda_k1, da_lambda_q2,
           da_lambda_k2, g_da_sub, g_mla_q, w_mla_q_up, g_mla_kv, w_mla_kv_up, w_o,
           g_attn_post, g_mlp_pre, w_ff1, w_ff2, g_mlp_post):
    nb, seq, d = x.shape
    assert d == D_MODEL and w_in.shape[0] == 1, "single-layer block only"
    assert seq % ATT_TQ == 0 and seq % PROJ_TM == 0 and ATT_TQ == ATT_TK

    w_in0 = w_in[0]
    n_da = 2 * DA_WIDTH
    v_cols = slice(n_da, n_da + DA_WIDTH)
    rest = w_in0[:, n_da + DA_WIDTH:]
    kr = rest[:, MLA_Q_RANK + MLA_KV_RANK:]
    zpad = jnp.zeros((D_MODEL, LANES - MLA_ROPE), F32)
    w_in_ext = jnp.concatenate(
        [w_in0[:, :n_da], rest[:, :MLA_Q_RANK + MLA_KV_RANK], kr, zpad, _swap_halves(kr), zpad],
        axis=1).astype(BF16)
    w_vt = w_in0[:, v_cols].T.astype(BF16)
    wq = w_mla_q_up[0].reshape(MLA_Q_RANK, MLA_HEADS, MLA_NOPE + MLA_ROPE)
    wq_rope = wq[..., MLA_NOPE:]
    zq = jnp.zeros((MLA_Q_RANK, MLA_HEADS, LANES - MLA_ROPE), F32)
    wq_ext = jnp.concatenate(
        [wq[..., :MLA_NOPE], wq_rope, zq, _swap_halves(wq_rope), zq], axis=-1
    ).reshape(MLA_Q_RANK, MLA_HEADS * Q_UP_W).astype(BF16)
    wkv = w_mla_kv_up[0].reshape(MLA_KV_RANK, MLA_HEADS, MLA_NOPE + MLA_V)
    wkv_k = wkv[..., :MLA_NOPE].reshape(MLA_KV_RANK, MLA_HEADS * MLA_NOPE).astype(BF16)
    wkv_vt = wkv[..., MLA_NOPE:].reshape(MLA_KV_RANK, MLA_WIDTH).T.astype(BF16)

    ct, st, posf = _position_tables(N_META + seq)
    lam_params = jnp.zeros((8, LANES), F32)
    for r, vec in enumerate((da_lambda_q1, da_lambda_k1, da_lambda_q2, da_lambda_k2)):
        lam_params = lam_params.at[r, :DA_HEAD_DIM].set(vec[0].astype(F32))

    proj = functools.partial(
        _proj_call, g_pre=g_attn_pre, w_in=w_in_ext, w_vt=w_vt, g_q=g_mla_q, w_q=wq_ext,
        g_kv=g_mla_kv, w_kv=wkv_k, w_kvt=wkv_vt)
    x2d = x.reshape(nb * seq, D_MODEL)
    qda, kda, vtda, qm, km, vtm = proj(
        x2d, PROJ_TM, seq // PROJ_TM, ct=ct[N_META:], st=st[N_META:], posf=posf[N_META:],
        name="proj_tokens")
    _, kda_meta, vtda_meta, _, km_meta, vtm_meta = proj(
        meta_tokens.astype(F32), N_META, 1, ct=ct[:N_META], st=st[:N_META], posf=posf[:N_META],
        name="proj_meta")

    def pad_rows(a):
        return jnp.pad(a, ((0, META_PAD - N_META), (0, 0)))

    def pad_cols(a):
        return jnp.pad(a, ((0, 0), (0, META_PAD - N_META)))

    def per_batch(a):
        return a.reshape(nb, seq, a.shape[-1])

    yda = _da_call(per_batch(qda), per_batch(kda), vtda, pad_rows(kda_meta), pad_cols(vtda_meta),
                   lam_params, g_da_sub.reshape(DA_V_DIM, 1))
    ymla = _mla_call(per_batch(qm), per_batch(km), vtm, pad_rows(km_meta), pad_cols(vtm_meta))

    out = _out_call(x2d, yda.reshape(nb * seq, DA_WIDTH), ymla.reshape(nb * seq, MLA_WIDTH),
                    w_o[0].astype(BF16), g_attn_post, g_mlp_pre,
                    w_ff1[0].astype(BF16), w_ff2[0].astype(BF16), g_mlp_post)
    return out.reshape(nb, seq, D_MODEL)
```

```python
import functools
import math

import jax
import jax.numpy as jnp
import ml_dtypes
import numpy as np
from jax import lax
from jax.experimental import pallas as pl
from jax.experimental.pallas import tpu as pltpu

F32 = jnp.float32
BF16 = jnp.bfloat16

D_MODEL = 1024
N_META = 16
CHUNK = 64
EPS = 1e-6
NEG = -1e30
ROPE_THETA = 10000.0
LOG2E = math.log2(math.e)

DA_HEADS = 4
DA_HEAD_DIM = 64
DA_V_DIM = 128
DA_WIDTH = DA_HEADS * DA_V_DIM
MLA_HEADS = 4
MLA_NOPE = 128
MLA_ROPE = 64
MLA_V = 128
MLA_WIDTH = MLA_HEADS * MLA_V
MLA_Q_RANK = 256
MLA_KV_RANK = 128
D_FF = 4 * D_MODEL
LAMBDA_INIT = 0.8 - 0.6 * math.exp(-0.3 * 0)

LANES = 128
BF16_SUBLANES = 16
KEY_W = 2 * LANES
VT_ROWS = DA_V_DIM + BF16_SUBLANES
VMEM_LIMIT = 56 * 1024 * 1024

PROJ_TM = 512
ATT_TQ = 512
ATT_TK = 512
OUT_TM = 512
FF_CHUNK = 1024
META_PAD = 128

C_QDA, C_KDA, C_CQ, C_CKV, C_KR, C_KRS, C_END = (0, 512, 1024, 1280, 1408, 1536, 1664)
Q_UP_W = 3 * LANES
ALIBI_PIECES = 3


def _rms_scale(x):
    return x * lax.rsqrt(jnp.mean(x * x, axis=-1, keepdims=True) + EPS)


def _dot(a, b):
    return jnp.dot(a, b, preferred_element_type=F32)


def _dot_nt(a, b):
    return lax.dot_general(a, b, (((1,), (1,)), ((), ())), preferred_element_type=F32)


def _alibi_pieces(h):
    c = np.float32(2.0 ** (-8.0 * (h + 1) / DA_HEADS)) * np.float32(LOG2E)
    pieces, rest = [], c
    for _ in range(ALIBI_PIECES):
        piece = np.float32(ml_dtypes.bfloat16(rest))
        pieces.append(float(piece))
        rest = np.float32(rest - piece)
    assert rest == 0.0
    return float(c), pieces


def _proj_kernel(x_ref, g_ref, win_ref, wvt_ref, gq_ref, wq_ref, gkv_ref, wkv_ref, wkvt_ref,
                 ct_ref, st_ref, posf_ref,
                 qda_ref, kda_ref, vtda_ref, qm_ref, km_ref, vtm_ref):
    x = x_ref[...]
    u = (_rms_scale(x) * g_ref[...]).astype(BF16)
    p = _dot(u, win_ref[...])
    ct = ct_ref[...]
    st = st_ref[...]
    ones = jnp.ones((BF16_SUBLANES, x.shape[0]), BF16)

    qda_ref[...] = (p[:, C_QDA:C_KDA] * (DA_HEAD_DIM ** -0.5 * LOG2E)).astype(BF16)
    posf = posf_ref[...]
    vt = _dot_nt(wvt_ref[...], u)
    for h in range(DA_HEADS):
        kda_ref[:, KEY_W * h:KEY_W * h + LANES] = (
            p[:, C_KDA + LANES * h:C_KDA + LANES * (h + 1)].astype(BF16))
        kda_ref[:, KEY_W * h + LANES:KEY_W * (h + 1)] = posf
        vtda_ref[VT_ROWS * h:VT_ROWS * h + DA_V_DIM, :] = (
            vt[DA_V_DIM * h:DA_V_DIM * (h + 1), :].astype(BF16))
        vtda_ref[VT_ROWS * h + DA_V_DIM:VT_ROWS * (h + 1), :] = ones

    k_rope = (p[:, C_KR:C_KRS] * ct + p[:, C_KRS:C_END] * st).astype(BF16)
    cq = (_rms_scale(p[:, C_CQ:C_CKV]) * gq_ref[...]).astype(BF16)
    qu = _dot(cq, wq_ref[...])
    ckv = (_rms_scale(p[:, C_CKV:C_KR]) * gkv_ref[...]).astype(BF16)
    kn = _dot(ckv, wkv_ref[...])
    vtm = _dot_nt(wkvt_ref[...], ckv)
    scale = (MLA_NOPE + MLA_ROPE) ** -0.5 * LOG2E
    for h in range(MLA_HEADS):
        b = Q_UP_W * h
        qm_ref[:, KEY_W * h:KEY_W * h + LANES] = (qu[:, b:b + LANES] * scale).astype(BF16)
        q_rope = qu[:, b + LANES:b + 2 * LANES] * ct + qu[:, b + 2 * LANES:b + 3 * LANES] * st
        qm_ref[:, KEY_W * h + LANES:KEY_W * (h + 1)] = (q_rope * scale).astype(BF16)
        km_ref[:, KEY_W * h:KEY_W * h + LANES] = kn[:, LANES * h:LANES * (h + 1)].astype(BF16)
        km_ref[:, KEY_W * h + LANES:KEY_W * (h + 1)] = k_rope
        vtm_ref[VT_ROWS * h:VT_ROWS * h + MLA_V, :] = vtm[MLA_V * h:MLA_V * (h + 1), :].astype(BF16)
        vtm_ref[VT_ROWS * h + MLA_V:VT_ROWS * (h + 1), :] = ones


def _proj_call(x2d, tm, n_tab_blocks, g_pre, w_in, w_vt, g_q, w_q, g_kv, w_kv, w_kvt,
               ct, st, posf, name):
    rows = x2d.shape[0]
    grid = (rows // tm,)
    row = lambda i: (i, 0)
    col = lambda i: (0, i)
    const = lambda i: (0, 0)
    tab = lambda i: (i % n_tab_blocks, 0)
    row_out = lambda w: (pl.BlockSpec((tm, w), row), jax.ShapeDtypeStruct((rows, w), BF16))
    col_out = lambda r: (pl.BlockSpec((r, tm), col), jax.ShapeDtypeStruct((r, rows), BF16))
    outs = [row_out(DA_WIDTH), row_out(DA_HEADS * KEY_W), col_out(DA_HEADS * VT_ROWS),
            row_out(MLA_HEADS * KEY_W), row_out(MLA_HEADS * KEY_W), col_out(MLA_HEADS * VT_ROWS)]
    return pl.pallas_call(
        _proj_kernel,
        grid=grid,
        in_specs=[
            pl.BlockSpec((tm, D_MODEL), row),
            pl.BlockSpec((1, D_MODEL), const),
            pl.BlockSpec((D_MODEL, C_END), const),
            pl.BlockSpec((DA_WIDTH, D_MODEL), const),
            pl.BlockSpec((1, MLA_Q_RANK), const),
            pl.BlockSpec((MLA_Q_RANK, MLA_HEADS * Q_UP_W), const),
            pl.BlockSpec((1, MLA_KV_RANK), const),
            pl.BlockSpec((MLA_KV_RANK, MLA_HEADS * MLA_NOPE), const),
            pl.BlockSpec((MLA_WIDTH, MLA_KV_RANK), const),
            pl.BlockSpec((tm, LANES), tab),
            pl.BlockSpec((tm, LANES), tab),
            pl.BlockSpec((tm, LANES), tab),
        ],
        out_specs=[o[0] for o in outs],
        out_shape=[o[1] for o in outs],
        compiler_params=pltpu.CompilerParams(
            dimension_semantics=("arbitrary",), vmem_limit_bytes=VMEM_LIMIT),
        name=name,
    )(x2d, g_pre, w_in, w_vt, g_q, w_q, g_kv, w_kv, w_kvt, ct, st, posf)


def _softmax_step(s_t, vt, m_ref, acc_ref, first):
    m_blk = jnp.max(s_t, axis=0, keepdims=True)
    if first:
        m_new = m_blk
    else:
        m_prev = m_ref[...]
        m_new = jnp.maximum(m_prev, m_blk)
    p_t = jnp.exp2((s_t - m_new).astype(BF16))
    pv = _dot(vt, p_t)
    if first:
        acc_ref[...] = pv
    else:
        acc_ref[...] = jnp.exp2(m_prev - m_new) * acc_ref[...] + pv
    m_ref[...] = m_new


def _meta_rows_valid(tq):
    return lax.broadcasted_iota(jnp.int32, (META_PAD, tq), 0) < N_META


def _diag_iotas(tk, tq):
    ik = lax.broadcasted_iota(jnp.int32, (tk, tq), 0)
    iq = lax.broadcasted_iota(jnp.int32, (tk, tq), 1)
    return ik, iq


def _normalised(acc_ref, dv):
    return acc_ref[0:dv, :] / acc_ref[dv:dv + 1, :]


def _attend(qi, queries, kmeta, vtmeta, k_ref, kcols, vt_ref, vrows, s_bufs, m_refs, acc_refs,
            diag_fn):
    tk = ATT_TK
    buf_a, buf_b = s_bufs
    valid = _meta_rows_valid(ATT_TQ)
    s_meta = [jnp.where(valid, _dot_nt(kmeta, q), NEG) for q in queries]

    def key_block(j):
        return pl.ds(pl.multiple_of(j * tk, tk), tk)

    def scores_into(buf, j):
        kb = k_ref[0, key_block(j), kcols]
        for c, q in enumerate(queries):
            buf[c] = _dot_nt(kb, q)

    def consume(buf, j, diagonal):
        vtb = vt_ref[vrows, key_block(j)]
        for c in range(len(queries)):
            s_t = buf[c]
            if diagonal:
                s_t = diag_fn(s_t)
            _softmax_step(s_t, vtb, m_refs[c], acc_refs[c], False)

    scores_into(buf_a, 0)
    for c in range(len(queries)):
        _softmax_step(s_meta[c], vtmeta, m_refs[c], acc_refs[c], True)

    def pair(t, carry):
        j = 2 * t
        scores_into(buf_b, j + 1)
        consume(buf_a, j, False)
        scores_into(buf_a, j + 2)
        consume(buf_b, j + 1, False)
        return carry

    lax.fori_loop(0, qi // 2, pair, 0)

    @pl.when(qi % 2 == 0)
    def _():
        consume(buf_a, qi, True)

    @pl.when(qi % 2 == 1)
    def _():
        scores_into(buf_b, qi)
        consume(buf_a, qi - 1, False)
        consume(buf_b, qi, True)


def _da_kernel(q_ref, k_ref, vt_ref, kmeta_ref, vtmeta_ref, lam_ref, gsub_ref, y_ref,
               qa_ref, qb_ref, m1_ref, a1_ref, m2_ref, a2_ref, sa_ref, sb_ref):
    qi = pl.program_id(1)
    tq, tk = ATT_TQ, ATT_TK
    lane = lax.broadcasted_iota(jnp.int32, (tq, LANES), 1)
    lp = lam_ref[...]
    lam = (jnp.exp(jnp.sum(lp[0:1] * lp[1:2], axis=-1, keepdims=True))
           - jnp.exp(jnp.sum(lp[2:3] * lp[3:4], axis=-1, keepdims=True)) + LAMBDA_INIT)

    for h in range(DA_HEADS):
        c, pieces = _alibi_pieces(h)
        q = q_ref[0, :, LANES * h:LANES * (h + 1)]
        pf = jnp.zeros((tq, LANES), F32)
        for i, piece in enumerate(pieces):
            pf = jnp.where((lane == 2 * i) | (lane == 2 * i + 1), piece, pf)
        pf = pf.astype(BF16)
        qa_ref[:, 0:LANES] = jnp.where(lane < DA_HEAD_DIM, q, jnp.zeros_like(q))
        qa_ref[:, LANES:KEY_W] = pf
        qb_ref[:, 0:LANES] = jnp.where(lane >= DA_HEAD_DIM, q, jnp.zeros_like(q))
        qb_ref[:, LANES:KEY_W] = pf
        qa = qa_ref[...]
        qb = qb_ref[...]
        kcols = pl.ds(KEY_W * h, KEY_W)
        vrows = pl.ds(VT_ROWS * h, VT_ROWS)

        def diag_fn(s_t, c=c):
            ik, iq = _diag_iotas(tk, tq)
            visible = (ik // CHUNK) <= (iq // CHUNK)
            corr = (2.0 * c) * jnp.minimum(iq - ik, 0).astype(F32)
            return jnp.where(visible, s_t + corr, NEG)

        _attend(qi, (qa, qb), kmeta_ref[:, kcols], vtmeta_ref[vrows, :], k_ref, kcols,
                vt_ref, vrows, (sa_ref, sb_ref), (m1_ref, m2_ref), (a1_ref, a2_ref), diag_fn)

        o = _normalised(a1_ref, DA_V_DIM) - lam * _normalised(a2_ref, DA_V_DIM)
        o = o * lax.rsqrt(jnp.mean(o * o, axis=0, keepdims=True) + EPS)
        o = o * gsub_ref[...] * (1.0 - LAMBDA_INIT)
        y_ref[0, :, LANES * h:LANES * (h + 1)] = o.T.astype(BF16)


def _da_call(q, k, vt, kmeta, vtmeta, lam_params, g_sub_col):
    nb, s, _ = q.shape
    tq = ATT_TQ
    stat = pltpu.VMEM((1, tq), F32)
    acc = pltpu.VMEM((VT_ROWS, tq), F32)
    sbuf = pltpu.VMEM((2, ATT_TK, tq), F32)
    return pl.pallas_call(
        _da_kernel,
        grid=(nb, s // tq),
        in_specs=[
            pl.BlockSpec((1, tq, DA_WIDTH), lambda b, i: (b, i, 0)),
            pl.BlockSpec((1, s, DA_HEADS * KEY_W), lambda b, i: (b, 0, 0)),
            pl.BlockSpec((DA_HEADS * VT_ROWS, s), lambda b, i: (0, b)),
            pl.BlockSpec((META_PAD, DA_HEADS * KEY_W), lambda b, i: (0, 0)),
            pl.BlockSpec((DA_HEADS * VT_ROWS, META_PAD), lambda b, i: (0, 0)),
            pl.BlockSpec((8, LANES), lambda b, i: (0, 0)),
            pl.BlockSpec((DA_V_DIM, 1), lambda b, i: (0, 0)),
        ],
        out_specs=pl.BlockSpec((1, tq, DA_WIDTH), lambda b, i: (b, i, 0)),
        out_shape=jax.ShapeDtypeStruct((nb, s, DA_WIDTH), BF16),
        scratch_shapes=[pltpu.VMEM((tq, KEY_W), BF16), pltpu.VMEM((tq, KEY_W), BF16),
                        stat, acc, stat, acc, sbuf, sbuf],
        compiler_params=pltpu.CompilerParams(
            dimension_semantics=("arbitrary", "arbitrary"), vmem_limit_bytes=VMEM_LIMIT),
        name="diff_attention",
    )(q, k, vt, kmeta, vtmeta, lam_params, g_sub_col)


def _mla_kernel(q_ref, k_ref, vt_ref, kmeta_ref, vtmeta_ref, y_ref, m_ref, a_ref,
                sa_ref, sb_ref):
    qi = pl.program_id(1)
    tq, tk = ATT_TQ, ATT_TK
    for h in range(MLA_HEADS):
        kcols = pl.ds(KEY_W * h, KEY_W)
        vrows = pl.ds(VT_ROWS * h, VT_ROWS)
        q = q_ref[0, :, kcols]

        def diag_fn(s_t):
            ik, iq = _diag_iotas(tk, tq)
            return jnp.where((ik // CHUNK) <= (iq // CHUNK), s_t, NEG)

        _attend(qi, (q,), kmeta_ref[:, kcols], vtmeta_ref[vrows, :], k_ref, kcols,
                vt_ref, vrows, (sa_ref, sb_ref), (m_ref,), (a_ref,), diag_fn)

        y_ref[0, :, LANES * h:LANES * (h + 1)] = _normalised(a_ref, MLA_V).T.astype(BF16)


def _mla_call(q, k, vt, kmeta, vtmeta):
    nb, s, _ = q.shape
    tq = ATT_TQ
    return pl.pallas_call(
        _mla_kernel,
        grid=(nb, s // tq),
        in_specs=[
            pl.BlockSpec((1, tq, MLA_HEADS * KEY_W), lambda b, i: (b, i, 0)),
            pl.BlockSpec((1, s, MLA_HEADS * KEY_W), lambda b, i: (b, 0, 0)),
            pl.BlockSpec((MLA_HEADS * VT_ROWS, s), lambda b, i: (0, b)),
            pl.BlockSpec((META_PAD, MLA_HEADS * KEY_W), lambda b, i: (0, 0)),
            pl.BlockSpec((MLA_HEADS * VT_ROWS, META_PAD), lambda b, i: (0, 0)),
        ],
        out_specs=pl.BlockSpec((1, tq, MLA_WIDTH), lambda b, i: (b, i, 0)),
        out_shape=jax.ShapeDtypeStruct((nb, s, MLA_WIDTH), BF16),
        scratch_shapes=[pltpu.VMEM((1, tq), F32), pltpu.VMEM((VT_ROWS, tq), F32),
                        pltpu.VMEM((1, ATT_TK, tq), F32), pltpu.VMEM((1, ATT_TK, tq), F32)],
        compiler_params=pltpu.CompilerParams(
            dimension_semantics=("arbitrary", "arbitrary"), vmem_limit_bytes=VMEM_LIMIT),
        name="mla_attention",
    )(q, k, vt, kmeta, vtmeta)


def _out_kernel(x_ref, yda_ref, ymla_ref, wo_ref, gpost_ref, gpre_ref, w1_ref, w2_ref,
                gmlp_ref, o_ref):
    mix = (_dot(yda_ref[...], wo_ref[0:DA_WIDTH, :])
           + _dot(ymla_ref[...], wo_ref[DA_WIDTH:DA_WIDTH + MLA_WIDTH, :]))
    h1 = x_ref[...] + _rms_scale(mix) * gpost_ref[...]
    u = (_rms_scale(h1) * gpre_ref[...]).astype(BF16)
    f = None
    for c in range(D_FF // FF_CHUNK):
        cols = slice(FF_CHUNK * c, FF_CHUNK * (c + 1))
        hid = jnp.square(jnp.maximum(_dot(u, w1_ref[:, cols]), 0.0)).astype(BF16)
        part = _dot(hid, w2_ref[cols, :])
        f = part if f is None else f + part
    o_ref[...] = h1 + _rms_scale(f) * gmlp_ref[...]


def _out_call(x2d, yda, ymla, w_o, g_post, g_pre, w1, w2, g_mlp):
    rows = x2d.shape[0]
    tm = OUT_TM
    row = lambda i: (i, 0)
    const = lambda i: (0, 0)
    single = pl.Buffered(1)
    return pl.pallas_call(
        _out_kernel,
        grid=(rows // tm,),
        in_specs=[
            pl.BlockSpec((tm, D_MODEL), row),
            pl.BlockSpec((tm, DA_WIDTH), row),
            pl.BlockSpec((tm, MLA_WIDTH), row),
            pl.BlockSpec((D_MODEL, D_MODEL), const, pipeline_mode=single),
            pl.BlockSpec((1, D_MODEL), const),
            pl.BlockSpec((1, D_MODEL), const),
            pl.BlockSpec((D_MODEL, D_FF), const, pipeline_mode=single),
            pl.BlockSpec((D_FF, D_MODEL), const, pipeline_mode=single),
            pl.BlockSpec((1, D_MODEL), const),
        ],
        out_specs=pl.BlockSpec((tm, D_MODEL), row),
        out_shape=jax.ShapeDtypeStruct((rows, D_MODEL), F32),
        compiler_params=pltpu.CompilerParams(
            dimension_semantics=("arbitrary",), vmem_limit_bytes=VMEM_LIMIT),
        name="out_mlp",
    )(x2d, yda, ymla, w_o, g_post, g_pre, w1, w2, g_mlp)


def _position_tables(n_pos):
    inv_freq = 1.0 / (ROPE_THETA ** (jnp.arange(0, MLA_ROPE, 2, dtype=F32) / MLA_ROPE))
    pos = jnp.arange(n_pos, dtype=F32)
    ang = pos[:, None] * inv_freq[None, :]
    cos, sin = jnp.cos(ang), jnp.sin(ang)
    zeros = jnp.zeros((n_pos, LANES - MLA_ROPE), F32)
    ct = jnp.concatenate([cos, cos, zeros], axis=1)
    st = jnp.concatenate([-sin, sin, zeros], axis=1)
    ipos = jnp.arange(n_pos, dtype=jnp.int32)
    hi = ((ipos // CHUNK) * CHUNK).astype(F32)[:, None]
    lo = (ipos % CHUNK).astype(F32)[:, None]
    posf = jnp.concatenate([hi, lo] * ALIBI_PIECES
                           + [jnp.zeros((n_pos, LANES - 2 * ALIBI_PIECES), F32)], axis=1)
    return ct, st, posf.astype(BF16)


def _swap_halves(w):
    half = w.shape[-1] // 2
    return jnp.concatenate([w[..., half:], w[..., :half]], axis=-1)


def kernel(x, meta_tokens, g_attn_pre, w_in, da_lambda_q1, da_lambda_k1, da_lambda_q2,
           da_lambda_k2, g_da_sub, g_mla_q, w_mla_q_up, g_mla_kv, w_mla_kv_up, w_o,
           g_attn_post, g_mlp_pre, w_ff1, w_ff2, g_mlp_post):
    nb, seq, d = x.shape
    assert d == D_MODEL and w_in.shape[0] == 1, "single-layer block only"
    assert seq % ATT_TQ == 0 and seq % PROJ_TM == 0 and ATT_TQ == ATT_TK

    w_in0 = w_in[0]
    n_da = 2 * DA_WIDTH
    v_cols = slice(n_da, n_da + DA_WIDTH)
    rest = w_in0[:, n_da + DA_WIDTH:]
    kr = rest[:, MLA_Q_RANK + MLA_KV_RANK:]
    zpad = jnp.zeros((D_MODEL, LANES - MLA_ROPE), F32)
    w_in_ext = jnp.concatenate(
        [w_in0[:, :n_da], rest[:, :MLA_Q_RANK + MLA_KV_RANK], kr, zpad, _swap_halves(kr), zpad],
        axis=1).astype(BF16)
    w_vt = w_in0[:, v_cols].T.astype(BF16)
    wq = w_mla_q_up[0].reshape(MLA_Q_RANK, MLA_HEADS, MLA_NOPE + MLA_ROPE)
    wq_rope = wq[..., MLA_NOPE:]
    zq = jnp.zeros((MLA_Q_RANK, MLA_HEADS, LANES - MLA_ROPE), F32)
    wq_ext = jnp.concatenate(
        [wq[..., :MLA_NOPE], wq_rope, zq, _swap_halves(wq_rope), zq], axis=-1
    ).reshape(MLA_Q_RANK, MLA_HEADS * Q_UP_W).astype(BF16)
    wkv = w_mla_kv_up[0].reshape(MLA_KV_RANK, MLA_HEADS, MLA_NOPE + MLA_V)
    wkv_k = wkv[..., :MLA_NOPE].reshape(MLA_KV_RANK, MLA_HEADS * MLA_NOPE).astype(BF16)
    wkv_vt = wkv[..., MLA_NOPE:].reshape(MLA_KV_RANK, MLA_WIDTH).T.astype(BF16)

    ct, st, posf = _position_tables(N_META + seq)
    lam_params = jnp.zeros((8, LANES), F32)
    for r, vec in enumerate((da_lambda_q1, da_lambda_k1, da_lambda_q2, da_lambda_k2)):
        lam_params = lam_params.at[r, :DA_HEAD_DIM].set(vec[0].astype(F32))

    proj = functools.partial(
        _proj_call, g_pre=g_attn_pre, w_in=w_in_ext, w_vt=w_vt, g_q=g_mla_q, w_q=wq_ext,
        g_kv=g_mla_kv, w_kv=wkv_k, w_kvt=wkv_vt)
    x2d = x.reshape(nb * seq, D_MODEL)
    qda, kda, vtda, qm, km, vtm = proj(
        x2d, PROJ_TM, seq // PROJ_TM, ct=ct[N_META:], st=st[N_META:], posf=posf[N_META:],
        name="proj_tokens")
    _, kda_meta, vtda_meta, _, km_meta, vtm_meta = proj(
        meta_tokens.astype(F32), N_META, 1, ct=ct[:N_META], st=st[:N_META], posf=posf[:N_META],
        name="proj_meta")

    def pad_rows(a):
        return jnp.pad(a, ((0, META_PAD - N_META), (0, 0)))

    def pad_cols(a):
        return jnp.pad(a, ((0, 0), (0, META_PAD - N_META)))

    def per_batch(a):
        return a.reshape(nb, seq, a.shape[-1])

    yda = _da_call(per_batch(qda), per_batch(kda), vtda, pad_rows(kda_meta), pad_cols(vtda_meta),
                   lam_params, g_da_sub.reshape(DA_V_DIM, 1))
    ymla = _mla_call(per_batch(qm), per_batch(km), vtm, pad_rows(km_meta), pad_cols(vtm_meta))

    out = _out_call(x2d, yda.reshape(nb * seq, DA_WIDTH), ymla.reshape(nb * seq, MLA_WIDTH),
                    w_o[0].astype(BF16), g_attn_post, g_mlp_pre,
                    w_ff1[0].astype(BF16), w_ff2[0].astype(BF16), g_mlp_post)
    return out.reshape(nb, seq, D_MODEL)
```

```python
import functools
import math
from typing import Any, Callable, NamedTuple

import jax
import jax.numpy as jnp
import ml_dtypes
import numpy as np
from jax import lax
from jax.experimental import pallas as pl
from jax.experimental.pallas import tpu as pltpu

F32 = jnp.float32
BF16 = jnp.bfloat16

D_MODEL = 1024
N_META = 16
CHUNK = 64
EPS = 1e-6
NEG = -1e30
ROPE_THETA = 10000.0
LOG2E = math.log2(math.e)

DA_HEADS = 4
DA_HEAD_DIM = 64
DA_V_DIM = 128
DA_WIDTH = DA_HEADS * DA_V_DIM
MLA_HEADS = 4
MLA_NOPE = 128
MLA_ROPE = 64
MLA_V = 128
MLA_WIDTH = MLA_HEADS * MLA_V
MLA_Q_RANK = 256
MLA_KV_RANK = 128
D_FF = 4 * D_MODEL
LAMBDA_INIT = 0.8 - 0.6 * math.exp(-0.3 * 0)

LANES = 128
BF16_SUBLANES = 16
KEY_W = 2 * LANES
VT_ROWS = DA_V_DIM + BF16_SUBLANES
VMEM_LIMIT = 56 * 1024 * 1024

PROJ_TM = 512
ATT_TQ = 512
ATT_TK = 512
OUT_TM = 512
FF_CHUNK = 1024
META_PAD = 128
DA_HEAD_GROUP = 4
MLA_HEAD_GROUP = 4

C_QDA, C_KDA, C_CQ, C_CKV, C_KR, C_KRS, C_END = (0, 512, 1024, 1280, 1408, 1536, 1664)
Q_UP_W = 3 * LANES
ALIBI_PIECES = 3


def _rms_scale(x):
    return x * lax.rsqrt(jnp.mean(x * x, axis=-1, keepdims=True) + EPS)


def _dot(a, b):
    return jnp.dot(a, b, preferred_element_type=F32)


def _dot_nt(a, b):
    return lax.dot_general(a, b, (((1,), (1,)), ((), ())), preferred_element_type=F32)


def _alibi_pieces(h):
    c = np.float32(2.0 ** (-8.0 * (h + 1) / DA_HEADS)) * np.float32(LOG2E)
    pieces, rest = [], c
    for _ in range(ALIBI_PIECES):
        piece = np.float32(ml_dtypes.bfloat16(rest))
        pieces.append(float(piece))
        rest = np.float32(rest - piece)
    assert rest == 0.0
    return float(c), pieces


def _proj_kernel(x_ref, g_ref, win_ref, wvt_ref, gq_ref, wq_ref, gkv_ref, wkv_ref, wkvt_ref,
                 ct_ref, st_ref, posf_ref,
                 qda_ref, kda_ref, vtda_ref, qm_ref, km_ref, vtm_ref):
    x = x_ref[...]
    u = (_rms_scale(x) * g_ref[...]).astype(BF16)
    p = _dot(u, win_ref[...])
    ct = ct_ref[...]
    st = st_ref[...]
    ones = jnp.ones((BF16_SUBLANES, x.shape[0]), BF16)

    qda_ref[...] = (p[:, C_QDA:C_KDA] * (DA_HEAD_DIM ** -0.5 * LOG2E)).astype(BF16)
    posf = posf_ref[...]
    vt = _dot_nt(wvt_ref[...], u)
    for h in range(DA_HEADS):
        kda_ref[:, KEY_W * h:KEY_W * h + LANES] = (
            p[:, C_KDA + LANES * h:C_KDA + LANES * (h + 1)].astype(BF16))
        kda_ref[:, KEY_W * h + LANES:KEY_W * (h + 1)] = posf
        vtda_ref[VT_ROWS * h:VT_ROWS * h + DA_V_DIM, :] = (
            vt[DA_V_DIM * h:DA_V_DIM * (h + 1), :].astype(BF16))
        vtda_ref[VT_ROWS * h + DA_V_DIM:VT_ROWS * (h + 1), :] = ones

    k_rope = (p[:, C_KR:C_KRS] * ct + p[:, C_KRS:C_END] * st).astype(BF16)
    cq = (_rms_scale(p[:, C_CQ:C_CKV]) * gq_ref[...]).astype(BF16)
    qu = _dot(cq, wq_ref[...])
    ckv = (_rms_scale(p[:, C_CKV:C_KR]) * gkv_ref[...]).astype(BF16)
    kn = _dot(ckv, wkv_ref[...])
    vtm = _dot_nt(wkvt_ref[...], ckv)
    scale = (MLA_NOPE + MLA_ROPE) ** -0.5 * LOG2E
    for h in range(MLA_HEADS):
        b = Q_UP_W * h
        qm_ref[:, KEY_W * h:KEY_W * h + LANES] = (qu[:, b:b + LANES] * scale).astype(BF16)
        q_rope = qu[:, b + LANES:b + 2 * LANES] * ct + qu[:, b + 2 * LANES:b + 3 * LANES] * st
        qm_ref[:, KEY_W * h + LANES:KEY_W * (h + 1)] = (q_rope * scale).astype(BF16)
        km_ref[:, KEY_W * h:KEY_W * h + LANES] = kn[:, LANES * h:LANES * (h + 1)].astype(BF16)
        km_ref[:, KEY_W * h + LANES:KEY_W * (h + 1)] = k_rope
        vtm_ref[VT_ROWS * h:VT_ROWS * h + MLA_V, :] = vtm[MLA_V * h:MLA_V * (h + 1), :].astype(BF16)
        vtm_ref[VT_ROWS * h + MLA_V:VT_ROWS * (h + 1), :] = ones


def _proj_call(x2d, tm, n_tab_blocks, g_pre, w_in, w_vt, g_q, w_q, g_kv, w_kv, w_kvt,
               ct, st, posf, name):
    rows = x2d.shape[0]
    grid = (rows // tm,)
    row = lambda i: (i, 0)
    col = lambda i: (0, i)
    const = lambda i: (0, 0)
    tab = lambda i: (i % n_tab_blocks, 0)
    row_out = lambda w: (pl.BlockSpec((tm, w), row), jax.ShapeDtypeStruct((rows, w), BF16))
    col_out = lambda r: (pl.BlockSpec((r, tm), col), jax.ShapeDtypeStruct((r, rows), BF16))
    outs = [row_out(DA_WIDTH), row_out(DA_HEADS * KEY_W), col_out(DA_HEADS * VT_ROWS),
            row_out(MLA_HEADS * KEY_W), row_out(MLA_HEADS * KEY_W), col_out(MLA_HEADS * VT_ROWS)]
    return pl.pallas_call(
        _proj_kernel,
        grid=grid,
        in_specs=[
            pl.BlockSpec((tm, D_MODEL), row),
            pl.BlockSpec((1, D_MODEL), const),
            pl.BlockSpec((D_MODEL, C_END), const),
            pl.BlockSpec((DA_WIDTH, D_MODEL), const),
            pl.BlockSpec((1, MLA_Q_RANK), const),
            pl.BlockSpec((MLA_Q_RANK, MLA_HEADS * Q_UP_W), const),
            pl.BlockSpec((1, MLA_KV_RANK), const),
            pl.BlockSpec((MLA_KV_RANK, MLA_HEADS * MLA_NOPE), const),
            pl.BlockSpec((MLA_WIDTH, MLA_KV_RANK), const),
            pl.BlockSpec((tm, LANES), tab),
            pl.BlockSpec((tm, LANES), tab),
            pl.BlockSpec((tm, LANES), tab),
        ],
        out_specs=[o[0] for o in outs],
        out_shape=[o[1] for o in outs],
        compiler_params=pltpu.CompilerParams(
            dimension_semantics=("arbitrary",), vmem_limit_bytes=VMEM_LIMIT),
        name=name,
    )(x2d, g_pre, w_in, w_vt, g_q, w_q, g_kv, w_kv, w_kvt, ct, st, posf)


def _softmax_step(s_t, vt, m_ref, acc_ref, first):
    m_blk = jnp.max(s_t, axis=0, keepdims=True)
    if first:
        m_new = m_blk
    else:
        m_prev = m_ref[...]
        m_new = jnp.maximum(m_prev, m_blk)
    p_t = jnp.exp2((s_t - m_new).astype(BF16))
    pv = _dot(vt, p_t)
    if first:
        acc_ref[...] = pv
    else:
        acc_ref[...] = jnp.exp2(m_prev - m_new) * acc_ref[...] + pv
    m_ref[...] = m_new


def _meta_rows_valid(tq):
    return lax.broadcasted_iota(jnp.int32, (META_PAD, tq), 0) < N_META


def _diag_iotas(tk, tq):
    ik = lax.broadcasted_iota(jnp.int32, (tk, tq), 0)
    iq = lax.broadcasted_iota(jnp.int32, (tk, tq), 1)
    return ik, iq


def _normalised(acc_ref, dv):
    return acc_ref[0:dv, :] / acc_ref[dv:dv + 1, :]


class _Chain(NamedTuple):
    load_q: Callable[[], jax.Array]
    kcols: Any
    vrows: Any
    diag_fn: Callable[[jax.Array], jax.Array]


def _attend(qi, chains, kmeta_ref, vtmeta_ref, k_ref, vt_ref, s_bufs, m_ref, acc_ref):
    tk = ATT_TK
    buf_a, buf_b = s_bufs
    valid = _meta_rows_valid(ATT_TQ)
    s_meta = [jnp.where(valid, _dot_nt(kmeta_ref[:, ch.kcols], ch.load_q()), NEG)
              for ch in chains]

    def key_block(j):
        return pl.ds(pl.multiple_of(j * tk, tk), tk)

    def scores_into(buf, j):
        for c, ch in enumerate(chains):
            buf[c] = _dot_nt(k_ref[0, key_block(j), ch.kcols], ch.load_q())

    def consume(buf, j, diagonal):
        for c, ch in enumerate(chains):
            s_t = buf[c]
            if diagonal:
                s_t = ch.diag_fn(s_t)
            _softmax_step(s_t, vt_ref[ch.vrows, key_block(j)], m_ref.at[c], acc_ref.at[c], False)

    scores_into(buf_a, 0)
    for c, ch in enumerate(chains):
        _softmax_step(s_meta[c], vtmeta_ref[ch.vrows, :], m_ref.at[c], acc_ref.at[c], True)

    def pair(t, carry):
        j = 2 * t
        scores_into(buf_b, j + 1)
        consume(buf_a, j, False)
        scores_into(buf_a, j + 2)
        consume(buf_b, j + 1, False)
        return carry

    lax.fori_loop(0, qi // 2, pair, 0)

    @pl.when(qi % 2 == 0)
    def _():
        consume(buf_a, qi, True)

    @pl.when(qi % 2 == 1)
    def _():
        scores_into(buf_b, qi)
        consume(buf_a, qi - 1, False)
        consume(buf_b, qi, True)


def _attention_scratch(n_chains, tq):
    return [pltpu.VMEM((n_chains, 1, tq), F32), pltpu.VMEM((n_chains, VT_ROWS, tq), F32),
            pltpu.VMEM((n_chains, ATT_TK, tq), F32), pltpu.VMEM((n_chains, ATT_TK, tq), F32)]


def _da_kernel(q_ref, k_ref, vt_ref, kmeta_ref, vtmeta_ref, lam_ref, gsub_ref, y_ref,
               qx_ref, m_ref, acc_ref, sa_ref, sb_ref):
    qi = pl.program_id(1)
    tq, tk = ATT_TQ, ATT_TK
    lane = lax.broadcasted_iota(jnp.int32, (tq, LANES), 1)
    lp = lam_ref[...]
    lam = (jnp.exp(jnp.sum(lp[0:1] * lp[1:2], axis=-1, keepdims=True))
           - jnp.exp(jnp.sum(lp[2:3] * lp[3:4], axis=-1, keepdims=True)) + LAMBDA_INIT)

    for h0 in range(0, DA_HEADS, DA_HEAD_GROUP):
        chains = []
        for g in range(DA_HEAD_GROUP):
            h = h0 + g
            c, pieces = _alibi_pieces(h)
            q = q_ref[0, :, LANES * h:LANES * (h + 1)]
            pf = jnp.zeros((tq, LANES), F32)
            for i, piece in enumerate(pieces):
                pf = jnp.where((lane == 2 * i) | (lane == 2 * i + 1), piece, pf)
            pf = pf.astype(BF16)
            zero = jnp.zeros_like(q)
            qx_ref[2 * g, :, 0:LANES] = jnp.where(lane < DA_HEAD_DIM, q, zero)
            qx_ref[2 * g + 1, :, 0:LANES] = jnp.where(lane >= DA_HEAD_DIM, q, zero)
            qx_ref[2 * g, :, LANES:KEY_W] = pf
            qx_ref[2 * g + 1, :, LANES:KEY_W] = pf

            def diag_fn(s_t, c=c):
                ik, iq = _diag_iotas(tk, tq)
                visible = (ik // CHUNK) <= (iq // CHUNK)
                corr = (2.0 * c) * jnp.minimum(iq - ik, 0).astype(F32)
                return jnp.where(visible, s_t + corr, NEG)

            for comp in range(2):
                chains.append(_Chain(
                    load_q=functools.partial(lambda i: qx_ref[i], 2 * g + comp),
                    kcols=pl.ds(KEY_W * h, KEY_W), vrows=pl.ds(VT_ROWS * h, VT_ROWS),
                    diag_fn=diag_fn))

        _attend(qi, chains, kmeta_ref, vtmeta_ref, k_ref, vt_ref, (sa_ref, sb_ref), m_ref, acc_ref)

        for g in range(DA_HEAD_GROUP):
            h = h0 + g
            o = (_normalised(acc_ref.at[2 * g], DA_V_DIM)
                 - lam * _normalised(acc_ref.at[2 * g + 1], DA_V_DIM))
            o = o * lax.rsqrt(jnp.mean(o * o, axis=0, keepdims=True) + EPS)
            o = o * gsub_ref[...] * (1.0 - LAMBDA_INIT)
            y_ref[0, :, LANES * h:LANES * (h + 1)] = o.T.astype(BF16)


def _da_call(q, k, vt, kmeta, vtmeta, lam_params, g_sub_col):
    nb, s, _ = q.shape
    tq = ATT_TQ
    n_chains = 2 * DA_HEAD_GROUP
    return pl.pallas_call(
        _da_kernel,
        grid=(nb, s // tq),
        in_specs=[
            pl.BlockSpec((1, tq, DA_WIDTH), lambda b, i: (b, i, 0)),
            pl.BlockSpec((1, s, DA_HEADS * KEY_W), lambda b, i: (b, 0, 0)),
            pl.BlockSpec((DA_HEADS * VT_ROWS, s), lambda b, i: (0, b)),
            pl.BlockSpec((META_PAD, DA_HEADS * KEY_W), lambda b, i: (0, 0)),
            pl.BlockSpec((DA_HEADS * VT_ROWS, META_PAD), lambda b, i: (0, 0)),
            pl.BlockSpec((8, LANES), lambda b, i: (0, 0)),
            pl.BlockSpec((DA_V_DIM, 1), lambda b, i: (0, 0)),
        ],
        out_specs=pl.BlockSpec((1, tq, DA_WIDTH), lambda b, i: (b, i, 0)),
        out_shape=jax.ShapeDtypeStruct((nb, s, DA_WIDTH), BF16),
        scratch_shapes=[pltpu.VMEM((n_chains, tq, KEY_W), BF16)] + _attention_scratch(n_chains, tq),
        compiler_params=pltpu.CompilerParams(
            dimension_semantics=("arbitrary", "arbitrary"), vmem_limit_bytes=VMEM_LIMIT),
        name="diff_attention",
    )(q, k, vt, kmeta, vtmeta, lam_params, g_sub_col)


def _mla_kernel(q_ref, k_ref, vt_ref, kmeta_ref, vtmeta_ref, y_ref, m_ref, acc_ref,
                sa_ref, sb_ref):
    qi = pl.program_id(1)
    tq, tk = ATT_TQ, ATT_TK

    def diag_fn(s_t):
        ik, iq = _diag_iotas(tk, tq)
        return jnp.where((ik // CHUNK) <= (iq // CHUNK), s_t, NEG)

    for h0 in range(0, MLA_HEADS, MLA_HEAD_GROUP):
        heads = range(h0, h0 + MLA_HEAD_GROUP)
        chains = [
            _Chain(load_q=functools.partial(lambda h: q_ref[0, :, KEY_W * h:KEY_W * (h + 1)], h),
                   kcols=pl.ds(KEY_W * h, KEY_W), vrows=pl.ds(VT_ROWS * h, VT_ROWS),
                   diag_fn=diag_fn)
            for h in heads]
        _attend(qi, chains, kmeta_ref, vtmeta_ref, k_ref, vt_ref, (sa_ref, sb_ref), m_ref, acc_ref)
        for g, h in enumerate(heads):
            y_ref[0, :, LANES * h:LANES * (h + 1)] = (
                _normalised(acc_ref.at[g], MLA_V).T.astype(BF16))


def _mla_call(q, k, vt, kmeta, vtmeta):
    nb, s, _ = q.shape
    tq = ATT_TQ
    return pl.pallas_call(
        _mla_kernel,
        grid=(nb, s // tq),
        in_specs=[
            pl.BlockSpec((1, tq, MLA_HEADS * KEY_W), lambda b, i: (b, i, 0)),
            pl.BlockSpec((1, s, MLA_HEADS * KEY_W), lambda b, i: (b, 0, 0)),
            pl.BlockSpec((MLA_HEADS * VT_ROWS, s), lambda b, i: (0, b)),
            pl.BlockSpec((META_PAD, MLA_HEADS * KEY_W), lambda b, i: (0, 0)),
            pl.BlockSpec((MLA_HEADS * VT_ROWS, META_PAD), lambda b, i: (0, 0)),
        ],
        out_specs=pl.BlockSpec((1, tq, MLA_WIDTH), lambda b, i: (b, i, 0)),
        out_shape=jax.ShapeDtypeStruct((nb, s, MLA_WIDTH), BF16),
        scratch_shapes=_attention_scratch(MLA_HEAD_GROUP, tq),
        compiler_params=pltpu.CompilerParams(
            dimension_semantics=("arbitrary", "arbitrary"), vmem_limit_bytes=VMEM_LIMIT),
        name="mla_attention",
    )(q, k, vt, kmeta, vtmeta)


def _out_kernel(x_ref, yda_ref, ymla_ref, wo_ref, gpost_ref, gpre_ref, w1_ref, w2_ref,
                gmlp_ref, o_ref):
    mix = (_dot(yda_ref[...], wo_ref[0:DA_WIDTH, :])
           + _dot(ymla_ref[...], wo_ref[DA_WIDTH:DA_WIDTH + MLA_WIDTH, :]))
    h1 = x_ref[...] + _rms_scale(mix) * gpost_ref[...]
    u = (_rms_scale(h1) * gpre_ref[...]).astype(BF16)
    f = None
    for c in range(D_FF // FF_CHUNK):
        cols = slice(FF_CHUNK * c, FF_CHUNK * (c + 1))
        hid = jnp.square(jnp.maximum(_dot(u, w1_ref[:, cols]), 0.0)).astype(BF16)
        part = _dot(hid, w2_ref[cols, :])
        f = part if f is None else f + part
    o_ref[...] = h1 + _rms_scale(f) * gmlp_ref[...]


def _out_call(x2d, yda, ymla, w_o, g_post, g_pre, w1, w2, g_mlp):
    rows = x2d.shape[0]
    tm = OUT_TM
    row = lambda i: (i, 0)
    const = lambda i: (0, 0)
    single = pl.Buffered(1)
    return pl.pallas_call(
        _out_kernel,
        grid=(rows // tm,),
        in_specs=[
            pl.BlockSpec((tm, D_MODEL), row),
            pl.BlockSpec((tm, DA_WIDTH), row),
            pl.BlockSpec((tm, MLA_WIDTH), row),
            pl.BlockSpec((D_MODEL, D_MODEL), const, pipeline_mode=single),
            pl.BlockSpec((1, D_MODEL), const),
            pl.BlockSpec((1, D_MODEL), const),
            pl.BlockSpec((D_MODEL, D_FF), const, pipeline_mode=single),
            pl.BlockSpec((D_FF, D_MODEL), const, pipeline_mode=single),
            pl.BlockSpec((1, D_MODEL), const),
        ],
        out_specs=pl.BlockSpec((tm, D_MODEL), row),
        out_shape=jax.ShapeDtypeStruct((rows, D_MODEL), F32),
        compiler_params=pltpu.CompilerParams(
            dimension_semantics=("arbitrary",), vmem_limit_bytes=VMEM_LIMIT),
        name="out_mlp",
    )(x2d, yda, ymla, w_o, g_post, g_pre, w1, w2, g_mlp)


def _position_tables(n_pos):
    inv_freq = 1.0 / (ROPE_THETA ** (jnp.arange(0, MLA_ROPE, 2, dtype=F32) / MLA_ROPE))
    pos = jnp.arange(n_pos, dtype=F32)
    ang = pos[:, None] * inv_freq[None, :]
    cos, sin = jnp.cos(ang), jnp.sin(ang)
    zeros = jnp.zeros((n_pos, LANES - MLA_ROPE), F32)
    ct = jnp.concatenate([cos, cos, zeros], axis=1)
    st = jnp.concatenate([-sin, sin, zeros], axis=1)
    ipos = jnp.arange(n_pos, dtype=jnp.int32)
    hi = ((ipos // CHUNK) * CHUNK).astype(F32)[:, None]
    lo = (ipos % CHUNK).astype(F32)[:, None]
    posf = jnp.concatenate([hi, lo] * ALIBI_PIECES
                           + [jnp.zeros((n_pos, LANES - 2 * ALIBI_PIECES), F32)], axis=1)
    return ct, st, posf.astype(BF16)


def _swap_halves(w):
    half = w.shape[-1] // 2
    return jnp.concatenate([w[..., half:], w[..., :half]], axis=-1)


def kernel(x, meta_tokens, g_attn_pre, w_in, da_lambda_q1, da_lambda_k1, da_lambda_q2,
           da_lambda_k2, g_da_sub, g_mla_q, w_mla_q_up, g_mla_kv, w_mla_kv_up, w_o,
           g_attn_post, g_mlp_pre, w_ff1, w_ff2, g_mlp_post):
    nb, seq, d = x.shape
    assert d == D_MODEL and w_in.shape[0] == 1, "single-layer block only"
    assert seq % ATT_TQ == 0 and seq % PROJ_TM == 0 and ATT_TQ == ATT_TK

    w_in0 = w_in[0]
    n_da = 2 * DA_WIDTH
    v_cols = slice(n_da, n_da + DA_WIDTH)
    rest = w_in0[:, n_da + DA_WIDTH:]
    kr = rest[:, MLA_Q_RANK + MLA_KV_RANK:]
    zpad = jnp.zeros((D_MODEL, LANES - MLA_ROPE), F32)
    w_in_ext = jnp.concatenate(
        [w_in0[:, :n_da], rest[:, :MLA_Q_RANK + MLA_KV_RANK], kr, zpad, _swap_halves(kr), zpad],
        axis=1).astype(BF16)
    w_vt = w_in0[:, v_cols].T.astype(BF16)
    wq = w_mla_q_up[0].reshape(MLA_Q_RANK, MLA_HEADS, MLA_NOPE + MLA_ROPE)
    wq_rope = wq[..., MLA_NOPE:]
    zq = jnp.zeros((MLA_Q_RANK, MLA_HEADS, LANES - MLA_ROPE), F32)
    wq_ext = jnp.concatenate(
        [wq[..., :MLA_NOPE], wq_rope, zq, _swap_halves(wq_rope), zq], axis=-1
    ).reshape(MLA_Q_RANK, MLA_HEADS * Q_UP_W).astype(BF16)
    wkv = w_mla_kv_up[0].reshape(MLA_KV_RANK, MLA_HEADS, MLA_NOPE + MLA_V)
    wkv_k = wkv[..., :MLA_NOPE].reshape(MLA_KV_RANK, MLA_HEADS * MLA_NOPE).astype(BF16)
    wkv_vt = wkv[..., MLA_NOPE:].reshape(MLA_KV_RANK, MLA_WIDTH).T.astype(BF16)

    ct, st, posf = _position_tables(N_META + seq)
    lam_params = jnp.zeros((8, LANES), F32)
    for r, vec in enumerate((da_lambda_q1, da_lambda_k1, da_lambda_q2, da_lambda_k2)):
        lam_params = lam_params.at[r, :DA_HEAD_DIM].set(vec[0].astype(F32))

    proj = functools.partial(
        _proj_call, g_pre=g_attn_pre, w_in=w_in_ext, w_vt=w_vt, g_q=g_mla_q, w_q=wq_ext,
        g_kv=g_mla_kv, w_kv=wkv_k, w_kvt=wkv_vt)
    x2d = x.reshape(nb * seq, D_MODEL)
    qda, kda, vtda, qm, km, vtm = proj(
        x2d, PROJ_TM, seq // PROJ_TM, ct=ct[N_META:], st=st[N_META:], posf=posf[N_META:],
        name="proj_tokens")
    _, kda_meta, vtda_meta, _, km_meta, vtm_meta = proj(
        meta_tokens.astype(F32), N_META, 1, ct=ct[:N_META], st=st[:N_META], posf=posf[:N_META],
        name="proj_meta")

    def pad_rows(a):
        return jnp.pad(a, ((0, META_PAD - N_META), (0, 0)))

    def pad_cols(a):
        return jnp.pad(a, ((0, 0), (0, META_PAD - N_META)))

    def per_batch(a):
        return a.reshape(nb, seq, a.shape[-1])

    yda = _da_call(per_batch(qda), per_batch(kda), vtda, pad_rows(kda_meta), pad_cols(vtda_meta),
                   lam_params, g_da_sub.reshape(DA_V_DIM, 1))
    ymla = _mla_call(per_batch(qm), per_batch(km), vtm, pad_rows(km_meta), pad_cols(vtm_meta))

    out = _out_call(x2d, yda.reshape(nb * seq, DA_WIDTH), ymla.reshape(nb * seq, MLA_WIDTH),
                    w_o[0].astype(BF16), g_attn_post, g_mlp_pre,
                    w_ff1[0].astype(BF16), w_ff2[0].astype(BF16), g_mlp_post)
    return out.reshape(nb, seq, D_MODEL)
```

```python
import functools
import math
from typing import Any, Callable, NamedTuple

import jax
import jax.numpy as jnp
import ml_dtypes
import numpy as np
from jax import lax
from jax.experimental import pallas as pl
from jax.experimental.pallas import tpu as pltpu

F32 = jnp.float32
BF16 = jnp.bfloat16

D_MODEL = 1024
N_META = 16
CHUNK = 64
EPS = 1e-6
NEG = -1e30
ROPE_THETA = 10000.0
LOG2E = math.log2(math.e)

DA_HEADS = 4
DA_HEAD_DIM = 64
DA_V_DIM = 128
DA_WIDTH = DA_HEADS * DA_V_DIM
MLA_HEADS = 4
MLA_NOPE = 128
MLA_ROPE = 64
MLA_V = 128
MLA_WIDTH = MLA_HEADS * MLA_V
MLA_Q_RANK = 256
MLA_KV_RANK = 128
D_FF = 4 * D_MODEL
LAMBDA_INIT = 0.8 - 0.6 * math.exp(-0.3 * 0)

LANES = 128
BF16_SUBLANES = 16
KEY_W = 2 * LANES
VT_ROWS = DA_V_DIM + BF16_SUBLANES
VMEM_LIMIT = 56 * 1024 * 1024

PROJ_TM = 1024
PROJ_ROW_GROUPS = 4
ATT_TQ = 512
ATT_TK = 512
OUT_TM = 1024
FF_CHUNK = 1024
OUT_ROW_GROUPS = 4
META_PAD = 128
DA_HEAD_GROUP = 4
MLA_HEAD_GROUP = 4

C_QDA, C_KDA, C_CQ, C_CKV, C_KR, C_KRS, C_END = (0, 512, 1024, 1280, 1408, 1536, 1664)
Q_UP_W = 3 * LANES
ALIBI_PIECES = 3


def _rms_scale(x):
    return x * lax.rsqrt(jnp.mean(x * x, axis=-1, keepdims=True) + EPS)


def _dot(a, b):
    return jnp.dot(a, b, preferred_element_type=F32)


def _dot_nt(a, b):
    return lax.dot_general(a, b, (((1,), (1,)), ((), ())), preferred_element_type=F32)


def _alibi_pieces(h):
    c = np.float32(2.0 ** (-8.0 * (h + 1) / DA_HEADS)) * np.float32(LOG2E)
    pieces, rest = [], c
    for _ in range(ALIBI_PIECES):
        piece = np.float32(ml_dtypes.bfloat16(rest))
        pieces.append(float(piece))
        rest = np.float32(rest - piece)
    assert rest == 0.0
    return float(c), pieces


def _proj_kernel(x_ref, g_ref, win_ref, wvt_ref, gq_ref, wq_ref, gkv_ref, wkv_ref, wkvt_ref,
                 ct_ref, st_ref, posf_ref,
                 qda_ref, kda_ref, vtda_ref, qm_ref, km_ref, vtm_ref):
    tm = x_ref.shape[0]
    n_groups = PROJ_ROW_GROUPS if tm % (PROJ_ROW_GROUPS * LANES) == 0 else 1
    groups = [slice(r, r + tm // n_groups) for r in range(0, tm, tm // n_groups)]
    ones = jnp.ones((BF16_SUBLANES, tm // n_groups), BF16)
    scale = (MLA_NOPE + MLA_ROPE) ** -0.5 * LOG2E

    u = [(_rms_scale(x_ref[r, :]) * g_ref[...]).astype(BF16) for r in groups]
    p = [_dot(ug, win_ref[...]) for ug in u]
    vt = [_dot_nt(wvt_ref[...], ug) for ug in u]

    for r, pg, vtg in zip(groups, p, vt):
        qda_ref[r, :] = (pg[:, C_QDA:C_KDA] * (DA_HEAD_DIM ** -0.5 * LOG2E)).astype(BF16)
        posf = posf_ref[r, :]
        for h in range(DA_HEADS):
            kda_ref[r, KEY_W * h:KEY_W * h + LANES] = (
                pg[:, C_KDA + LANES * h:C_KDA + LANES * (h + 1)].astype(BF16))
            kda_ref[r, KEY_W * h + LANES:KEY_W * (h + 1)] = posf
            vtda_ref[VT_ROWS * h:VT_ROWS * h + DA_V_DIM, r] = (
                vtg[DA_V_DIM * h:DA_V_DIM * (h + 1), :].astype(BF16))
            vtda_ref[VT_ROWS * h + DA_V_DIM:VT_ROWS * (h + 1), r] = ones

    cq = [(_rms_scale(pg[:, C_CQ:C_CKV]) * gq_ref[...]).astype(BF16) for pg in p]
    ckv = [(_rms_scale(pg[:, C_CKV:C_KR]) * gkv_ref[...]).astype(BF16) for pg in p]
    qu = [_dot(c, wq_ref[...]) for c in cq]
    kn = [_dot(c, wkv_ref[...]) for c in ckv]
    vtm = [_dot_nt(wkvt_ref[...], c) for c in ckv]
    for r, pg, qug, kng, vtmg in zip(groups, p, qu, kn, vtm):
        ct = ct_ref[r, :]
        st = st_ref[r, :]
        k_rope = (pg[:, C_KR:C_KRS] * ct + pg[:, C_KRS:C_END] * st).astype(BF16)
        for h in range(MLA_HEADS):
            b = Q_UP_W * h
            qm_ref[r, KEY_W * h:KEY_W * h + LANES] = (qug[:, b:b + LANES] * scale).astype(BF16)
            q_rope = (qug[:, b + LANES:b + 2 * LANES] * ct
                      + qug[:, b + 2 * LANES:b + 3 * LANES] * st)
            qm_ref[r, KEY_W * h + LANES:KEY_W * (h + 1)] = (q_rope * scale).astype(BF16)
            km_ref[r, KEY_W * h:KEY_W * h + LANES] = kng[:, LANES * h:LANES * (h + 1)].astype(BF16)
            km_ref[r, KEY_W * h + LANES:KEY_W * (h + 1)] = k_rope
            vtm_ref[VT_ROWS * h:VT_ROWS * h + MLA_V, r] = (
                vtmg[MLA_V * h:MLA_V * (h + 1), :].astype(BF16))
            vtm_ref[VT_ROWS * h + MLA_V:VT_ROWS * (h + 1), r] = ones


def _proj_call(x2d, tm, n_tab_blocks, g_pre, w_in, w_vt, g_q, w_q, g_kv, w_kv, w_kvt,
               ct, st, posf, name):
    rows = x2d.shape[0]
    grid = (rows // tm,)
    row = lambda i: (i, 0)
    col = lambda i: (0, i)
    const = lambda i: (0, 0)
    tab = lambda i: (i % n_tab_blocks, 0)
    row_out = lambda w: (pl.BlockSpec((tm, w), row), jax.ShapeDtypeStruct((rows, w), BF16))
    col_out = lambda r: (pl.BlockSpec((r, tm), col), jax.ShapeDtypeStruct((r, rows), BF16))
    outs = [row_out(DA_WIDTH), row_out(DA_HEADS * KEY_W), col_out(DA_HEADS * VT_ROWS),
            row_out(MLA_HEADS * KEY_W), row_out(MLA_HEADS * KEY_W), col_out(MLA_HEADS * VT_ROWS)]
    return pl.pallas_call(
        _proj_kernel,
        grid=grid,
        in_specs=[
            pl.BlockSpec((tm, D_MODEL), row),
            pl.BlockSpec((1, D_MODEL), const),
            pl.BlockSpec((D_MODEL, C_END), const),
            pl.BlockSpec((DA_WIDTH, D_MODEL), const),
            pl.BlockSpec((1, MLA_Q_RANK), const),
            pl.BlockSpec((MLA_Q_RANK, MLA_HEADS * Q_UP_W), const),
            pl.BlockSpec((1, MLA_KV_RANK), const),
            pl.BlockSpec((MLA_KV_RANK, MLA_HEADS * MLA_NOPE), const),
            pl.BlockSpec((MLA_WIDTH, MLA_KV_RANK), const),
            pl.BlockSpec((tm, LANES), tab),
            pl.BlockSpec((tm, LANES), tab),
            pl.BlockSpec((tm, LANES), tab),
        ],
        out_specs=[o[0] for o in outs],
        out_shape=[o[1] for o in outs],
        compiler_params=pltpu.CompilerParams(
            dimension_semantics=("arbitrary",), vmem_limit_bytes=VMEM_LIMIT),
        name=name,
    )(x2d, g_pre, w_in, w_vt, g_q, w_q, g_kv, w_kv, w_kvt, ct, st, posf)


def _softmax_step(s_t, vt, m_ref, acc_ref, first):
    m_blk = jnp.max(s_t, axis=0, keepdims=True)
    if first:
        m_new = m_blk
    else:
        m_prev = m_ref[...]
        m_new = jnp.maximum(m_prev, m_blk)
    p_t = jnp.exp2((s_t - m_new).astype(BF16))
    pv = _dot(vt, p_t)
    if first:
        acc_ref[...] = pv
    else:
        acc_ref[...] = jnp.exp2(m_prev - m_new) * acc_ref[...] + pv
    m_ref[...] = m_new


def _meta_rows_valid(tq):
    return lax.broadcasted_iota(jnp.int32, (META_PAD, tq), 0) < N_META


def _diag_iotas(tk, tq):
    ik = lax.broadcasted_iota(jnp.int32, (tk, tq), 0)
    iq = lax.broadcasted_iota(jnp.int32, (tk, tq), 1)
    return ik, iq


def _normalised(acc_ref, dv):
    return acc_ref[0:dv, :] / acc_ref[dv:dv + 1, :]


class _Chain(NamedTuple):
    load_q: Callable[[], jax.Array]
    kcols: Any
    vrows: Any
    diag_fn: Callable[[jax.Array], jax.Array]


def _attend(qi, chains, kmeta_ref, vtmeta_ref, k_ref, vt_ref, s_bufs, m_ref, acc_ref):
    tk = ATT_TK
    buf_a, buf_b = s_bufs
    valid = _meta_rows_valid(ATT_TQ)
    s_meta = [jnp.where(valid, _dot_nt(kmeta_ref[:, ch.kcols], ch.load_q()), NEG)
              for ch in chains]

    def key_block(j):
        return pl.ds(pl.multiple_of(j * tk, tk), tk)

    def scores_into(buf, j, c):
        ch = chains[c]
        buf[c] = _dot_nt(k_ref[0, key_block(j), ch.kcols], ch.load_q())

    def consume(buf, j, c, diagonal=False):
        ch = chains[c]
        s_t = buf[c]
        if diagonal:
            s_t = ch.diag_fn(s_t)
        _softmax_step(s_t, vt_ref[ch.vrows, key_block(j)], m_ref.at[c], acc_ref.at[c], False)

    n = len(chains)
    for c in range(n):
        scores_into(buf_a, 0, c)
        _softmax_step(s_meta[c], vtmeta_ref[chains[c].vrows, :], m_ref.at[c], acc_ref.at[c], True)

    def pair(t, carry):
        j = 2 * t
        for c in range(n):
            scores_into(buf_b, j + 1, c)
            consume(buf_a, j, c)
        for c in range(n):
            scores_into(buf_a, j + 2, c)
            consume(buf_b, j + 1, c)
        return carry

    lax.fori_loop(0, qi // 2, pair, 0)

    @pl.when(qi % 2 == 0)
    def _():
        for c in range(n):
            consume(buf_a, qi, c, diagonal=True)

    @pl.when(qi % 2 == 1)
    def _():
        for c in range(n):
            scores_into(buf_b, qi, c)
            consume(buf_a, qi - 1, c)
        for c in range(n):
            consume(buf_b, qi, c, diagonal=True)


def _attention_scratch(n_chains, tq):
    return [pltpu.VMEM((n_chains, 1, tq), F32), pltpu.VMEM((n_chains, VT_ROWS, tq), F32),
            pltpu.VMEM((n_chains, ATT_TK, tq), F32), pltpu.VMEM((n_chains, ATT_TK, tq), F32)]


def _da_kernel(q_ref, k_ref, vt_ref, kmeta_ref, vtmeta_ref, lam_ref, gsub_ref, y_ref,
               qx_ref, m_ref, acc_ref, sa_ref, sb_ref):
    qi = pl.program_id(1)
    tq, tk = ATT_TQ, ATT_TK
    lane = lax.broadcasted_iota(jnp.int32, (tq, LANES), 1)
    lp = lam_ref[...]
    lam = (jnp.exp(jnp.sum(lp[0:1] * lp[1:2], axis=-1, keepdims=True))
           - jnp.exp(jnp.sum(lp[2:3] * lp[3:4], axis=-1, keepdims=True)) + LAMBDA_INIT)

    for h0 in range(0, DA_HEADS, DA_HEAD_GROUP):
        chains = []
        for g in range(DA_HEAD_GROUP):
            h = h0 + g
            c, pieces = _alibi_pieces(h)
            q = q_ref[0, :, LANES * h:LANES * (h + 1)]
            pf = jnp.zeros((tq, LANES), F32)
            for i, piece in enumerate(pieces):
                pf = jnp.where((lane == 2 * i) | (lane == 2 * i + 1), piece, pf)
            pf = pf.astype(BF16)
            zero = jnp.zeros_like(q)
            qx_ref[2 * g, :, 0:LANES] = jnp.where(lane < DA_HEAD_DIM, q, zero)
            qx_ref[2 * g + 1, :, 0:LANES] = jnp.where(lane >= DA_HEAD_DIM, q, zero)
            qx_ref[2 * g, :, LANES:KEY_W] = pf
            qx_ref[2 * g + 1, :, LANES:KEY_W] = pf

            def diag_fn(s_t, c=c):
                ik, iq = _diag_iotas(tk, tq)
                visible = (ik // CHUNK) <= (iq // CHUNK)
                corr = (2.0 * c) * jnp.minimum(iq - ik, 0).astype(F32)
                return jnp.where(visible, s_t + corr, NEG)

            for comp in range(2):
                chains.append(_Chain(
                    load_q=functools.partial(lambda i: qx_ref[i], 2 * g + comp),
                    kcols=pl.ds(KEY_W * h, KEY_W), vrows=pl.ds(VT_ROWS * h, VT_ROWS),
                    diag_fn=diag_fn))

        _attend(qi, chains, kmeta_ref, vtmeta_ref, k_ref, vt_ref, (sa_ref, sb_ref), m_ref, acc_ref)

        for g in range(DA_HEAD_GROUP):
            h = h0 + g
            o = (_normalised(acc_ref.at[2 * g], DA_V_DIM)
                 - lam * _normalised(acc_ref.at[2 * g + 1], DA_V_DIM))
            o = o * lax.rsqrt(jnp.mean(o * o, axis=0, keepdims=True) + EPS)
            o = o * gsub_ref[...] * (1.0 - LAMBDA_INIT)
            y_ref[0, :, LANES * h:LANES * (h + 1)] = o.T.astype(BF16)


def _da_call(q, k, vt, kmeta, vtmeta, lam_params, g_sub_col):
    nb, s, _ = q.shape
    tq = ATT_TQ
    n_chains = 2 * DA_HEAD_GROUP
    return pl.pallas_call(
        _da_kernel,
        grid=(nb, s // tq),
        in_specs=[
            pl.BlockSpec((1, tq, DA_WIDTH), lambda b, i: (b, i, 0)),
            pl.BlockSpec((1, s, DA_HEADS * KEY_W), lambda b, i: (b, 0, 0)),
            pl.BlockSpec((DA_HEADS * VT_ROWS, s), lambda b, i: (0, b)),
            pl.BlockSpec((META_PAD, DA_HEADS * KEY_W), lambda b, i: (0, 0)),
            pl.BlockSpec((DA_HEADS * VT_ROWS, META_PAD), lambda b, i: (0, 0)),
            pl.BlockSpec((8, LANES), lambda b, i: (0, 0)),
            pl.BlockSpec((DA_V_DIM, 1), lambda b, i: (0, 0)),
        ],
        out_specs=pl.BlockSpec((1, tq, DA_WIDTH), lambda b, i: (b, i, 0)),
        out_shape=jax.ShapeDtypeStruct((nb, s, DA_WIDTH), BF16),
        scratch_shapes=[pltpu.VMEM((n_chains, tq, KEY_W), BF16)] + _attention_scratch(n_chains, tq),
        compiler_params=pltpu.CompilerParams(
            dimension_semantics=("arbitrary", "arbitrary"), vmem_limit_bytes=VMEM_LIMIT),
        name="diff_attention",
    )(q, k, vt, kmeta, vtmeta, lam_params, g_sub_col)


def _mla_kernel(q_ref, k_ref, vt_ref, kmeta_ref, vtmeta_ref, y_ref, m_ref, acc_ref,
                sa_ref, sb_ref):
    qi = pl.program_id(1)
    tq, tk = ATT_TQ, ATT_TK

    def diag_fn(s_t):
        ik, iq = _diag_iotas(tk, tq)
        return jnp.where((ik // CHUNK) <= (iq // CHUNK), s_t, NEG)

    for h0 in range(0, MLA_HEADS, MLA_HEAD_GROUP):
        heads = range(h0, h0 + MLA_HEAD_GROUP)
        chains = [
            _Chain(load_q=functools.partial(lambda h: q_ref[0, :, KEY_W * h:KEY_W * (h + 1)], h),
                   kcols=pl.ds(KEY_W * h, KEY_W), vrows=pl.ds(VT_ROWS * h, VT_ROWS),
                   diag_fn=diag_fn)
            for h in heads]
        _attend(qi, chains, kmeta_ref, vtmeta_ref, k_ref, vt_ref, (sa_ref, sb_ref), m_ref, acc_ref)
        for g, h in enumerate(heads):
            y_ref[0, :, LANES * h:LANES * (h + 1)] = (
                _normalised(acc_ref.at[g], MLA_V).T.astype(BF16))


def _mla_call(q, k, vt, kmeta, vtmeta):
    nb, s, _ = q.shape
    tq = ATT_TQ
    return pl.pallas_call(
        _mla_kernel,
        grid=(nb, s // tq),
        in_specs=[
            pl.BlockSpec((1, tq, MLA_HEADS * KEY_W), lambda b, i: (b, i, 0)),
            pl.BlockSpec((1, s, MLA_HEADS * KEY_W), lambda b, i: (b, 0, 0)),
            pl.BlockSpec((MLA_HEADS * VT_ROWS, s), lambda b, i: (0, b)),
            pl.BlockSpec((META_PAD, MLA_HEADS * KEY_W), lambda b, i: (0, 0)),
            pl.BlockSpec((MLA_HEADS * VT_ROWS, META_PAD), lambda b, i: (0, 0)),
        ],
        out_specs=pl.BlockSpec((1, tq, MLA_WIDTH), lambda b, i: (b, i, 0)),
        out_shape=jax.ShapeDtypeStruct((nb, s, MLA_WIDTH), BF16),
        scratch_shapes=_attention_scratch(MLA_HEAD_GROUP, tq),
        compiler_params=pltpu.CompilerParams(
            dimension_semantics=("arbitrary", "arbitrary"), vmem_limit_bytes=VMEM_LIMIT),
        name="mla_attention",
    )(q, k, vt, kmeta, vtmeta)


def _out_kernel(x_ref, yda_ref, ymla_ref, wo_ref, gpost_ref, gpre_ref, w1_ref, w2_ref,
                gmlp_ref, o_ref):
    tm = x_ref.shape[0]
    groups = [slice(r, r + tm // OUT_ROW_GROUPS) for r in range(0, tm, tm // OUT_ROW_GROUPS)]
    mix = [_dot(yda_ref[r, :], wo_ref[0:DA_WIDTH, :])
           + _dot(ymla_ref[r, :], wo_ref[DA_WIDTH:DA_WIDTH + MLA_WIDTH, :]) for r in groups]
    h1 = [x_ref[r, :] + _rms_scale(m) * gpost_ref[...] for r, m in zip(groups, mix)]
    u = [(_rms_scale(h) * gpre_ref[...]).astype(BF16) for h in h1]
    for r, h, ug in zip(groups, h1, u):
        f = None
        for c in range(D_FF // FF_CHUNK):
            cols = slice(FF_CHUNK * c, FF_CHUNK * (c + 1))
            hid = jnp.square(jnp.maximum(_dot(ug, w1_ref[:, cols]), 0.0)).astype(BF16)
            part = _dot(hid, w2_ref[cols, :])
            f = part if f is None else f + part
        o_ref[r, :] = h + _rms_scale(f) * gmlp_ref[...]


def _out_call(x2d, yda, ymla, w_o, g_post, g_pre, w1, w2, g_mlp):
    rows = x2d.shape[0]
    tm = OUT_TM
    row = lambda i: (i, 0)
    const = lambda i: (0, 0)
    single = pl.Buffered(1)
    return pl.pallas_call(
        _out_kernel,
        grid=(rows // tm,),
        in_specs=[
            pl.BlockSpec((tm, D_MODEL), row),
            pl.BlockSpec((tm, DA_WIDTH), row),
            pl.BlockSpec((tm, MLA_WIDTH), row),
            pl.BlockSpec((D_MODEL, D_MODEL), const, pipeline_mode=single),
            pl.BlockSpec((1, D_MODEL), const),
            pl.BlockSpec((1, D_MODEL), const),
            pl.BlockSpec((D_MODEL, D_FF), const, pipeline_mode=single),
            pl.BlockSpec((D_FF, D_MODEL), const, pipeline_mode=single),
            pl.BlockSpec((1, D_MODEL), const),
        ],
        out_specs=pl.BlockSpec((tm, D_MODEL), row),
        out_shape=jax.ShapeDtypeStruct((rows, D_MODEL), F32),
        compiler_params=pltpu.CompilerParams(
            dimension_semantics=("arbitrary",), vmem_limit_bytes=VMEM_LIMIT),
        name="out_mlp",
    )(x2d, yda, ymla, w_o, g_post, g_pre, w1, w2, g_mlp)


def _position_tables(n_pos):
    inv_freq = 1.0 / (ROPE_THETA ** (np.arange(0, MLA_ROPE, 2, dtype=np.float64) / MLA_ROPE))
    ang = np.arange(n_pos, dtype=np.float64)[:, None] * inv_freq[None, :]
    cos, sin = np.cos(ang), np.sin(ang)
    zeros = np.zeros((n_pos, LANES - MLA_ROPE))
    ct = np.concatenate([cos, cos, zeros], axis=1).astype(np.float32)
    st = np.concatenate([-sin, sin, zeros], axis=1).astype(np.float32)
    ipos = np.arange(n_pos)
    hi = ((ipos // CHUNK) * CHUNK)[:, None]
    lo = (ipos % CHUNK)[:, None]
    posf = np.concatenate([hi, lo] * ALIBI_PIECES
                          + [np.zeros((n_pos, LANES - 2 * ALIBI_PIECES), np.int64)], axis=1)
    return ct, st, posf.astype(ml_dtypes.bfloat16)


def _swap_halves(w):
    half = w.shape[-1] // 2
    return jnp.concatenate([w[..., half:], w[..., :half]], axis=-1)


def kernel(x, meta_tokens, g_attn_pre, w_in, da_lambda_q1, da_lambda_k1, da_lambda_q2,
           da_lambda_k2, g_da_sub, g_mla_q, w_mla_q_up, g_mla_kv, w_mla_kv_up, w_o,
           g_attn_post, g_mlp_pre, w_ff1, w_ff2, g_mlp_post):
    nb, seq, d = x.shape
    assert d == D_MODEL and w_in.shape[0] == 1, "single-layer block only"
    assert seq % ATT_TQ == 0 and seq % PROJ_TM == 0 and ATT_TQ == ATT_TK

    w_in0 = w_in[0].astype(BF16)
    n_da = 2 * DA_WIDTH
    v_cols = slice(n_da, n_da + DA_WIDTH)
    rest = w_in0[:, n_da + DA_WIDTH:]
    kr = rest[:, MLA_Q_RANK + MLA_KV_RANK:]
    zpad = jnp.zeros((D_MODEL, LANES - MLA_ROPE), BF16)
    w_in_ext = jnp.concatenate(
        [w_in0[:, :n_da], rest[:, :MLA_Q_RANK + MLA_KV_RANK], kr, zpad, _swap_halves(kr), zpad],
        axis=1)
    w_vt = w_in0[:, v_cols].T
    wq = w_mla_q_up[0].astype(BF16).reshape(MLA_Q_RANK, MLA_HEADS, MLA_NOPE + MLA_ROPE)
    wq_rope = wq[..., MLA_NOPE:]
    zq = jnp.zeros((MLA_Q_RANK, MLA_HEADS, LANES - MLA_ROPE), BF16)
    wq_ext = jnp.concatenate(
        [wq[..., :MLA_NOPE], wq_rope, zq, _swap_halves(wq_rope), zq], axis=-1
    ).reshape(MLA_Q_RANK, MLA_HEADS * Q_UP_W)
    wkv = w_mla_kv_up[0].astype(BF16).reshape(MLA_KV_RANK, MLA_HEADS, MLA_NOPE + MLA_V)
    wkv_k = wkv[..., :MLA_NOPE].reshape(MLA_KV_RANK, MLA_HEADS * MLA_NOPE)
    wkv_vt = wkv[..., MLA_NOPE:].reshape(MLA_KV_RANK, MLA_WIDTH).T

    ct, st, posf = _position_tables(N_META + seq)
    lam_rows = jnp.concatenate(
        [da_lambda_q1, da_lambda_k1, da_lambda_q2, da_lambda_k2], axis=0).astype(F32)
    lam_params = jnp.pad(lam_rows, ((0, 8 - lam_rows.shape[0]), (0, LANES - DA_HEAD_DIM)))

    proj = functools.partial(
        _proj_call, g_pre=g_attn_pre, w_in=w_in_ext, w_vt=w_vt, g_q=g_mla_q, w_q=wq_ext,
        g_kv=g_mla_kv, w_kv=wkv_k, w_kvt=wkv_vt)
    x2d = x.reshape(nb * seq, D_MODEL)
    qda, kda, vtda, qm, km, vtm = proj(
        x2d, PROJ_TM, seq // PROJ_TM, ct=ct[N_META:], st=st[N_META:], posf=posf[N_META:],
        name="proj_tokens")
    meta_rows = jnp.pad(meta_tokens.astype(F32), ((0, META_PAD - N_META), (0, 0)))
    _, kda_meta, vtda_meta, _, km_meta, vtm_meta = proj(
        meta_rows, META_PAD, 1, ct=ct[:META_PAD], st=st[:META_PAD], posf=posf[:META_PAD],
        name="proj_meta")

    def per_batch(a):
        return a.reshape(nb, seq, a.shape[-1])

    yda = _da_call(per_batch(qda), per_batch(kda), vtda, kda_meta, vtda_meta,
                   lam_params, g_da_sub.reshape(DA_V_DIM, 1))
    ymla = _mla_call(per_batch(qm), per_batch(km), vtm, km_meta, vtm_meta)

    out = _out_call(x2d, yda.reshape(nb * seq, DA_WIDTH), ymla.reshape(nb * seq, MLA_WIDTH),
                    w_o[0].astype(BF16), g_attn_post, g_mlp_pre,
                    w_ff1[0].astype(BF16), w_ff2[0].astype(BF16), g_mlp_post)
    return out.reshape(nb, seq, D_MODEL)
```

```python
import functools
import math
from typing import Any, Callable, NamedTuple

import jax
import jax.numpy as jnp
import ml_dtypes
import numpy as np
from jax import lax
from jax.experimental import pallas as pl
from jax.experimental.pallas import tpu as pltpu

F32 = jnp.float32
BF16 = jnp.bfloat16

D_MODEL = 1024
N_META = 16
CHUNK = 64
EPS = 1e-6
NEG = -1e30
ROPE_THETA = 10000.0
LOG2E = math.log2(math.e)

DA_HEADS = 4
DA_HEAD_DIM = 64
DA_V_DIM = 128
DA_WIDTH = DA_HEADS * DA_V_DIM
MLA_HEADS = 4
MLA_NOPE = 128
MLA_ROPE = 64
MLA_V = 128
MLA_WIDTH = MLA_HEADS * MLA_V
MLA_Q_RANK = 256
MLA_KV_RANK = 128
D_FF = 4 * D_MODEL
LAMBDA_INIT = 0.8 - 0.6 * math.exp(-0.3 * 0)

LANES = 128
BF16_SUBLANES = 16
KEY_W = 2 * LANES
VT_ROWS = DA_V_DIM + BF16_SUBLANES
VMEM_LIMIT = 56 * 1024 * 1024

PROJ_TM = 1024
PROJ_ROW_GROUPS = 4
ATT_TQ = 512
ATT_TK = 512
OUT_TM = 1024
FF_CHUNK = 1024
OUT_ROW_GROUPS = 4
META_PAD = 128
DA_HEAD_GROUP = 4
MLA_HEAD_GROUP = 4

C_QDA, C_KDA, C_CQ, C_CKV, C_KR, C_KRS, C_END = (0, 512, 1024, 1280, 1408, 1536, 1664)
Q_UP_W = 3 * LANES
ALIBI_PIECES = 3


def _rms_scale(x):
    return x * lax.rsqrt(jnp.mean(x * x, axis=-1, keepdims=True) + EPS)


def _dot(a, b):
    return jnp.dot(a, b, preferred_element_type=F32)


def _dot_nt(a, b):
    return lax.dot_general(a, b, (((1,), (1,)), ((), ())), preferred_element_type=F32)


def _alibi_pieces(h):
    c = np.float32(2.0 ** (-8.0 * (h + 1) / DA_HEADS)) * np.float32(LOG2E)
    pieces, rest = [], c
    for _ in range(ALIBI_PIECES):
        piece = np.float32(ml_dtypes.bfloat16(rest))
        pieces.append(float(piece))
        rest = np.float32(rest - piece)
    assert rest == 0.0
    return float(c), pieces


def _proj_kernel(x_ref, g_ref, win_ref, wvt_ref, gq_ref, wq_ref, gkv_ref, wkv_ref, wkvt_ref,
                 ct_ref, st_ref, posf_ref,
                 qda_ref, kda_ref, vtda_ref, qm_ref, km_ref, vtm_ref):
    tm = x_ref.shape[0]
    n_groups = PROJ_ROW_GROUPS if tm % (PROJ_ROW_GROUPS * LANES) == 0 else 1
    groups = [slice(r, r + tm // n_groups) for r in range(0, tm, tm // n_groups)]
    ones = jnp.ones((BF16_SUBLANES, tm // n_groups), BF16)
    scale = (MLA_NOPE + MLA_ROPE) ** -0.5 * LOG2E

    u = [(_rms_scale(x_ref[r, :]) * g_ref[...]).astype(BF16) for r in groups]
    p = [_dot(ug, win_ref[...]) for ug in u]
    vt = [_dot_nt(wvt_ref[...], ug) for ug in u]

    for r, pg, vtg in zip(groups, p, vt):
        qda_ref[r, :] = (pg[:, C_QDA:C_KDA] * (DA_HEAD_DIM ** -0.5 * LOG2E)).astype(BF16)
        posf = posf_ref[r, :]
        for h in range(DA_HEADS):
            kda_ref[r, KEY_W * h:KEY_W * h + LANES] = (
                pg[:, C_KDA + LANES * h:C_KDA + LANES * (h + 1)].astype(BF16))
            kda_ref[r, KEY_W * h + LANES:KEY_W * (h + 1)] = posf
            vtda_ref[VT_ROWS * h:VT_ROWS * h + DA_V_DIM, r] = (
                vtg[DA_V_DIM * h:DA_V_DIM * (h + 1), :].astype(BF16))
            vtda_ref[VT_ROWS * h + DA_V_DIM:VT_ROWS * (h + 1), r] = ones

    cq = [(_rms_scale(pg[:, C_CQ:C_CKV]) * gq_ref[...]).astype(BF16) for pg in p]
    ckv = [(_rms_scale(pg[:, C_CKV:C_KR]) * gkv_ref[...]).astype(BF16) for pg in p]
    qu = [_dot(c, wq_ref[...]) for c in cq]
    kn = [_dot(c, wkv_ref[...]) for c in ckv]
    vtm = [_dot_nt(wkvt_ref[...], c) for c in ckv]
    for r, pg, qug, kng, vtmg in zip(groups, p, qu, kn, vtm):
        ct = ct_ref[r, :]
        st = st_ref[r, :]
        k_rope = (pg[:, C_KR:C_KRS] * ct + pg[:, C_KRS:C_END] * st).astype(BF16)
        for h in range(MLA_HEADS):
            b = Q_UP_W * h
            qm_ref[r, KEY_W * h:KEY_W * h + LANES] = (qug[:, b:b + LANES] * scale).astype(BF16)
            q_rope = (qug[:, b + LANES:b + 2 * LANES] * ct
                      + qug[:, b + 2 * LANES:b + 3 * LANES] * st)
            qm_ref[r, KEY_W * h + LANES:KEY_W * (h + 1)] = (q_rope * scale).astype(BF16)
            km_ref[r, KEY_W * h:KEY_W * h + LANES] = kng[:, LANES * h:LANES * (h + 1)].astype(BF16)
            km_ref[r, KEY_W * h + LANES:KEY_W * (h + 1)] = k_rope
            vtm_ref[VT_ROWS * h:VT_ROWS * h + MLA_V, r] = (
                vtmg[MLA_V * h:MLA_V * (h + 1), :].astype(BF16))
            vtm_ref[VT_ROWS * h + MLA_V:VT_ROWS * (h + 1), r] = ones


def _proj_call(x2d, tm, n_tab_blocks, g_pre, w_in, w_vt, g_q, w_q, g_kv, w_kv, w_kvt,
               ct, st, posf, name):
    rows = x2d.shape[0]
    grid = (rows // tm,)
    row = lambda i: (i, 0)
    col = lambda i: (0, i)
    const = lambda i: (0, 0)
    tab = lambda i: (i % n_tab_blocks, 0)
    row_out = lambda w: (pl.BlockSpec((tm, w), row), jax.ShapeDtypeStruct((rows, w), BF16))
    col_out = lambda r: (pl.BlockSpec((r, tm), col), jax.ShapeDtypeStruct((r, rows), BF16))
    outs = [row_out(DA_WIDTH), row_out(DA_HEADS * KEY_W), col_out(DA_HEADS * VT_ROWS),
            row_out(MLA_HEADS * KEY_W), row_out(MLA_HEADS * KEY_W), col_out(MLA_HEADS * VT_ROWS)]
    return pl.pallas_call(
        _proj_kernel,
        grid=grid,
        in_specs=[
            pl.BlockSpec((tm, D_MODEL), row),
            pl.BlockSpec((1, D_MODEL), const),
            pl.BlockSpec((D_MODEL, C_END), const),
            pl.BlockSpec((DA_WIDTH, D_MODEL), const),
            pl.BlockSpec((1, MLA_Q_RANK), const),
            pl.BlockSpec((MLA_Q_RANK, MLA_HEADS * Q_UP_W), const),
            pl.BlockSpec((1, MLA_KV_RANK), const),
            pl.BlockSpec((MLA_KV_RANK, MLA_HEADS * MLA_NOPE), const),
            pl.BlockSpec((MLA_WIDTH, MLA_KV_RANK), const),
            pl.BlockSpec((tm, LANES), tab),
            pl.BlockSpec((tm, LANES), tab),
            pl.BlockSpec((tm, LANES), tab),
        ],
        out_specs=[o[0] for o in outs],
        out_shape=[o[1] for o in outs],
        compiler_params=pltpu.CompilerParams(
            dimension_semantics=("arbitrary",), vmem_limit_bytes=VMEM_LIMIT),
        name=name,
    )(x2d, g_pre, w_in, w_vt, g_q, w_q, g_kv, w_kv, w_kvt, ct, st, posf)


def _softmax_step(s_t, vt, m_ref, acc_ref, first):
    m_blk = jnp.max(s_t, axis=0, keepdims=True)
    if first:
        m_new = m_blk
    else:
        m_prev = m_ref[...]
        m_new = jnp.maximum(m_prev, m_blk)
    p_t = jnp.exp2((s_t - m_new).astype(BF16))
    pv = _dot(vt, p_t)
    if first:
        acc_ref[...] = pv
    else:
        acc_ref[...] = jnp.exp2(m_prev - m_new) * acc_ref[...] + pv
    m_ref[...] = m_new


def _meta_rows_valid(tq):
    return lax.broadcasted_iota(jnp.int32, (META_PAD, tq), 0) < N_META


def _diag_iotas(tk, tq):
    ik = lax.broadcasted_iota(jnp.int32, (tk, tq), 0)
    iq = lax.broadcasted_iota(jnp.int32, (tk, tq), 1)
    return ik, iq


def _normalised(acc_ref, dv):
    return acc_ref[0:dv, :] / acc_ref[dv:dv + 1, :]


class _Chain(NamedTuple):
    load_q: Callable[[], jax.Array]
    kcols: Any
    vrows: Any
    diag_fn: Callable[[jax.Array], jax.Array]


def _attend(qi, chains, kmeta_ref, vtmeta_ref, k_ref, vt_ref, s_bufs, m_ref, acc_ref):
    tk = ATT_TK
    buf_a, buf_b = s_bufs
    valid = _meta_rows_valid(ATT_TQ)
    s_meta = [jnp.where(valid, _dot_nt(kmeta_ref[:, ch.kcols], ch.load_q()), NEG)
              for ch in chains]

    def key_block(j):
        return pl.ds(pl.multiple_of(j * tk, tk), tk)

    def scores_into(buf, j, c):
        ch = chains[c]
        buf[c] = _dot_nt(k_ref[0, key_block(j), ch.kcols], ch.load_q())

    def consume(buf, j, c, diagonal=False):
        ch = chains[c]
        s_t = buf[c]
        if diagonal:
            s_t = ch.diag_fn(s_t)
        _softmax_step(s_t, vt_ref[ch.vrows, key_block(j)], m_ref.at[c], acc_ref.at[c], False)

    n = len(chains)

    def all_scores_into(buf, j):
        for c in range(n):
            scores_into(buf, j, c)

    def consume_all(buf, j, diagonal=False):
        for c in range(n):
            consume(buf, j, c, diagonal)

    all_scores_into(buf_a, 0)
    for c in range(n):
        _softmax_step(s_meta[c], vtmeta_ref[chains[c].vrows, :], m_ref.at[c], acc_ref.at[c], True)

    def pair(t, carry):
        j = 2 * t
        all_scores_into(buf_b, j + 1)
        consume_all(buf_a, j)
        all_scores_into(buf_a, j + 2)
        consume_all(buf_b, j + 1)
        return carry

    lax.fori_loop(0, qi // 2, pair, 0)

    @pl.when(qi % 2 == 0)
    def _():
        consume_all(buf_a, qi, diagonal=True)

    @pl.when(qi % 2 == 1)
    def _():
        all_scores_into(buf_b, qi)
        consume_all(buf_a, qi - 1)
        consume_all(buf_b, qi, diagonal=True)


def _attention_scratch(n_chains, tq):
    stat = pltpu.VMEM((n_chains, 1, tq), F32)
    scores = pltpu.VMEM((n_chains, ATT_TK, tq), F32)
    return [stat, pltpu.VMEM((n_chains, VT_ROWS, tq), F32), scores, scores]


def _da_kernel(q_ref, k_ref, vt_ref, kmeta_ref, vtmeta_ref, lam_ref, gsub_ref, y_ref,
               qx_ref, m_ref, acc_ref, sa_ref, sb_ref):
    qi = pl.program_id(1)
    tq, tk = ATT_TQ, ATT_TK
    lane = lax.broadcasted_iota(jnp.int32, (tq, LANES), 1)
    lp = lam_ref[...]
    lam = (jnp.exp(jnp.sum(lp[0:1] * lp[1:2], axis=-1, keepdims=True))
           - jnp.exp(jnp.sum(lp[2:3] * lp[3:4], axis=-1, keepdims=True)) + LAMBDA_INIT)

    for h0 in range(0, DA_HEADS, DA_HEAD_GROUP):
        chains = []
        for g in range(DA_HEAD_GROUP):
            h = h0 + g
            c, pieces = _alibi_pieces(h)
            q = q_ref[0, :, LANES * h:LANES * (h + 1)]
            pf = jnp.zeros((tq, LANES), F32)
            for i, piece in enumerate(pieces):
                pf = jnp.where((lane == 2 * i) | (lane == 2 * i + 1), piece, pf)
            pf = pf.astype(BF16)
            zero = jnp.zeros_like(q)
            qx_ref[2 * g, :, 0:LANES] = jnp.where(lane < DA_HEAD_DIM, q, zero)
            qx_ref[2 * g + 1, :, 0:LANES] = jnp.where(lane >= DA_HEAD_DIM, q, zero)
            qx_ref[2 * g, :, LANES:KEY_W] = pf
            qx_ref[2 * g + 1, :, LANES:KEY_W] = pf

            def diag_fn(s_t, c=c):
                ik, iq = _diag_iotas(tk, tq)
                visible = (ik // CHUNK) <= (iq // CHUNK)
                corr = (2.0 * c) * jnp.minimum(iq - ik, 0).astype(F32)
                return jnp.where(visible, s_t + corr, NEG)

            for comp in range(2):
                chains.append(_Chain(
                    load_q=functools.partial(lambda i: qx_ref[i], 2 * g + comp),
                    kcols=pl.ds(KEY_W * h, KEY_W), vrows=pl.ds(VT_ROWS * h, VT_ROWS),
                    diag_fn=diag_fn))

        _attend(qi, chains, kmeta_ref, vtmeta_ref, k_ref, vt_ref, (sa_ref, sb_ref), m_ref, acc_ref)

        for g in range(DA_HEAD_GROUP):
            h = h0 + g
            o = (_normalised(acc_ref.at[2 * g], DA_V_DIM)
                 - lam * _normalised(acc_ref.at[2 * g + 1], DA_V_DIM))
            o = o * lax.rsqrt(jnp.mean(o * o, axis=0, keepdims=True) + EPS)
            o = o * gsub_ref[...] * (1.0 - LAMBDA_INIT)
            y_ref[0, :, LANES * h:LANES * (h + 1)] = o.T.astype(BF16)


def _da_call(q, k, vt, kmeta, vtmeta, lam_params, g_sub_col):
    nb, s, _ = q.shape
    tq = ATT_TQ
    n_chains = 2 * DA_HEAD_GROUP
    return pl.pallas_call(
        _da_kernel,
        grid=(nb, s // tq),
        in_specs=[
            pl.BlockSpec((1, tq, DA_WIDTH), lambda b, i: (b, i, 0)),
            pl.BlockSpec((1, s, DA_HEADS * KEY_W), lambda b, i: (b, 0, 0)),
            pl.BlockSpec((DA_HEADS * VT_ROWS, s), lambda b, i: (0, b)),
            pl.BlockSpec((META_PAD, DA_HEADS * KEY_W), lambda b, i: (0, 0)),
            pl.BlockSpec((DA_HEADS * VT_ROWS, META_PAD), lambda b, i: (0, 0)),
            pl.BlockSpec((8, LANES), lambda b, i: (0, 0)),
            pl.BlockSpec((DA_V_DIM, 1), lambda b, i: (0, 0)),
        ],
        out_specs=pl.BlockSpec((1, tq, DA_WIDTH), lambda b, i: (b, i, 0)),
        out_shape=jax.ShapeDtypeStruct((nb, s, DA_WIDTH), BF16),
        scratch_shapes=[pltpu.VMEM((n_chains, tq, KEY_W), BF16)] + _attention_scratch(n_chains, tq),
        compiler_params=pltpu.CompilerParams(
            dimension_semantics=("arbitrary", "arbitrary"), vmem_limit_bytes=VMEM_LIMIT),
        name="diff_attention",
    )(q, k, vt, kmeta, vtmeta, lam_params, g_sub_col)


def _mla_kernel(q_ref, k_ref, vt_ref, kmeta_ref, vtmeta_ref, y_ref, m_ref, acc_ref,
                sa_ref, sb_ref):
    qi = pl.program_id(1)
    tq, tk = ATT_TQ, ATT_TK

    def diag_fn(s_t):
        ik, iq = _diag_iotas(tk, tq)
        return jnp.where((ik // CHUNK) <= (iq // CHUNK), s_t, NEG)

    for h0 in range(0, MLA_HEADS, MLA_HEAD_GROUP):
        heads = range(h0, h0 + MLA_HEAD_GROUP)
        chains = [
            _Chain(load_q=functools.partial(lambda h: q_ref[0, :, KEY_W * h:KEY_W * (h + 1)], h),
                   kcols=pl.ds(KEY_W * h, KEY_W), vrows=pl.ds(VT_ROWS * h, VT_ROWS),
                   diag_fn=diag_fn)
            for h in heads]
        _attend(qi, chains, kmeta_ref, vtmeta_ref, k_ref, vt_ref, (sa_ref, sb_ref), m_ref, acc_ref)
        for g, h in enumerate(heads):
            y_ref[0, :, LANES * h:LANES * (h + 1)] = (
                _normalised(acc_ref.at[g], MLA_V).T.astype(BF16))


def _mla_call(q, k, vt, kmeta, vtmeta):
    nb, s, _ = q.shape
    tq = ATT_TQ
    return pl.pallas_call(
        _mla_kernel,
        grid=(nb, s // tq),
        in_specs=[
            pl.BlockSpec((1, tq, MLA_HEADS * KEY_W), lambda b, i: (b, i, 0)),
            pl.BlockSpec((1, s, MLA_HEADS * KEY_W), lambda b, i: (b, 0, 0)),
            pl.BlockSpec((MLA_HEADS * VT_ROWS, s), lambda b, i: (0, b)),
            pl.BlockSpec((META_PAD, MLA_HEADS * KEY_W), lambda b, i: (0, 0)),
            pl.BlockSpec((MLA_HEADS * VT_ROWS, META_PAD), lambda b, i: (0, 0)),
        ],
        out_specs=pl.BlockSpec((1, tq, MLA_WIDTH), lambda b, i: (b, i, 0)),
        out_shape=jax.ShapeDtypeStruct((nb, s, MLA_WIDTH), BF16),
        scratch_shapes=_attention_scratch(MLA_HEAD_GROUP, tq),
        compiler_params=pltpu.CompilerParams(
            dimension_semantics=("arbitrary", "arbitrary"), vmem_limit_bytes=VMEM_LIMIT),
        name="mla_attention",
    )(q, k, vt, kmeta, vtmeta)


def _out_kernel(x_ref, yda_ref, ymla_ref, wo_ref, gpost_ref, gpre_ref, w1_ref, w2_ref,
                gmlp_ref, o_ref):
    tm = x_ref.shape[0]
    groups = [slice(r, r + tm // OUT_ROW_GROUPS) for r in range(0, tm, tm // OUT_ROW_GROUPS)]
    mix = [_dot(yda_ref[r, :], wo_ref[0:DA_WIDTH, :])
           + _dot(ymla_ref[r, :], wo_ref[DA_WIDTH:DA_WIDTH + MLA_WIDTH, :]) for r in groups]
    h1 = [x_ref[r, :] + _rms_scale(m) * gpost_ref[...] for r, m in zip(groups, mix)]
    u = [(_rms_scale(h) * gpre_ref[...]).astype(BF16) for h in h1]
    for r, h, ug in zip(groups, h1, u):
        f = None
        for c in range(D_FF // FF_CHUNK):
            cols = slice(FF_CHUNK * c, FF_CHUNK * (c + 1))
            hid = jnp.square(jnp.maximum(_dot(ug, w1_ref[:, cols]), 0.0)).astype(BF16)
            part = _dot(hid, w2_ref[cols, :])
            f = part if f is None else f + part
        o_ref[r, :] = h + _rms_scale(f) * gmlp_ref[...]


def _out_call(x2d, yda, ymla, w_o, g_post, g_pre, w1, w2, g_mlp):
    rows = x2d.shape[0]
    tm = OUT_TM
    row = lambda i: (i, 0)
    const = lambda i: (0, 0)
    single = pl.Buffered(1)
    return pl.pallas_call(
        _out_kernel,
        grid=(rows // tm,),
        in_specs=[
            pl.BlockSpec((tm, D_MODEL), row),
            pl.BlockSpec((tm, DA_WIDTH), row),
            pl.BlockSpec((tm, MLA_WIDTH), row),
            pl.BlockSpec((D_MODEL, D_MODEL), const, pipeline_mode=single),
            pl.BlockSpec((1, D_MODEL), const),
            pl.BlockSpec((1, D_MODEL), const),
            pl.BlockSpec((D_MODEL, D_FF), const, pipeline_mode=single),
            pl.BlockSpec((D_FF, D_MODEL), const, pipeline_mode=single),
            pl.BlockSpec((1, D_MODEL), const),
        ],
        out_specs=pl.BlockSpec((tm, D_MODEL), row),
        out_shape=jax.ShapeDtypeStruct((rows, D_MODEL), F32),
        compiler_params=pltpu.CompilerParams(
            dimension_semantics=("arbitrary",), vmem_limit_bytes=VMEM_LIMIT),
        name="out_mlp",
    )(x2d, yda, ymla, w_o, g_post, g_pre, w1, w2, g_mlp)


def _position_tables(n_pos):
    inv_freq = 1.0 / (ROPE_THETA ** (np.arange(0, MLA_ROPE, 2, dtype=np.float64) / MLA_ROPE))
    ang = np.arange(n_pos, dtype=np.float64)[:, None] * inv_freq[None, :]
    cos, sin = np.cos(ang), np.sin(ang)
    zeros = np.zeros((n_pos, LANES - MLA_ROPE))
    ct = np.concatenate([cos, cos, zeros], axis=1).astype(np.float32)
    st = np.concatenate([-sin, sin, zeros], axis=1).astype(np.float32)
    ipos = np.arange(n_pos)
    hi = ((ipos // CHUNK) * CHUNK)[:, None]
    lo = (ipos % CHUNK)[:, None]
    posf = np.concatenate([hi, lo] * ALIBI_PIECES
                          + [np.zeros((n_pos, LANES - 2 * ALIBI_PIECES), np.int64)], axis=1)
    return ct, st, posf.astype(ml_dtypes.bfloat16)


def _swap_halves(w):
    half = w.shape[-1] // 2
    return jnp.concatenate([w[..., half:], w[..., :half]], axis=-1)


def kernel(x, meta_tokens, g_attn_pre, w_in, da_lambda_q1, da_lambda_k1, da_lambda_q2,
           da_lambda_k2, g_da_sub, g_mla_q, w_mla_q_up, g_mla_kv, w_mla_kv_up, w_o,
           g_attn_post, g_mlp_pre, w_ff1, w_ff2, g_mlp_post):
    nb, seq, d = x.shape
    assert d == D_MODEL and w_in.shape[0] == 1, "single-layer block only"
    assert seq % ATT_TQ == 0 and seq % PROJ_TM == 0 and ATT_TQ == ATT_TK

    w_in0 = w_in[0].astype(BF16)
    n_da = 2 * DA_WIDTH
    v_cols = slice(n_da, n_da + DA_WIDTH)
    rest = w_in0[:, n_da + DA_WIDTH:]
    kr = rest[:, MLA_Q_RANK + MLA_KV_RANK:]
    zpad = jnp.zeros((D_MODEL, LANES - MLA_ROPE), BF16)
    w_in_ext = jnp.concatenate(
        [w_in0[:, :n_da], rest[:, :MLA_Q_RANK + MLA_KV_RANK], kr, zpad, _swap_halves(kr), zpad],
        axis=1)
    w_vt = w_in0[:, v_cols].T
    wq = w_mla_q_up[0].astype(BF16).reshape(MLA_Q_RANK, MLA_HEADS, MLA_NOPE + MLA_ROPE)
    wq_rope = wq[..., MLA_NOPE:]
    zq = jnp.zeros((MLA_Q_RANK, MLA_HEADS, LANES - MLA_ROPE), BF16)
    wq_ext = jnp.concatenate(
        [wq[..., :MLA_NOPE], wq_rope, zq, _swap_halves(wq_rope), zq], axis=-1
    ).reshape(MLA_Q_RANK, MLA_HEADS * Q_UP_W)
    wkv = w_mla_kv_up[0].astype(BF16).reshape(MLA_KV_RANK, MLA_HEADS, MLA_NOPE + MLA_V)
    wkv_k = wkv[..., :MLA_NOPE].reshape(MLA_KV_RANK, MLA_HEADS * MLA_NOPE)
    wkv_vt = wkv[..., MLA_NOPE:].reshape(MLA_KV_RANK, MLA_WIDTH).T

    ct, st, posf = _position_tables(N_META + seq)
    lam_rows = jnp.concatenate(
        [da_lambda_q1, da_lambda_k1, da_lambda_q2, da_lambda_k2], axis=0).astype(F32)
    lam_params = jnp.pad(lam_rows, ((0, 8 - lam_rows.shape[0]), (0, LANES - DA_HEAD_DIM)))

    proj = functools.partial(
        _proj_call, g_pre=g_attn_pre, w_in=w_in_ext, w_vt=w_vt, g_q=g_mla_q, w_q=wq_ext,
        g_kv=g_mla_kv, w_kv=wkv_k, w_kvt=wkv_vt)
    x2d = x.reshape(nb * seq, D_MODEL)
    qda, kda, vtda, qm, km, vtm = proj(
        x2d, PROJ_TM, seq // PROJ_TM, ct=ct[N_META:], st=st[N_META:], posf=posf[N_META:],
        name="proj_tokens")
    meta_rows = jnp.pad(meta_tokens.astype(F32), ((0, META_PAD - N_META), (0, 0)))
    _, kda_meta, vtda_meta, _, km_meta, vtm_meta = proj(
        meta_rows, META_PAD, 1, ct=ct[:META_PAD], st=st[:META_PAD], posf=posf[:META_PAD],
        name="proj_meta")

    def per_batch(a):
        return a.reshape(nb, seq, a.shape[-1])

    yda = _da_call(per_batch(qda), per_batch(kda), vtda, kda_meta, vtda_meta,
                   lam_params, g_da_sub.reshape(DA_V_DIM, 1))
    ymla = _mla_call(per_batch(qm), per_batch(km), vtm, km_meta, vtm_meta)

    out = _out_call(x2d, yda.reshape(nb * seq, DA_WIDTH), ymla.reshape(nb * seq, MLA_WIDTH),
                    w_o[0].astype(BF16), g_attn_post, g_mlp_pre,
                    w_ff1[0].astype(BF16), w_ff2[0].astype(BF16), g_mlp_post)
    return out.reshape(nb, seq, D_MODEL)
```

```python
import functools
import math
from typing import Any, Callable, NamedTuple

import jax
import jax.numpy as jnp
import ml_dtypes
import numpy as np
from jax import lax
from jax.experimental import pallas as pl
from jax.experimental.pallas import tpu as pltpu

F32 = jnp.float32
BF16 = jnp.bfloat16

D_MODEL = 1024
N_META = 16
CHUNK = 64
EPS = 1e-6
NEG = -1e30
ROPE_THETA = 10000.0
LOG2E = math.log2(math.e)

DA_HEADS = 4
DA_HEAD_DIM = 64
DA_V_DIM = 128
DA_WIDTH = DA_HEADS * DA_V_DIM
MLA_HEADS = 4
MLA_NOPE = 128
MLA_ROPE = 64
MLA_V = 128
MLA_WIDTH = MLA_HEADS * MLA_V
MLA_Q_RANK = 256
MLA_KV_RANK = 128
D_FF = 4 * D_MODEL
LAMBDA_INIT = 0.8 - 0.6 * math.exp(-0.3 * 0)

LANES = 128
BF16_SUBLANES = 16
KEY_W = 2 * LANES
VT_ROWS = DA_V_DIM + BF16_SUBLANES
VMEM_LIMIT = 56 * 1024 * 1024

PROJ_TM = 1024
PROJ_ROW_GROUPS = 4
ATT_TQ = 512
ATT_TK = 512
OUT_TM = 1024
FF_CHUNK = 1024
OUT_ROW_GROUPS = 4
META_PAD = 128
DA_HEAD_GROUP = 4
MLA_HEAD_GROUP = 4

C_QDA, C_KDA, C_CQ, C_CKV, C_KR, C_KRS, C_END = (0, 512, 1024, 1280, 1408, 1536, 1664)
Q_UP_W = 3 * LANES
ALIBI_PIECES = 3


def _rms_scale(x):
    return x * lax.rsqrt(jnp.mean(x * x, axis=-1, keepdims=True) + EPS)


def _dot(a, b):
    return jnp.dot(a, b, preferred_element_type=F32)


def _dot_nt(a, b):
    return lax.dot_general(a, b, (((1,), (1,)), ((), ())), preferred_element_type=F32)


def _alibi_pieces(h):
    c = np.float32(2.0 ** (-8.0 * (h + 1) / DA_HEADS)) * np.float32(LOG2E)
    pieces, rest = [], c
    for _ in range(ALIBI_PIECES):
        piece = np.float32(ml_dtypes.bfloat16(rest))
        pieces.append(float(piece))
        rest = np.float32(rest - piece)
    assert rest == 0.0
    return float(c), pieces


def _proj_kernel(x_ref, g_ref, win_ref, wvt_ref, gq_ref, wq_ref, gkv_ref, wkv_ref, wkvt_ref,
                 ct_ref, st_ref, posf_ref,
                 qda_ref, kda_ref, vtda_ref, qm_ref, km_ref, vtm_ref):
    tm = x_ref.shape[0]
    n_groups = PROJ_ROW_GROUPS if tm % (PROJ_ROW_GROUPS * LANES) == 0 else 1
    groups = [slice(r, r + tm // n_groups) for r in range(0, tm, tm // n_groups)]
    ones = jnp.ones((BF16_SUBLANES, tm // n_groups), BF16)
    scale = (MLA_NOPE + MLA_ROPE) ** -0.5 * LOG2E

    u = [(_rms_scale(x_ref[r, :]) * g_ref[...]).astype(BF16) for r in groups]
    p = [_dot(ug, win_ref[...]) for ug in u]
    vt = [_dot_nt(wvt_ref[...], ug) for ug in u]

    for r, pg, vtg in zip(groups, p, vt):
        qda_ref[r, :] = (pg[:, C_QDA:C_KDA] * (DA_HEAD_DIM ** -0.5 * LOG2E)).astype(BF16)
        posf = posf_ref[r, :]
        for h in range(DA_HEADS):
            kda_ref[r, KEY_W * h:KEY_W * h + LANES] = (
                pg[:, C_KDA + LANES * h:C_KDA + LANES * (h + 1)].astype(BF16))
            kda_ref[r, KEY_W * h + LANES:KEY_W * (h + 1)] = posf
            vtda_ref[VT_ROWS * h:VT_ROWS * h + DA_V_DIM, r] = (
                vtg[DA_V_DIM * h:DA_V_DIM * (h + 1), :].astype(BF16))
            vtda_ref[VT_ROWS * h + DA_V_DIM:VT_ROWS * (h + 1), r] = ones

    cq = [(_rms_scale(pg[:, C_CQ:C_CKV]) * gq_ref[...]).astype(BF16) for pg in p]
    ckv = [(_rms_scale(pg[:, C_CKV:C_KR]) * gkv_ref[...]).astype(BF16) for pg in p]
    qu = [_dot(c, wq_ref[...]) for c in cq]
    kn = [_dot(c, wkv_ref[...]) for c in ckv]
    vtm = [_dot_nt(wkvt_ref[...], c) for c in ckv]
    for r, pg, qug, kng, vtmg in zip(groups, p, qu, kn, vtm):
        ct = ct_ref[r, :]
        st = st_ref[r, :]
        k_rope = (pg[:, C_KR:C_KRS] * ct + pg[:, C_KRS:C_END] * st).astype(BF16)
        for h in range(MLA_HEADS):
            b = Q_UP_W * h
            qm_ref[r, KEY_W * h:KEY_W * h + LANES] = (qug[:, b:b + LANES] * scale).astype(BF16)
            q_rope = (qug[:, b + LANES:b + 2 * LANES] * ct
                      + qug[:, b + 2 * LANES:b + 3 * LANES] * st)
            qm_ref[r, KEY_W * h + LANES:KEY_W * (h + 1)] = (q_rope * scale).astype(BF16)
            km_ref[r, KEY_W * h:KEY_W * h + LANES] = kng[:, LANES * h:LANES * (h + 1)].astype(BF16)
            km_ref[r, KEY_W * h + LANES:KEY_W * (h + 1)] = k_rope
            vtm_ref[VT_ROWS * h:VT_ROWS * h + MLA_V, r] = (
                vtmg[MLA_V * h:MLA_V * (h + 1), :].astype(BF16))
            vtm_ref[VT_ROWS * h + MLA_V:VT_ROWS * (h + 1), r] = ones


def _proj_call(x2d, tm, n_tab_blocks, g_pre, w_in, w_vt, g_q, w_q, g_kv, w_kv, w_kvt,
               ct, st, posf, name):
    rows = x2d.shape[0]
    grid = (rows // tm,)
    row = lambda i: (i, 0)
    col = lambda i: (0, i)
    const = lambda i: (0, 0)
    tab = lambda i: (i % n_tab_blocks, 0)
    row_out = lambda w: (pl.BlockSpec((tm, w), row), jax.ShapeDtypeStruct((rows, w), BF16))
    col_out = lambda r: (pl.BlockSpec((r, tm), col), jax.ShapeDtypeStruct((r, rows), BF16))
    outs = [row_out(DA_WIDTH), row_out(DA_HEADS * KEY_W), col_out(DA_HEADS * VT_ROWS),
            row_out(MLA_HEADS * KEY_W), row_out(MLA_HEADS * KEY_W), col_out(MLA_HEADS * VT_ROWS)]
    return pl.pallas_call(
        _proj_kernel,
        grid=grid,
        in_specs=[
            pl.BlockSpec((tm, D_MODEL), row),
            pl.BlockSpec((1, D_MODEL), const),
            pl.BlockSpec((D_MODEL, C_END), const),
            pl.BlockSpec((DA_WIDTH, D_MODEL), const),
            pl.BlockSpec((1, MLA_Q_RANK), const),
            pl.BlockSpec((MLA_Q_RANK, MLA_HEADS * Q_UP_W), const),
            pl.BlockSpec((1, MLA_KV_RANK), const),
            pl.BlockSpec((MLA_KV_RANK, MLA_HEADS * MLA_NOPE), const),
            pl.BlockSpec((MLA_WIDTH, MLA_KV_RANK), const),
            pl.BlockSpec((tm, LANES), tab),
            pl.BlockSpec((tm, LANES), tab),
            pl.BlockSpec((tm, LANES), tab),
        ],
        out_specs=[o[0] for o in outs],
        out_shape=[o[1] for o in outs],
        compiler_params=pltpu.CompilerParams(
            dimension_semantics=("arbitrary",), vmem_limit_bytes=VMEM_LIMIT),
        name=name,
    )(x2d, g_pre, w_in, w_vt, g_q, w_q, g_kv, w_kv, w_kvt, ct, st, posf)


def _softmax_step(s_t, vt, m_ref, acc_ref, first):
    m_blk = jnp.max(s_t, axis=0, keepdims=True)
    if first:
        m_new = m_blk
    else:
        m_prev = m_ref[...]
        m_new = jnp.maximum(m_prev, m_blk)
    p_t = jnp.exp2((s_t - m_new).astype(BF16))
    pv = _dot(vt, p_t)
    if first:
        acc_ref[...] = pv
    else:
        acc_ref[...] = jnp.exp2(m_prev - m_new) * acc_ref[...] + pv
    m_ref[...] = m_new


def _meta_rows_valid(tq):
    return lax.broadcasted_iota(jnp.int32, (META_PAD, tq), 0) < N_META


def _diag_iotas(tk, tq):
    ik = lax.broadcasted_iota(jnp.int32, (tk, tq), 0)
    iq = lax.broadcasted_iota(jnp.int32, (tk, tq), 1)
    return ik, iq


def _normalised(acc_ref, dv):
    return acc_ref[0:dv, :] / acc_ref[dv:dv + 1, :]


class _Chain(NamedTuple):
    load_q: Callable[[], jax.Array]
    kcols: Any
    vrows: Any
    diag_fn: Callable[[jax.Array, int], jax.Array]


def _attend(qi, chains, kmeta_ref, vtmeta_ref, k_ref, vt_ref, s_bufs, m_ref, acc_ref,
            on_chain_done):
    tk = ATT_TK
    buf_a, buf_b = s_bufs
    valid = _meta_rows_valid(ATT_TQ)
    s_meta = [jnp.where(valid, _dot_nt(kmeta_ref[:, ch.kcols], ch.load_q()), NEG)
              for ch in chains]

    def key_block(j):
        return pl.ds(pl.multiple_of(j * tk, tk), tk)

    def scores_into(buf, j, c):
        ch = chains[c]
        buf[c] = _dot_nt(k_ref[0, key_block(j), ch.kcols], ch.load_q())

    def consume(buf, j, c):
        ch = chains[c]
        _softmax_step(buf[c], vt_ref[ch.vrows, key_block(j)], m_ref.at[c], acc_ref.at[c], False)

    def consume_diagonal(buf, c):
        ch = chains[c]
        half = ATT_TQ // 2
        for q0, n_keys in ((0, half), (half, tk)):
            lanes = pl.ds(q0, half)
            keys = pl.ds(pl.multiple_of(qi * tk, tk), n_keys)
            s_t = ch.diag_fn(buf[c, 0:n_keys, q0:q0 + half], q0)
            _softmax_step(s_t, vt_ref[ch.vrows, keys], m_ref.at[c, :, lanes],
                          acc_ref.at[c, :, lanes], False)

    n = len(chains)

    def all_scores_into(buf, j):
        for c in range(n):
            scores_into(buf, j, c)

    def consume_all(buf, j):
        for c in range(n):
            consume(buf, j, c)

    def finish(buf):
        for c in range(n):
            consume_diagonal(buf, c)
            on_chain_done(c)

    all_scores_into(buf_a, 0)
    for c in range(n):
        _softmax_step(s_meta[c], vtmeta_ref[chains[c].vrows, :], m_ref.at[c], acc_ref.at[c], True)

    def pair(t, carry):
        j = 2 * t
        all_scores_into(buf_b, j + 1)
        consume_all(buf_a, j)
        all_scores_into(buf_a, j + 2)
        consume_all(buf_b, j + 1)
        return carry

    lax.fori_loop(0, qi // 2, pair, 0)

    @pl.when(qi % 2 == 0)
    def _():
        finish(buf_a)

    @pl.when(qi % 2 == 1)
    def _():
        all_scores_into(buf_b, qi)
        consume_all(buf_a, qi - 1)
        finish(buf_b)


def _attention_scratch(n_chains, tq):
    stat = pltpu.VMEM((n_chains, 1, tq), F32)
    scores = pltpu.VMEM((n_chains, ATT_TK, tq), F32)
    return [stat, pltpu.VMEM((n_chains, VT_ROWS, tq), F32), scores, scores]


def _da_kernel(q_ref, k_ref, vt_ref, kmeta_ref, vtmeta_ref, lam_ref, gsub_ref, y_ref,
               qx_ref, m_ref, acc_ref, sa_ref, sb_ref):
    qi = pl.program_id(1)
    tq, tk = ATT_TQ, ATT_TK
    lane = lax.broadcasted_iota(jnp.int32, (tq, LANES), 1)
    lp = lam_ref[...]
    lam = (jnp.exp(jnp.sum(lp[0:1] * lp[1:2], axis=-1, keepdims=True))
           - jnp.exp(jnp.sum(lp[2:3] * lp[3:4], axis=-1, keepdims=True)) + LAMBDA_INIT)

    for h0 in range(0, DA_HEADS, DA_HEAD_GROUP):
        chains = []
        for g in range(DA_HEAD_GROUP):
            h = h0 + g
            c, pieces = _alibi_pieces(h)
            q = q_ref[0, :, LANES * h:LANES * (h + 1)]
            pf = jnp.zeros((tq, LANES), F32)
            for i, piece in enumerate(pieces):
                pf = jnp.where((lane == 2 * i) | (lane == 2 * i + 1), piece, pf)
            pf = pf.astype(BF16)
            zero = jnp.zeros_like(q)
            qx_ref[2 * g, :, 0:LANES] = jnp.where(lane < DA_HEAD_DIM, q, zero)
            qx_ref[2 * g + 1, :, 0:LANES] = jnp.where(lane >= DA_HEAD_DIM, q, zero)
            qx_ref[2 * g, :, LANES:KEY_W] = pf
            qx_ref[2 * g + 1, :, LANES:KEY_W] = pf

            def diag_fn(s_t, q0, c=c):
                ik, iq = _diag_iotas(*s_t.shape)
                iq = iq + q0
                visible = (ik // CHUNK) <= (iq // CHUNK)
                corr = (2.0 * c) * jnp.minimum(iq - ik, 0).astype(F32)
                return jnp.where(visible, s_t + corr, NEG)

            for comp in range(2):
                chains.append(_Chain(
                    load_q=functools.partial(lambda i: qx_ref[i], 2 * g + comp),
                    kcols=pl.ds(KEY_W * h, KEY_W), vrows=pl.ds(VT_ROWS * h, VT_ROWS),
                    diag_fn=diag_fn))

        def finish_head(c, h0=h0):
            if c % 2 == 0:
                return
            h = h0 + c // 2
            o = (_normalised(acc_ref.at[c - 1], DA_V_DIM)
                 - lam * _normalised(acc_ref.at[c], DA_V_DIM))
            o = o * lax.rsqrt(jnp.mean(o * o, axis=0, keepdims=True) + EPS)
            o = o * gsub_ref[...] * (1.0 - LAMBDA_INIT)
            y_ref[0, :, LANES * h:LANES * (h + 1)] = o.T.astype(BF16)

        _attend(qi, chains, kmeta_ref, vtmeta_ref, k_ref, vt_ref, (sa_ref, sb_ref), m_ref, acc_ref,
                finish_head)


def _da_call(q, k, vt, kmeta, vtmeta, lam_params, g_sub_col):
    nb, s, _ = q.shape
    tq = ATT_TQ
    n_chains = 2 * DA_HEAD_GROUP
    return pl.pallas_call(
        _da_kernel,
        grid=(nb, s // tq),
        in_specs=[
            pl.BlockSpec((1, tq, DA_WIDTH), lambda b, i: (b, i, 0)),
            pl.BlockSpec((1, s, DA_HEADS * KEY_W), lambda b, i: (b, 0, 0)),
            pl.BlockSpec((DA_HEADS * VT_ROWS, s), lambda b, i: (0, b)),
            pl.BlockSpec((META_PAD, DA_HEADS * KEY_W), lambda b, i: (0, 0)),
            pl.BlockSpec((DA_HEADS * VT_ROWS, META_PAD), lambda b, i: (0, 0)),
            pl.BlockSpec((8, LANES), lambda b, i: (0, 0)),
            pl.BlockSpec((DA_V_DIM, 1), lambda b, i: (0, 0)),
        ],
        out_specs=pl.BlockSpec((1, tq, DA_WIDTH), lambda b, i: (b, i, 0)),
        out_shape=jax.ShapeDtypeStruct((nb, s, DA_WIDTH), BF16),
        scratch_shapes=[pltpu.VMEM((n_chains, tq, KEY_W), BF16)] + _attention_scratch(n_chains, tq),
        compiler_params=pltpu.CompilerParams(
            dimension_semantics=("arbitrary", "arbitrary"), vmem_limit_bytes=VMEM_LIMIT),
        name="diff_attention",
    )(q, k, vt, kmeta, vtmeta, lam_params, g_sub_col)


def _mla_kernel(q_ref, k_ref, vt_ref, kmeta_ref, vtmeta_ref, y_ref, m_ref, acc_ref,
                sa_ref, sb_ref):
    qi = pl.program_id(1)
    tq, tk = ATT_TQ, ATT_TK

    def diag_fn(s_t, q0):
        ik, iq = _diag_iotas(*s_t.shape)
        return jnp.where((ik // CHUNK) <= ((iq + q0) // CHUNK), s_t, NEG)

    for h0 in range(0, MLA_HEADS, MLA_HEAD_GROUP):
        heads = range(h0, h0 + MLA_HEAD_GROUP)
        chains = [
            _Chain(load_q=functools.partial(lambda h: q_ref[0, :, KEY_W * h:KEY_W * (h + 1)], h),
                   kcols=pl.ds(KEY_W * h, KEY_W), vrows=pl.ds(VT_ROWS * h, VT_ROWS),
                   diag_fn=diag_fn)
            for h in heads]
        def finish_head(c, h0=h0):
            h = h0 + c
            y_ref[0, :, LANES * h:LANES * (h + 1)] = (
                _normalised(acc_ref.at[c], MLA_V).T.astype(BF16))

        _attend(qi, chains, kmeta_ref, vtmeta_ref, k_ref, vt_ref, (sa_ref, sb_ref), m_ref, acc_ref,
                finish_head)


def _mla_call(q, k, vt, kmeta, vtmeta):
    nb, s, _ = q.shape
    tq = ATT_TQ
    return pl.pallas_call(
        _mla_kernel,
        grid=(nb, s // tq),
        in_specs=[
            pl.BlockSpec((1, tq, MLA_HEADS * KEY_W), lambda b, i: (b, i, 0)),
            pl.BlockSpec((1, s, MLA_HEADS * KEY_W), lambda b, i: (b, 0, 0)),
            pl.BlockSpec((MLA_HEADS * VT_ROWS, s), lambda b, i: (0, b)),
            pl.BlockSpec((META_PAD, MLA_HEADS * KEY_W), lambda b, i: (0, 0)),
            pl.BlockSpec((MLA_HEADS * VT_ROWS, META_PAD), lambda b, i: (0, 0)),
        ],
        out_specs=pl.BlockSpec((1, tq, MLA_WIDTH), lambda b, i: (b, i, 0)),
        out_shape=jax.ShapeDtypeStruct((nb, s, MLA_WIDTH), BF16),
        scratch_shapes=_attention_scratch(MLA_HEAD_GROUP, tq),
        compiler_params=pltpu.CompilerParams(
            dimension_semantics=("arbitrary", "arbitrary"), vmem_limit_bytes=VMEM_LIMIT),
        name="mla_attention",
    )(q, k, vt, kmeta, vtmeta)


def _out_kernel(x_ref, yda_ref, ymla_ref, wo_ref, gpost_ref, gpre_ref, w1_ref, w2_ref,
                gmlp_ref, o_ref):
    tm = x_ref.shape[0]
    groups = [slice(r, r + tm // OUT_ROW_GROUPS) for r in range(0, tm, tm // OUT_ROW_GROUPS)]
    mix = [_dot(yda_ref[r, :], wo_ref[0:DA_WIDTH, :])
           + _dot(ymla_ref[r, :], wo_ref[DA_WIDTH:DA_WIDTH + MLA_WIDTH, :]) for r in groups]
    h1 = [x_ref[r, :] + _rms_scale(m) * gpost_ref[...] for r, m in zip(groups, mix)]
    u = [(_rms_scale(h) * gpre_ref[...]).astype(BF16) for h in h1]
    for r, h, ug in zip(groups, h1, u):
        f = None
        for c in range(D_FF // FF_CHUNK):
            cols = slice(FF_CHUNK * c, FF_CHUNK * (c + 1))
            hid = jnp.square(jnp.maximum(_dot(ug, w1_ref[:, cols]), 0.0)).astype(BF16)
            part = _dot(hid, w2_ref[cols, :])
            f = part if f is None else f + part
        o_ref[r, :] = h + _rms_scale(f) * gmlp_ref[...]


def _out_call(x2d, yda, ymla, w_o, g_post, g_pre, w1, w2, g_mlp):
    rows = x2d.shape[0]
    tm = OUT_TM
    row = lambda i: (i, 0)
    const = lambda i: (0, 0)
    single = pl.Buffered(1)
    return pl.pallas_call(
        _out_kernel,
        grid=(rows // tm,),
        in_specs=[
            pl.BlockSpec((tm, D_MODEL), row),
            pl.BlockSpec((tm, DA_WIDTH), row),
            pl.BlockSpec((tm, MLA_WIDTH), row),
            pl.BlockSpec((D_MODEL, D_MODEL), const, pipeline_mode=single),
            pl.BlockSpec((1, D_MODEL), const),
            pl.BlockSpec((1, D_MODEL), const),
            pl.BlockSpec((D_MODEL, D_FF), const, pipeline_mode=single),
            pl.BlockSpec((D_FF, D_MODEL), const, pipeline_mode=single),
            pl.BlockSpec((1, D_MODEL), const),
        ],
        out_specs=pl.BlockSpec((tm, D_MODEL), row),
        out_shape=jax.ShapeDtypeStruct((rows, D_MODEL), F32),
        compiler_params=pltpu.CompilerParams(
            dimension_semantics=("arbitrary",), vmem_limit_bytes=VMEM_LIMIT),
        name="out_mlp",
    )(x2d, yda, ymla, w_o, g_post, g_pre, w1, w2, g_mlp)


def _position_tables(n_pos):
    inv_freq = 1.0 / (ROPE_THETA ** (np.arange(0, MLA_ROPE, 2, dtype=np.float64) / MLA_ROPE))
    ang = np.arange(n_pos, dtype=np.float64)[:, None] * inv_freq[None, :]
    cos, sin = np.cos(ang), np.sin(ang)
    zeros = np.zeros((n_pos, LANES - MLA_ROPE))
    ct = np.concatenate([cos, cos, zeros], axis=1).astype(np.float32)
    st = np.concatenate([-sin, sin, zeros], axis=1).astype(np.float32)
    ipos = np.arange(n_pos)
    hi = ((ipos // CHUNK) * CHUNK)[:, None]
    lo = (ipos % CHUNK)[:, None]
    posf = np.concatenate([hi, lo] * ALIBI_PIECES
                          + [np.zeros((n_pos, LANES - 2 * ALIBI_PIECES), np.int64)], axis=1)
    return ct, st, posf.astype(ml_dtypes.bfloat16)


def _swap_halves(w):
    half = w.shape[-1] // 2
    return jnp.concatenate([w[..., half:], w[..., :half]], axis=-1)


def kernel(x, meta_tokens, g_attn_pre, w_in, da_lambda_q1, da_lambda_k1, da_lambda_q2,
           da_lambda_k2, g_da_sub, g_mla_q, w_mla_q_up, g_mla_kv, w_mla_kv_up, w_o,
           g_attn_post, g_mlp_pre, w_ff1, w_ff2, g_mlp_post):
    nb, seq, d = x.shape
    assert d == D_MODEL and w_in.shape[0] == 1, "single-layer block only"
    assert seq % ATT_TQ == 0 and seq % PROJ_TM == 0 and ATT_TQ == ATT_TK

    w_in0 = w_in[0].astype(BF16)
    n_da = 2 * DA_WIDTH
    v_cols = slice(n_da, n_da + DA_WIDTH)
    rest = w_in0[:, n_da + DA_WIDTH:]
    kr = rest[:, MLA_Q_RANK + MLA_KV_RANK:]
    zpad = jnp.zeros((D_MODEL, LANES - MLA_ROPE), BF16)
    w_in_ext = jnp.concatenate(
        [w_in0[:, :n_da], rest[:, :MLA_Q_RANK + MLA_KV_RANK], kr, zpad, _swap_halves(kr), zpad],
        axis=1)
    w_vt = w_in0[:, v_cols].T
    wq = w_mla_q_up[0].astype(BF16).reshape(MLA_Q_RANK, MLA_HEADS, MLA_NOPE + MLA_ROPE)
    wq_rope = wq[..., MLA_NOPE:]
    zq = jnp.zeros((MLA_Q_RANK, MLA_HEADS, LANES - MLA_ROPE), BF16)
    wq_ext = jnp.concatenate(
        [wq[..., :MLA_NOPE], wq_rope, zq, _swap_halves(wq_rope), zq], axis=-1
    ).reshape(MLA_Q_RANK, MLA_HEADS * Q_UP_W)
    wkv = w_mla_kv_up[0].astype(BF16).reshape(MLA_KV_RANK, MLA_HEADS, MLA_NOPE + MLA_V)
    wkv_k = wkv[..., :MLA_NOPE].reshape(MLA_KV_RANK, MLA_HEADS * MLA_NOPE)
    wkv_vt = wkv[..., MLA_NOPE:].reshape(MLA_KV_RANK, MLA_WIDTH).T

    ct, st, posf = _position_tables(N_META + seq)
    lam_rows = jnp.concatenate(
        [da_lambda_q1, da_lambda_k1, da_lambda_q2, da_lambda_k2], axis=0).astype(F32)
    lam_params = jnp.pad(lam_rows, ((0, 8 - lam_rows.shape[0]), (0, LANES - DA_HEAD_DIM)))

    proj = functools.partial(
        _proj_call, g_pre=g_attn_pre, w_in=w_in_ext, w_vt=w_vt, g_q=g_mla_q, w_q=wq_ext,
        g_kv=g_mla_kv, w_kv=wkv_k, w_kvt=wkv_vt)
    x2d = x.reshape(nb * seq, D_MODEL)
    qda, kda, vtda, qm, km, vtm = proj(
        x2d, PROJ_TM, seq // PROJ_TM, ct=ct[N_META:], st=st[N_META:], posf=posf[N_META:],
        name="proj_tokens")
    meta_rows = jnp.pad(meta_tokens.astype(F32), ((0, META_PAD - N_META), (0, 0)))
    _, kda_meta, vtda_meta, _, km_meta, vtm_meta = proj(
        meta_rows, META_PAD, 1, ct=ct[:META_PAD], st=st[:META_PAD], posf=posf[:META_PAD],
        name="proj_meta")

    def per_batch(a):
        return a.reshape(nb, seq, a.shape[-1])

    yda = _da_call(per_batch(qda), per_batch(kda), vtda, kda_meta, vtda_meta,
                   lam_params, g_da_sub.reshape(DA_V_DIM, 1))
    ymla = _mla_call(per_batch(qm), per_batch(km), vtm, km_meta, vtm_meta)

    out = _out_call(x2d, yda.reshape(nb * seq, DA_WIDTH), ymla.reshape(nb * seq, MLA_WIDTH),
                    w_o[0].astype(BF16), g_attn_post, g_mlp_pre,
                    w_ff1[0].astype(BF16), w_ff2[0].astype(BF16), g_mlp_post)
    return out.reshape(nb, seq, D_MODEL)
```

```python
import functools
import math
from typing import Any, Callable, NamedTuple

import jax
import jax.numpy as jnp
import ml_dtypes
import numpy as np
from jax import lax
from jax.experimental import pallas as pl
from jax.experimental.pallas import tpu as pltpu

F32 = jnp.float32
BF16 = jnp.bfloat16

D_MODEL = 1024
N_META = 16
CHUNK = 64
EPS = 1e-6
NEG = -1e30
ROPE_THETA = 10000.0
LOG2E = math.log2(math.e)

DA_HEADS = 4
DA_HEAD_DIM = 64
DA_V_DIM = 128
DA_WIDTH = DA_HEADS * DA_V_DIM
MLA_HEADS = 4
MLA_NOPE = 128
MLA_ROPE = 64
MLA_V = 128
MLA_WIDTH = MLA_HEADS * MLA_V
MLA_Q_RANK = 256
MLA_KV_RANK = 128
D_FF = 4 * D_MODEL
LAMBDA_INIT = 0.8 - 0.6 * math.exp(-0.3 * 0)

LANES = 128
BF16_SUBLANES = 16
KEY_W = 2 * LANES
VT_ROWS = DA_V_DIM + BF16_SUBLANES
VMEM_LIMIT = 58 * 1024 * 1024

PROJ_TM = 1024
PROJ_ROW_GROUPS = 4
ATT_TQ = 512
ATT_TK = 512
OUT_TM = 1024
FF_CHUNK = 1024
OUT_ROW_GROUPS = 4
META_PAD = 128
DA_HEAD_GROUP = 4
MLA_HEAD_GROUP = 4

C_QDA, C_KDA, C_CQ, C_CKV, C_KR, C_KRS, C_END = (0, 512, 1024, 1280, 1408, 1536, 1664)
Q_UP_W = 3 * LANES
ALIBI_PIECES = 3


def _rms_scale(x):
    return x * lax.rsqrt(jnp.mean(x * x, axis=-1, keepdims=True) + EPS)


def _dot(a, b):
    return jnp.dot(a, b, preferred_element_type=F32)


def _dot_nt(a, b):
    return lax.dot_general(a, b, (((1,), (1,)), ((), ())), preferred_element_type=F32)


def _alibi_pieces(h):
    c = np.float32(2.0 ** (-8.0 * (h + 1) / DA_HEADS)) * np.float32(LOG2E)
    pieces, rest = [], c
    for _ in range(ALIBI_PIECES):
        piece = np.float32(ml_dtypes.bfloat16(rest))
        pieces.append(float(piece))
        rest = np.float32(rest - piece)
    assert rest == 0.0
    return float(c), pieces


def _proj_kernel(x_ref, g_ref, win_ref, wvt_ref, gq_ref, wq_ref, gkv_ref, wkv_ref, wkvt_ref,
                 ct_ref, st_ref, posf_ref,
                 qda_ref, kda_ref, vtda_ref, qm_ref, km_ref, vtm_ref):
    tm = x_ref.shape[0]
    n_groups = PROJ_ROW_GROUPS if tm % (PROJ_ROW_GROUPS * LANES) == 0 else 1
    groups = [slice(r, r + tm // n_groups) for r in range(0, tm, tm // n_groups)]
    ones = jnp.ones((BF16_SUBLANES, tm // n_groups), BF16)
    scale = (MLA_NOPE + MLA_ROPE) ** -0.5 * LOG2E

    u = [(_rms_scale(x_ref[r, :]) * g_ref[...]).astype(BF16) for r in groups]
    p = [_dot(ug, win_ref[...]) for ug in u]
    vt = [_dot_nt(wvt_ref[...], ug) for ug in u]

    for r, pg, vtg in zip(groups, p, vt):
        qda_ref[r, :] = (pg[:, C_QDA:C_KDA] * (DA_HEAD_DIM ** -0.5 * LOG2E)).astype(BF16)
        posf = posf_ref[r, :]
        for h in range(DA_HEADS):
            kda_ref[r, KEY_W * h:KEY_W * h + LANES] = (
                pg[:, C_KDA + LANES * h:C_KDA + LANES * (h + 1)].astype(BF16))
            kda_ref[r, KEY_W * h + LANES:KEY_W * (h + 1)] = posf
            vtda_ref[VT_ROWS * h:VT_ROWS * h + DA_V_DIM, r] = (
                vtg[DA_V_DIM * h:DA_V_DIM * (h + 1), :].astype(BF16))
            vtda_ref[VT_ROWS * h + DA_V_DIM:VT_ROWS * (h + 1), r] = ones

    cq = [(_rms_scale(pg[:, C_CQ:C_CKV]) * gq_ref[...]).astype(BF16) for pg in p]
    ckv = [(_rms_scale(pg[:, C_CKV:C_KR]) * gkv_ref[...]).astype(BF16) for pg in p]
    qu = [_dot(c, wq_ref[...]) for c in cq]
    kn = [_dot(c, wkv_ref[...]) for c in ckv]
    vtm = [_dot_nt(wkvt_ref[...], c) for c in ckv]
    for r, pg, qug, kng, vtmg in zip(groups, p, qu, kn, vtm):
        ct = ct_ref[r, :]
        st = st_ref[r, :]
        k_rope = (pg[:, C_KR:C_KRS] * ct + pg[:, C_KRS:C_END] * st).astype(BF16)
        for h in range(MLA_HEADS):
            b = Q_UP_W * h
            qm_ref[r, KEY_W * h:KEY_W * h + LANES] = (qug[:, b:b + LANES] * scale).astype(BF16)
            q_rope = (qug[:, b + LANES:b + 2 * LANES] * ct
                      + qug[:, b + 2 * LANES:b + 3 * LANES] * st)
            qm_ref[r, KEY_W * h + LANES:KEY_W * (h + 1)] = (q_rope * scale).astype(BF16)
            km_ref[r, KEY_W * h:KEY_W * h + LANES] = kng[:, LANES * h:LANES * (h + 1)].astype(BF16)
            km_ref[r, KEY_W * h + LANES:KEY_W * (h + 1)] = k_rope
            vtm_ref[VT_ROWS * h:VT_ROWS * h + MLA_V, r] = (
                vtmg[MLA_V * h:MLA_V * (h + 1), :].astype(BF16))
            vtm_ref[VT_ROWS * h + MLA_V:VT_ROWS * (h + 1), r] = ones


def _proj_call(x2d, tm, n_tab_blocks, g_pre, w_in, w_vt, g_q, w_q, g_kv, w_kv, w_kvt,
               ct, st, posf, name):
    rows = x2d.shape[0]
    grid = (rows // tm,)
    row = lambda i: (i, 0)
    col = lambda i: (0, i)
    const = lambda i: (0, 0)
    tab = lambda i: (i % n_tab_blocks, 0)
    row_out = lambda w: (pl.BlockSpec((tm, w), row), jax.ShapeDtypeStruct((rows, w), BF16))
    col_out = lambda r: (pl.BlockSpec((r, tm), col), jax.ShapeDtypeStruct((r, rows), BF16))
    outs = [row_out(DA_WIDTH), row_out(DA_HEADS * KEY_W), col_out(DA_HEADS * VT_ROWS),
            row_out(MLA_HEADS * KEY_W), row_out(MLA_HEADS * KEY_W), col_out(MLA_HEADS * VT_ROWS)]
    return pl.pallas_call(
        _proj_kernel,
        grid=grid,
        in_specs=[
            pl.BlockSpec((tm, D_MODEL), row),
            pl.BlockSpec((1, D_MODEL), const),
            pl.BlockSpec((D_MODEL, C_END), const),
            pl.BlockSpec((DA_WIDTH, D_MODEL), const),
            pl.BlockSpec((1, MLA_Q_RANK), const),
            pl.BlockSpec((MLA_Q_RANK, MLA_HEADS * Q_UP_W), const),
            pl.BlockSpec((1, MLA_KV_RANK), const),
            pl.BlockSpec((MLA_KV_RANK, MLA_HEADS * MLA_NOPE), const),
            pl.BlockSpec((MLA_WIDTH, MLA_KV_RANK), const),
            pl.BlockSpec((tm, LANES), tab),
            pl.BlockSpec((tm, LANES), tab),
            pl.BlockSpec((tm, LANES), tab),
        ],
        out_specs=[o[0] for o in outs],
        out_shape=[o[1] for o in outs],
        compiler_params=pltpu.CompilerParams(
            dimension_semantics=("arbitrary",), vmem_limit_bytes=VMEM_LIMIT),
        name=name,
    )(x2d, g_pre, w_in, w_vt, g_q, w_q, g_kv, w_kv, w_kvt, ct, st, posf)


def _softmax_step(s_t, vt, m_ref, acc_ref):
    m_prev = m_ref[...]
    m_new = jnp.maximum(m_prev, jnp.max(s_t, axis=0, keepdims=True))
    p_t = jnp.exp2((s_t - m_new).astype(BF16))
    acc_ref[...] = jnp.exp2(m_prev - m_new) * acc_ref[...] + _dot(vt, p_t)
    m_ref[...] = m_new


def _diag_iotas(tk, tq):
    ik = lax.broadcasted_iota(jnp.int32, (tk, tq), 0)
    iq = lax.broadcasted_iota(jnp.int32, (tk, tq), 1)
    return ik, iq


def _normalised(acc_ref, dv):
    return acc_ref[0:dv, :] / acc_ref[dv:dv + 1, :]


class _Chain(NamedTuple):
    load_q: Callable[[Any], jax.Array]
    kcols: Any
    vrows: Any
    diag_fn: Callable[[jax.Array, int], jax.Array]


def _attend(qi, chains, kmeta_ref, vtmeta_ref, k_ref, vt_ref, s_bufs, smeta_ref, m_ref, acc_ref,
            on_chain_done):
    tk, tq = ATT_TK, ATT_TQ
    half = tq // 2
    buf_a, buf_b = s_bufs
    halves = ((0, half), (half, tk))

    m_ref[...] = jnp.full(m_ref.shape, NEG, F32)
    acc_ref[...] = jnp.zeros(acc_ref.shape, F32)

    def key_block(j, size=tk):
        return pl.ds(pl.multiple_of(j * tk, tk), size)

    def scores_into(buf, j):
        for c, ch in enumerate(chains):
            buf[c] = _dot_nt(k_ref[0, key_block(j), ch.kcols], ch.load_q(slice(None)))

    def consume(buf, j):
        for c, ch in enumerate(chains):
            _softmax_step(buf[c], vt_ref[ch.vrows, key_block(j)], m_ref.at[c], acc_ref.at[c])

    def diagonal_scores_into(buf):
        for c, ch in enumerate(chains):
            kmeta = kmeta_ref[:, ch.kcols]
            for q0, n_keys in halves:
                keys = jnp.concatenate([k_ref[0, key_block(qi, n_keys), ch.kcols], kmeta], axis=0)
                s_t = _dot_nt(keys, ch.load_q(slice(q0, q0 + half)))
                buf[c, 0:n_keys, q0:q0 + half] = s_t[0:n_keys]
                smeta_ref[c, :, q0:q0 + half] = s_t[n_keys:n_keys + META_PAD]

    def finish(buf):
        meta_valid = lax.broadcasted_iota(jnp.int32, (META_PAD, half), 0) < N_META
        for c, ch in enumerate(chains):
            for q0, n_keys in halves:
                lanes = pl.ds(q0, half)
                s_t = jnp.concatenate(
                    [ch.diag_fn(buf[c, 0:n_keys, q0:q0 + half], q0),
                     jnp.where(meta_valid, smeta_ref[c, :, q0:q0 + half], NEG)], axis=0)
                vt = jnp.concatenate(
                    [vt_ref[ch.vrows, key_block(qi, n_keys)], vtmeta_ref[ch.vrows, :]], axis=1)
                _softmax_step(s_t, vt, m_ref.at[c, :, lanes], acc_ref.at[c, :, lanes])
            on_chain_done(c)

    @pl.when(qi == 0)
    def _():
        diagonal_scores_into(buf_a)
        finish(buf_a)

    @pl.when(qi > 0)
    def _():
        scores_into(buf_a, 0)

        def pair(t, carry):
            j = 2 * t
            scores_into(buf_b, j + 1)
            consume(buf_a, j)
            scores_into(buf_a, j + 2)
            consume(buf_b, j + 1)
            return carry

        lax.fori_loop(0, (qi - 1) // 2, pair, 0)

        @pl.when(qi % 2 == 1)
        def _():
            diagonal_scores_into(buf_b)
            consume(buf_a, qi - 1)
            finish(buf_b)

        @pl.when(qi % 2 == 0)
        def _():
            scores_into(buf_b, qi - 1)
            consume(buf_a, qi - 2)
            diagonal_scores_into(buf_a)
            consume(buf_b, qi - 1)
            finish(buf_a)


def _attention_scratch(n_chains, tq):
    stat = pltpu.VMEM((n_chains, 1, tq), F32)
    scores = pltpu.VMEM((n_chains, ATT_TK, tq), F32)
    return [stat, pltpu.VMEM((n_chains, VT_ROWS, tq), F32), scores, scores,
            pltpu.VMEM((n_chains, META_PAD, tq), F32)]


def _da_kernel(q_ref, k_ref, vt_ref, kmeta_ref, vtmeta_ref, lam_ref, gsub_ref, y_ref,
               qx_ref, m_ref, acc_ref, sa_ref, sb_ref, smeta_ref):
    qi = pl.program_id(1)
    tq, tk = ATT_TQ, ATT_TK
    lane = lax.broadcasted_iota(jnp.int32, (tq, LANES), 1)
    lp = lam_ref[...]
    lam = (jnp.exp(jnp.sum(lp[0:1] * lp[1:2], axis=-1, keepdims=True))
           - jnp.exp(jnp.sum(lp[2:3] * lp[3:4], axis=-1, keepdims=True)) + LAMBDA_INIT)

    for h0 in range(0, DA_HEADS, DA_HEAD_GROUP):
        chains = []
        for g in range(DA_HEAD_GROUP):
            h = h0 + g
            c, pieces = _alibi_pieces(h)
            q = q_ref[0, :, LANES * h:LANES * (h + 1)]
            pf = jnp.zeros((tq, LANES), F32)
            for i, piece in enumerate(pieces):
                pf = jnp.where((lane == 2 * i) | (lane == 2 * i + 1), piece, pf)
            pf = pf.astype(BF16)
            zero = jnp.zeros_like(q)
            qx_ref[2 * g, :, 0:LANES] = jnp.where(lane < DA_HEAD_DIM, q, zero)
            qx_ref[2 * g + 1, :, 0:LANES] = jnp.where(lane >= DA_HEAD_DIM, q, zero)
            qx_ref[2 * g, :, LANES:KEY_W] = pf
            qx_ref[2 * g + 1, :, LANES:KEY_W] = pf

            def diag_fn(s_t, q0, c=c):
                ik, iq = _diag_iotas(*s_t.shape)
                iq = iq + q0
                visible = (ik // CHUNK) <= (iq // CHUNK)
                corr = (2.0 * c) * jnp.minimum(iq - ik, 0).astype(F32)
                return jnp.where(visible, s_t + corr, NEG)

            for comp in range(2):
                chains.append(_Chain(
                    load_q=functools.partial(lambda i, rows: qx_ref[i, rows, :], 2 * g + comp),
                    kcols=pl.ds(KEY_W * h, KEY_W), vrows=pl.ds(VT_ROWS * h, VT_ROWS),
                    diag_fn=diag_fn))

        def finish_head(c, h0=h0):
            if c % 2 == 0:
                return
            h = h0 + c // 2
            o = (_normalised(acc_ref.at[c - 1], DA_V_DIM)
                 - lam * _normalised(acc_ref.at[c], DA_V_DIM))
            o = o * lax.rsqrt(jnp.mean(o * o, axis=0, keepdims=True) + EPS)
            o = o * gsub_ref[...] * (1.0 - LAMBDA_INIT)
            y_ref[0, :, LANES * h:LANES * (h + 1)] = o.T.astype(BF16)

        _attend(qi, chains, kmeta_ref, vtmeta_ref, k_ref, vt_ref, (sa_ref, sb_ref), smeta_ref,
                m_ref, acc_ref,
                finish_head)


def _da_call(q, k, vt, kmeta, vtmeta, lam_params, g_sub_col):
    nb, s, _ = q.shape
    tq = ATT_TQ
    n_chains = 2 * DA_HEAD_GROUP
    return pl.pallas_call(
        _da_kernel,
        grid=(nb, s // tq),
        in_specs=[
            pl.BlockSpec((1, tq, DA_WIDTH), lambda b, i: (b, i, 0)),
            pl.BlockSpec((1, s, DA_HEADS * KEY_W), lambda b, i: (b, 0, 0)),
            pl.BlockSpec((DA_HEADS * VT_ROWS, s), lambda b, i: (0, b)),
            pl.BlockSpec((META_PAD, DA_HEADS * KEY_W), lambda b, i: (0, 0)),
            pl.BlockSpec((DA_HEADS * VT_ROWS, META_PAD), lambda b, i: (0, 0)),
            pl.BlockSpec((8, LANES), lambda b, i: (0, 0)),
            pl.BlockSpec((DA_V_DIM, 1), lambda b, i: (0, 0)),
        ],
        out_specs=pl.BlockSpec((1, tq, DA_WIDTH), lambda b, i: (b, i, 0)),
        out_shape=jax.ShapeDtypeStruct((nb, s, DA_WIDTH), BF16),
        scratch_shapes=[pltpu.VMEM((n_chains, tq, KEY_W), BF16)] + _attention_scratch(n_chains, tq),
        compiler_params=pltpu.CompilerParams(
            dimension_semantics=("arbitrary", "arbitrary"), vmem_limit_bytes=VMEM_LIMIT),
        name="diff_attention",
    )(q, k, vt, kmeta, vtmeta, lam_params, g_sub_col)


def _mla_kernel(q_ref, k_ref, vt_ref, kmeta_ref, vtmeta_ref, y_ref, m_ref, acc_ref,
                sa_ref, sb_ref, smeta_ref):
    qi = pl.program_id(1)
    tq, tk = ATT_TQ, ATT_TK

    def diag_fn(s_t, q0):
        ik, iq = _diag_iotas(*s_t.shape)
        return jnp.where((ik // CHUNK) <= ((iq + q0) // CHUNK), s_t, NEG)

    for h0 in range(0, MLA_HEADS, MLA_HEAD_GROUP):
        heads = range(h0, h0 + MLA_HEAD_GROUP)
        chains = [
            _Chain(load_q=functools.partial(
                       lambda h, rows: q_ref[0, rows, KEY_W * h:KEY_W * (h + 1)], h),
                   kcols=pl.ds(KEY_W * h, KEY_W), vrows=pl.ds(VT_ROWS * h, VT_ROWS),
                   diag_fn=diag_fn)
            for h in heads]
        def finish_head(c, h0=h0):
            h = h0 + c
            y_ref[0, :, LANES * h:LANES * (h + 1)] = (
                _normalised(acc_ref.at[c], MLA_V).T.astype(BF16))

        _attend(qi, chains, kmeta_ref, vtmeta_ref, k_ref, vt_ref, (sa_ref, sb_ref), smeta_ref,
                m_ref, acc_ref,
                finish_head)


def _mla_call(q, k, vt, kmeta, vtmeta):
    nb, s, _ = q.shape
    tq = ATT_TQ
    return pl.pallas_call(
        _mla_kernel,
        grid=(nb, s // tq),
        in_specs=[
            pl.BlockSpec((1, tq, MLA_HEADS * KEY_W), lambda b, i: (b, i, 0)),
            pl.BlockSpec((1, s, MLA_HEADS * KEY_W), lambda b, i: (b, 0, 0)),
            pl.BlockSpec((MLA_HEADS * VT_ROWS, s), lambda b, i: (0, b)),
            pl.BlockSpec((META_PAD, MLA_HEADS * KEY_W), lambda b, i: (0, 0)),
            pl.BlockSpec((MLA_HEADS * VT_ROWS, META_PAD), lambda b, i: (0, 0)),
        ],
        out_specs=pl.BlockSpec((1, tq, MLA_WIDTH), lambda b, i: (b, i, 0)),
        out_shape=jax.ShapeDtypeStruct((nb, s, MLA_WIDTH), BF16),
        scratch_shapes=_attention_scratch(MLA_HEAD_GROUP, tq),
        compiler_params=pltpu.CompilerParams(
            dimension_semantics=("arbitrary", "arbitrary"), vmem_limit_bytes=VMEM_LIMIT),
        name="mla_attention",
    )(q, k, vt, kmeta, vtmeta)


def _out_kernel(x_ref, yda_ref, ymla_ref, wo_ref, gpost_ref, gpre_ref, w1_ref, w2_ref,
                gmlp_ref, o_ref):
    tm = x_ref.shape[0]
    groups = [slice(r, r + tm // OUT_ROW_GROUPS) for r in range(0, tm, tm // OUT_ROW_GROUPS)]
    mix = [_dot(yda_ref[r, :], wo_ref[0:DA_WIDTH, :])
           + _dot(ymla_ref[r, :], wo_ref[DA_WIDTH:DA_WIDTH + MLA_WIDTH, :]) for r in groups]
    h1 = [x_ref[r, :] + _rms_scale(m) * gpost_ref[...] for r, m in zip(groups, mix)]
    u = [(_rms_scale(h) * gpre_ref[...]).astype(BF16) for h in h1]
    for r, h, ug in zip(groups, h1, u):
        f = None
        for c in range(D_FF // FF_CHUNK):
            cols = slice(FF_CHUNK * c, FF_CHUNK * (c + 1))
            hid = jnp.square(jnp.maximum(_dot(ug, w1_ref[:, cols]), 0.0)).astype(BF16)
            part = _dot(hid, w2_ref[cols, :])
            f = part if f is None else f + part
        o_ref[r, :] = h + _rms_scale(f) * gmlp_ref[...]


def _out_call(x2d, yda, ymla, w_o, g_post, g_pre, w1, w2, g_mlp):
    rows = x2d.shape[0]
    tm = OUT_TM
    row = lambda i: (i, 0)
    const = lambda i: (0, 0)
    single = pl.Buffered(1)
    return pl.pallas_call(
        _out_kernel,
        grid=(rows // tm,),
        in_specs=[
            pl.BlockSpec((tm, D_MODEL), row),
            pl.BlockSpec((tm, DA_WIDTH), row),
            pl.BlockSpec((tm, MLA_WIDTH), row),
            pl.BlockSpec((D_MODEL, D_MODEL), const, pipeline_mode=single),
            pl.BlockSpec((1, D_MODEL), const),
            pl.BlockSpec((1, D_MODEL), const),
            pl.BlockSpec((D_MODEL, D_FF), const, pipeline_mode=single),
            pl.BlockSpec((D_FF, D_MODEL), const, pipeline_mode=single),
            pl.BlockSpec((1, D_MODEL), const),
        ],
        out_specs=pl.BlockSpec((tm, D_MODEL), row),
        out_shape=jax.ShapeDtypeStruct((rows, D_MODEL), F32),
        compiler_params=pltpu.CompilerParams(
            dimension_semantics=("arbitrary",), vmem_limit_bytes=VMEM_LIMIT),
        name="out_mlp",
    )(x2d, yda, ymla, w_o, g_post, g_pre, w1, w2, g_mlp)


def _position_tables(n_pos):
    inv_freq = 1.0 / (ROPE_THETA ** (np.arange(0, MLA_ROPE, 2, dtype=np.float64) / MLA_ROPE))
    ang = np.arange(n_pos, dtype=np.float64)[:, None] * inv_freq[None, :]
    cos, sin = np.cos(ang), np.sin(ang)
    zeros = np.zeros((n_pos, LANES - MLA_ROPE))
    ct = np.concatenate([cos, cos, zeros], axis=1).astype(np.float32)
    st = np.concatenate([-sin, sin, zeros], axis=1).astype(np.float32)
    ipos = np.arange(n_pos)
    hi = ((ipos // CHUNK) * CHUNK)[:, None]
    lo = (ipos % CHUNK)[:, None]
    posf = np.concatenate([hi, lo] * ALIBI_PIECES
                          + [np.zeros((n_pos, LANES - 2 * ALIBI_PIECES), np.int64)], axis=1)
    return ct, st, posf.astype(ml_dtypes.bfloat16)


def _swap_halves(w):
    half = w.shape[-1] // 2
    return jnp.concatenate([w[..., half:], w[..., :half]], axis=-1)


def kernel(x, meta_tokens, g_attn_pre, w_in, da_lambda_q1, da_lambda_k1, da_lambda_q2,
           da_lambda_k2, g_da_sub, g_mla_q, w_mla_q_up, g_mla_kv, w_mla_kv_up, w_o,
           g_attn_post, g_mlp_pre, w_ff1, w_ff2, g_mlp_post):
    nb, seq, d = x.shape
    assert d == D_MODEL and w_in.shape[0] == 1, "single-layer block only"
    assert seq % ATT_TQ == 0 and seq % PROJ_TM == 0 and ATT_TQ == ATT_TK

    w_in0 = w_in[0].astype(BF16)
    n_da = 2 * DA_WIDTH
    v_cols = slice(n_da, n_da + DA_WIDTH)
    rest = w_in0[:, n_da + DA_WIDTH:]
    kr = rest[:, MLA_Q_RANK + MLA_KV_RANK:]
    zpad = jnp.zeros((D_MODEL, LANES - MLA_ROPE), BF16)
    w_in_ext = jnp.concatenate(
        [w_in0[:, :n_da], rest[:, :MLA_Q_RANK + MLA_KV_RANK], kr, zpad, _swap_halves(kr), zpad],
        axis=1)
    w_vt = w_in0[:, v_cols].T
    wq = w_mla_q_up[0].astype(BF16).reshape(MLA_Q_RANK, MLA_HEADS, MLA_NOPE + MLA_ROPE)
    wq_rope = wq[..., MLA_NOPE:]
    zq = jnp.zeros((MLA_Q_RANK, MLA_HEADS, LANES - MLA_ROPE), BF16)
    wq_ext = jnp.concatenate(
        [wq[..., :MLA_NOPE], wq_rope, zq, _swap_halves(wq_rope), zq], axis=-1
    ).reshape(MLA_Q_RANK, MLA_HEADS * Q_UP_W)
    wkv = w_mla_kv_up[0].astype(BF16).reshape(MLA_KV_RANK, MLA_HEADS, MLA_NOPE + MLA_V)
    wkv_k = wkv[..., :MLA_NOPE].reshape(MLA_KV_RANK, MLA_HEADS * MLA_NOPE)
    wkv_vt = wkv[..., MLA_NOPE:].reshape(MLA_KV_RANK, MLA_WIDTH).T

    ct, st, posf = _position_tables(N_META + seq)
    lam_rows = jnp.concatenate(
        [da_lambda_q1, da_lambda_k1, da_lambda_q2, da_lambda_k2], axis=0).astype(F32)
    lam_params = jnp.pad(lam_rows, ((0, 8 - lam_rows.shape[0]), (0, LANES - DA_HEAD_DIM)))

    proj = functools.partial(
        _proj_call, g_pre=g_attn_pre, w_in=w_in_ext, w_vt=w_vt, g_q=g_mla_q, w_q=wq_ext,
        g_kv=g_mla_kv, w_kv=wkv_k, w_kvt=wkv_vt)
    x2d = x.reshape(nb * seq, D_MODEL)
    qda, kda, vtda, qm, km, vtm = proj(
        x2d, PROJ_TM, seq // PROJ_TM, ct=ct[N_META:], st=st[N_META:], posf=posf[N_META:],
        name="proj_tokens")
    meta_rows = jnp.pad(meta_tokens.astype(F32), ((0, META_PAD - N_META), (0, 0)))
    _, kda_meta, vtda_meta, _, km_meta, vtm_meta = proj(
        meta_rows, META_PAD, 1, ct=ct[:META_PAD], st=st[:META_PAD], posf=posf[:META_PAD],
        name="proj_meta")

    def per_batch(a):
        return a.reshape(nb, seq, a.shape[-1])

    yda = _da_call(per_batch(qda), per_batch(kda), vtda, kda_meta, vtda_meta,
                   lam_params, g_da_sub.reshape(DA_V_DIM, 1))
    ymla = _mla_call(per_batch(qm), per_batch(km), vtm, km_meta, vtm_meta)

    out = _out_call(x2d, yda.reshape(nb * seq, DA_WIDTH), ymla.reshape(nb * seq, MLA_WIDTH),
                    w_o[0].astype(BF16), g_attn_post, g_mlp_pre,
                    w_ff1[0].astype(BF16), w_ff2[0].astype(BF16), g_mlp_post)
    return out.reshape(nb, seq, D_MODEL)
```

```python
import functools
import math
from typing import Any, Callable, NamedTuple

import jax
import jax.numpy as jnp
import ml_dtypes
import numpy as np
from jax import lax
from jax.experimental import pallas as pl
from jax.experimental.pallas import tpu as pltpu

F32 = jnp.float32
BF16 = jnp.bfloat16

D_MODEL = 1024
N_META = 16
CHUNK = 64
EPS = 1e-6
NEG = -1e30
ROPE_THETA = 10000.0
LOG2E = math.log2(math.e)

DA_HEADS = 4
DA_HEAD_DIM = 64
DA_V_DIM = 128
DA_WIDTH = DA_HEADS * DA_V_DIM
MLA_HEADS = 4
MLA_NOPE = 128
MLA_ROPE = 64
MLA_V = 128
MLA_WIDTH = MLA_HEADS * MLA_V
MLA_Q_RANK = 256
MLA_KV_RANK = 128
D_FF = 4 * D_MODEL
LAMBDA_INIT = 0.8 - 0.6 * math.exp(-0.3 * 0)

LANES = 128
BF16_SUBLANES = 16
KEY_W = 2 * LANES
VT_ROWS = DA_V_DIM + BF16_SUBLANES
VMEM_LIMIT = 58 * 1024 * 1024

PROJ_TM = 1024
PROJ_ROW_GROUPS = 4
ATT_TQ = 512
ATT_TK = 512
OUT_TM = 1024
FF_CHUNK = 1024
OUT_ROW_GROUPS = 4
META_PAD = 128
DA_HEAD_GROUP = 4
MLA_HEAD_GROUP = 4

C_QDA, C_KDA, C_CQ, C_CKV, C_KR, C_KRS, C_END = (0, 512, 1024, 1280, 1408, 1536, 1664)
Q_UP_W = 3 * LANES
ALIBI_PIECES = 3


def _rms_scale(x):
    return x * lax.rsqrt(jnp.mean(x * x, axis=-1, keepdims=True) + EPS)


def _dot(a, b):
    return jnp.dot(a, b, preferred_element_type=F32)


def _dot_nt(a, b):
    return lax.dot_general(a, b, (((1,), (1,)), ((), ())), preferred_element_type=F32)


def _alibi_pieces(h):
    c = np.float32(2.0 ** (-8.0 * (h + 1) / DA_HEADS)) * np.float32(LOG2E)
    pieces, rest = [], c
    for _ in range(ALIBI_PIECES):
        piece = np.float32(ml_dtypes.bfloat16(rest))
        pieces.append(float(piece))
        rest = np.float32(rest - piece)
    assert rest == 0.0
    return float(c), pieces


def _proj_kernel(x_ref, g_ref, win_ref, wvt_ref, gq_ref, wq_ref, gkv_ref, wkv_ref, wkvt_ref,
                 ct_ref, st_ref, posf_ref,
                 qda_ref, kda_ref, vtda_ref, qm_ref, km_ref, vtm_ref):
    tm = x_ref.shape[0]
    n_groups = PROJ_ROW_GROUPS if tm % (PROJ_ROW_GROUPS * LANES) == 0 else 1
    groups = [slice(r, r + tm // n_groups) for r in range(0, tm, tm // n_groups)]
    ones = jnp.ones((BF16_SUBLANES, tm // n_groups), BF16)
    scale = (MLA_NOPE + MLA_ROPE) ** -0.5 * LOG2E

    u = [(_rms_scale(x_ref[r, :]) * g_ref[...]).astype(BF16) for r in groups]
    p = [_dot(ug, win_ref[...]) for ug in u]
    vt = [_dot_nt(wvt_ref[...], ug) for ug in u]

    for r, pg, vtg in zip(groups, p, vt):
        qda_ref[r, :] = (pg[:, C_QDA:C_KDA] * (DA_HEAD_DIM ** -0.5 * LOG2E)).astype(BF16)
        posf = posf_ref[r, :]
        for h in range(DA_HEADS):
            kda_ref[r, KEY_W * h:KEY_W * h + LANES] = (
                pg[:, C_KDA + LANES * h:C_KDA + LANES * (h + 1)].astype(BF16))
            kda_ref[r, KEY_W * h + LANES:KEY_W * (h + 1)] = posf
            vtda_ref[VT_ROWS * h:VT_ROWS * h + DA_V_DIM, r] = (
                vtg[DA_V_DIM * h:DA_V_DIM * (h + 1), :].astype(BF16))
            vtda_ref[VT_ROWS * h + DA_V_DIM:VT_ROWS * (h + 1), r] = ones

    cq = [(_rms_scale(pg[:, C_CQ:C_CKV]) * gq_ref[...]).astype(BF16) for pg in p]
    ckv = [(_rms_scale(pg[:, C_CKV:C_KR]) * gkv_ref[...]).astype(BF16) for pg in p]
    qu = [_dot(c, wq_ref[...]) for c in cq]
    kn = [_dot(c, wkv_ref[...]) for c in ckv]
    vtm = [_dot_nt(wkvt_ref[...], c) for c in ckv]
    for r, pg, qug, kng, vtmg in zip(groups, p, qu, kn, vtm):
        ct = ct_ref[r, :]
        st = st_ref[r, :]
        k_rope = (pg[:, C_KR:C_KRS] * ct + pg[:, C_KRS:C_END] * st).astype(BF16)
        for h in range(MLA_HEADS):
            b = Q_UP_W * h
            qm_ref[r, KEY_W * h:KEY_W * h + LANES] = (qug[:, b:b + LANES] * scale).astype(BF16)
            q_rope = (qug[:, b + LANES:b + 2 * LANES] * ct
                      + qug[:, b + 2 * LANES:b + 3 * LANES] * st)
            qm_ref[r, KEY_W * h + LANES:KEY_W * (h + 1)] = (q_rope * scale).astype(BF16)
            km_ref[r, KEY_W * h:KEY_W * h + LANES] = kng[:, LANES * h:LANES * (h + 1)].astype(BF16)
            km_ref[r, KEY_W * h + LANES:KEY_W * (h + 1)] = k_rope
            vtm_ref[VT_ROWS * h:VT_ROWS * h + MLA_V, r] = (
                vtmg[MLA_V * h:MLA_V * (h + 1), :].astype(BF16))
            vtm_ref[VT_ROWS * h + MLA_V:VT_ROWS * (h + 1), r] = ones


def _proj_call(x2d, tm, n_tab_blocks, g_pre, w_in, w_vt, g_q, w_q, g_kv, w_kv, w_kvt,
               ct, st, posf, name):
    rows = x2d.shape[0]
    grid = (rows // tm,)
    row = lambda i: (i, 0)
    col = lambda i: (0, i)
    const = lambda i: (0, 0)
    tab = lambda i: (i % n_tab_blocks, 0)
    row_out = lambda w: (pl.BlockSpec((tm, w), row), jax.ShapeDtypeStruct((rows, w), BF16))
    col_out = lambda r: (pl.BlockSpec((r, tm), col), jax.ShapeDtypeStruct((r, rows), BF16))
    outs = [row_out(DA_WIDTH), row_out(DA_HEADS * KEY_W), col_out(DA_HEADS * VT_ROWS),
            row_out(MLA_HEADS * KEY_W), row_out(MLA_HEADS * KEY_W), col_out(MLA_HEADS * VT_ROWS)]
    return pl.pallas_call(
        _proj_kernel,
        grid=grid,
        in_specs=[
            pl.BlockSpec((tm, D_MODEL), row),
            pl.BlockSpec((1, D_MODEL), const),
            pl.BlockSpec((D_MODEL, C_END), const),
            pl.BlockSpec((DA_WIDTH, D_MODEL), const),
            pl.BlockSpec((1, MLA_Q_RANK), const),
            pl.BlockSpec((MLA_Q_RANK, MLA_HEADS * Q_UP_W), const),
            pl.BlockSpec((1, MLA_KV_RANK), const),
            pl.BlockSpec((MLA_KV_RANK, MLA_HEADS * MLA_NOPE), const),
            pl.BlockSpec((MLA_WIDTH, MLA_KV_RANK), const),
            pl.BlockSpec((tm, LANES), tab),
            pl.BlockSpec((tm, LANES), tab),
            pl.BlockSpec((tm, LANES), tab),
        ],
        out_specs=[o[0] for o in outs],
        out_shape=[o[1] for o in outs],
        compiler_params=pltpu.CompilerParams(
            dimension_semantics=("arbitrary",), vmem_limit_bytes=VMEM_LIMIT),
        name=name,
    )(x2d, g_pre, w_in, w_vt, g_q, w_q, g_kv, w_kv, w_kvt, ct, st, posf)


def _softmax_step(s_t, vt, m_ref, acc_ref):
    m_prev = m_ref[...]
    m_new = jnp.maximum(m_prev, jnp.max(s_t, axis=0, keepdims=True))
    p_t = jnp.exp2(s_t - m_new).astype(BF16)
    acc_ref[...] = jnp.exp2(m_prev - m_new) * acc_ref[...] + _dot(vt, p_t)
    m_ref[...] = m_new


def _diag_iotas(tk, tq):
    ik = lax.broadcasted_iota(jnp.int32, (tk, tq), 0)
    iq = lax.broadcasted_iota(jnp.int32, (tk, tq), 1)
    return ik, iq


def _normalised(acc_ref, dv):
    return acc_ref[0:dv, :] / acc_ref[dv:dv + 1, :]


class _Chain(NamedTuple):
    load_q: Callable[[Any], jax.Array]
    kcols: Any
    vrows: Any
    diag_fn: Callable[[jax.Array, int], jax.Array]


def _attend(qi, chains, kmeta_ref, vtmeta_ref, k_ref, vt_ref, s_bufs, smeta_ref, m_ref, acc_ref,
            on_chain_done):
    tk, tq = ATT_TK, ATT_TQ
    half = tq // 2
    buf_a, buf_b = s_bufs
    halves = ((0, half), (half, tk))

    m_ref[...] = jnp.full(m_ref.shape, NEG, F32)
    acc_ref[...] = jnp.zeros(acc_ref.shape, F32)

    def key_block(j, size=tk):
        return pl.ds(pl.multiple_of(j * tk, tk), size)

    def scores_into(buf, j):
        for c, ch in enumerate(chains):
            buf[c] = _dot_nt(k_ref[0, key_block(j), ch.kcols], ch.load_q(slice(None)))

    def consume(buf, j):
        for c, ch in enumerate(chains):
            _softmax_step(buf[c], vt_ref[ch.vrows, key_block(j)], m_ref.at[c], acc_ref.at[c])

    def diagonal_scores_into(buf):
        for c, ch in enumerate(chains):
            kmeta = kmeta_ref[:, ch.kcols]
            for q0, n_keys in halves:
                keys = jnp.concatenate([k_ref[0, key_block(qi, n_keys), ch.kcols], kmeta], axis=0)
                s_t = _dot_nt(keys, ch.load_q(slice(q0, q0 + half)))
                buf[c, 0:n_keys, q0:q0 + half] = s_t[0:n_keys]
                smeta_ref[c, :, q0:q0 + half] = s_t[n_keys:n_keys + META_PAD]

    def finish(buf):
        meta_valid = lax.broadcasted_iota(jnp.int32, (META_PAD, half), 0) < N_META
        for c, ch in enumerate(chains):
            for q0, n_keys in halves:
                lanes = pl.ds(q0, half)
                s_t = jnp.concatenate(
                    [ch.diag_fn(buf[c, 0:n_keys, q0:q0 + half], q0),
                     jnp.where(meta_valid, smeta_ref[c, :, q0:q0 + half], NEG)], axis=0)
                vt = jnp.concatenate(
                    [vt_ref[ch.vrows, key_block(qi, n_keys)], vtmeta_ref[ch.vrows, :]], axis=1)
                _softmax_step(s_t, vt, m_ref.at[c, :, lanes], acc_ref.at[c, :, lanes])
            on_chain_done(c)

    @pl.when(qi == 0)
    def _():
        diagonal_scores_into(buf_a)
        finish(buf_a)

    @pl.when(qi > 0)
    def _():
        scores_into(buf_a, 0)

        def pair(t, carry):
            j = 2 * t
            scores_into(buf_b, j + 1)
            consume(buf_a, j)
            scores_into(buf_a, j + 2)
            consume(buf_b, j + 1)
            return carry

        lax.fori_loop(0, (qi - 1) // 2, pair, 0)

        @pl.when(qi % 2 == 1)
        def _():
            diagonal_scores_into(buf_b)
            consume(buf_a, qi - 1)
            finish(buf_b)

        @pl.when(qi % 2 == 0)
        def _():
            scores_into(buf_b, qi - 1)
            consume(buf_a, qi - 2)
            diagonal_scores_into(buf_a)
            consume(buf_b, qi - 1)
            finish(buf_a)


def _attention_scratch(n_chains, tq):
    stat = pltpu.VMEM((n_chains, 1, tq), F32)
    scores = pltpu.VMEM((n_chains, ATT_TK, tq), F32)
    return [stat, pltpu.VMEM((n_chains, VT_ROWS, tq), F32), scores, scores,
            pltpu.VMEM((n_chains, META_PAD, tq), F32)]


def _da_kernel(q_ref, k_ref, vt_ref, kmeta_ref, vtmeta_ref, lam_ref, gsub_ref, y_ref,
               qx_ref, m_ref, acc_ref, sa_ref, sb_ref, smeta_ref):
    qi = pl.program_id(1)
    tq, tk = ATT_TQ, ATT_TK
    lane = lax.broadcasted_iota(jnp.int32, (tq, LANES), 1)
    lp = lam_ref[...]
    lam = (jnp.exp(jnp.sum(lp[0:1] * lp[1:2], axis=-1, keepdims=True))
           - jnp.exp(jnp.sum(lp[2:3] * lp[3:4], axis=-1, keepdims=True)) + LAMBDA_INIT)

    for h0 in range(0, DA_HEADS, DA_HEAD_GROUP):
        chains = []
        for g in range(DA_HEAD_GROUP):
            h = h0 + g
            c, pieces = _alibi_pieces(h)
            q = q_ref[0, :, LANES * h:LANES * (h + 1)]
            pf = jnp.zeros((tq, LANES), F32)
            for i, piece in enumerate(pieces):
                pf = jnp.where((lane == 2 * i) | (lane == 2 * i + 1), piece, pf)
            pf = pf.astype(BF16)
            zero = jnp.zeros_like(q)
            qx_ref[2 * g, :, 0:LANES] = jnp.where(lane < DA_HEAD_DIM, q, zero)
            qx_ref[2 * g + 1, :, 0:LANES] = jnp.where(lane >= DA_HEAD_DIM, q, zero)
            qx_ref[2 * g, :, LANES:KEY_W] = pf
            qx_ref[2 * g + 1, :, LANES:KEY_W] = pf

            def diag_fn(s_t, q0, c=c):
                ik, iq = _diag_iotas(*s_t.shape)
                iq = iq + q0
                visible = (ik // CHUNK) <= (iq // CHUNK)
                corr = (2.0 * c) * jnp.minimum(iq - ik, 0).astype(F32)
                return jnp.where(visible, s_t + corr, NEG)

            for comp in range(2):
                chains.append(_Chain(
                    load_q=functools.partial(lambda i, rows: qx_ref[i, rows, :], 2 * g + comp),
                    kcols=pl.ds(KEY_W * h, KEY_W), vrows=pl.ds(VT_ROWS * h, VT_ROWS),
                    diag_fn=diag_fn))

        def finish_head(c, h0=h0):
            if c % 2 == 0:
                return
            h = h0 + c // 2
            o = (_normalised(acc_ref.at[c - 1], DA_V_DIM)
                 - lam * _normalised(acc_ref.at[c], DA_V_DIM))
            o = o * lax.rsqrt(jnp.mean(o * o, axis=0, keepdims=True) + EPS)
            o = o * gsub_ref[...] * (1.0 - LAMBDA_INIT)
            y_ref[0, :, LANES * h:LANES * (h + 1)] = o.T.astype(BF16)

        _attend(qi, chains, kmeta_ref, vtmeta_ref, k_ref, vt_ref, (sa_ref, sb_ref), smeta_ref,
                m_ref, acc_ref,
                finish_head)


def _da_call(q, k, vt, kmeta, vtmeta, lam_params, g_sub_col):
    nb, s, _ = q.shape
    tq = ATT_TQ
    n_chains = 2 * DA_HEAD_GROUP
    return pl.pallas_call(
        _da_kernel,
        grid=(nb, s // tq),
        in_specs=[
            pl.BlockSpec((1, tq, DA_WIDTH), lambda b, i: (b, i, 0)),
            pl.BlockSpec((1, s, DA_HEADS * KEY_W), lambda b, i: (b, 0, 0)),
            pl.BlockSpec((DA_HEADS * VT_ROWS, s), lambda b, i: (0, b)),
            pl.BlockSpec((META_PAD, DA_HEADS * KEY_W), lambda b, i: (0, 0)),
            pl.BlockSpec((DA_HEADS * VT_ROWS, META_PAD), lambda b, i: (0, 0)),
            pl.BlockSpec((8, LANES), lambda b, i: (0, 0)),
            pl.BlockSpec((DA_V_DIM, 1), lambda b, i: (0, 0)),
        ],
        out_specs=pl.BlockSpec((1, tq, DA_WIDTH), lambda b, i: (b, i, 0)),
        out_shape=jax.ShapeDtypeStruct((nb, s, DA_WIDTH), BF16),
        scratch_shapes=[pltpu.VMEM((n_chains, tq, KEY_W), BF16)] + _attention_scratch(n_chains, tq),
        compiler_params=pltpu.CompilerParams(
            dimension_semantics=("arbitrary", "arbitrary"), vmem_limit_bytes=VMEM_LIMIT),
        name="diff_attention",
    )(q, k, vt, kmeta, vtmeta, lam_params, g_sub_col)


def _mla_kernel(q_ref, k_ref, vt_ref, kmeta_ref, vtmeta_ref, y_ref, m_ref, acc_ref,
                sa_ref, sb_ref, smeta_ref):
    qi = pl.program_id(1)
    tq, tk = ATT_TQ, ATT_TK

    def diag_fn(s_t, q0):
        ik, iq = _diag_iotas(*s_t.shape)
        return jnp.where((ik // CHUNK) <= ((iq + q0) // CHUNK), s_t, NEG)

    for h0 in range(0, MLA_HEADS, MLA_HEAD_GROUP):
        heads = range(h0, h0 + MLA_HEAD_GROUP)
        chains = [
            _Chain(load_q=functools.partial(
                       lambda h, rows: q_ref[0, rows, KEY_W * h:KEY_W * (h + 1)], h),
                   kcols=pl.ds(KEY_W * h, KEY_W), vrows=pl.ds(VT_ROWS * h, VT_ROWS),
                   diag_fn=diag_fn)
            for h in heads]
        def finish_head(c, h0=h0):
            h = h0 + c
            y_ref[0, :, LANES * h:LANES * (h + 1)] = (
                _normalised(acc_ref.at[c], MLA_V).T.astype(BF16))

        _attend(qi, chains, kmeta_ref, vtmeta_ref, k_ref, vt_ref, (sa_ref, sb_ref), smeta_ref,
                m_ref, acc_ref,
                finish_head)


def _mla_call(q, k, vt, kmeta, vtmeta):
    nb, s, _ = q.shape
    tq = ATT_TQ
    return pl.pallas_call(
        _mla_kernel,
        grid=(nb, s // tq),
        in_specs=[
            pl.BlockSpec((1, tq, MLA_HEADS * KEY_W), lambda b, i: (b, i, 0)),
            pl.BlockSpec((1, s, MLA_HEADS * KEY_W), lambda b, i: (b, 0, 0)),
            pl.BlockSpec((MLA_HEADS * VT_ROWS, s), lambda b, i: (0, b)),
            pl.BlockSpec((META_PAD, MLA_HEADS * KEY_W), lambda b, i: (0, 0)),
            pl.BlockSpec((MLA_HEADS * VT_ROWS, META_PAD), lambda b, i: (0, 0)),
        ],
        out_specs=pl.BlockSpec((1, tq, MLA_WIDTH), lambda b, i: (b, i, 0)),
        out_shape=jax.ShapeDtypeStruct((nb, s, MLA_WIDTH), BF16),
        scratch_shapes=_attention_scratch(MLA_HEAD_GROUP, tq),
        compiler_params=pltpu.CompilerParams(
            dimension_semantics=("arbitrary", "arbitrary"), vmem_limit_bytes=VMEM_LIMIT),
        name="mla_attention",
    )(q, k, vt, kmeta, vtmeta)


def _out_kernel(x_ref, yda_ref, ymla_ref, wo_ref, gpost_ref, gpre_ref, w1_ref, w2_ref,
                gmlp_ref, o_ref):
    tm = x_ref.shape[0]
    groups = [slice(r, r + tm // OUT_ROW_GROUPS) for r in range(0, tm, tm // OUT_ROW_GROUPS)]
    mix = [_dot(yda_ref[r, :], wo_ref[0:DA_WIDTH, :])
           + _dot(ymla_ref[r, :], wo_ref[DA_WIDTH:DA_WIDTH + MLA_WIDTH, :]) for r in groups]
    h1 = [x_ref[r, :] + _rms_scale(m) * gpost_ref[...] for r, m in zip(groups, mix)]
    u = [(_rms_scale(h) * gpre_ref[...]).astype(BF16) for h in h1]
    for r, h, ug in zip(groups, h1, u):
        f = None
        for c in range(D_FF // FF_CHUNK):
            cols = slice(FF_CHUNK * c, FF_CHUNK * (c + 1))
            hid = jnp.square(jnp.maximum(_dot(ug, w1_ref[:, cols]), 0.0)).astype(BF16)
            part = _dot(hid, w2_ref[cols, :])
            f = part if f is None else f + part
        o_ref[r, :] = h + _rms_scale(f) * gmlp_ref[...]


def _out_call(x2d, yda, ymla, w_o, g_post, g_pre, w1, w2, g_mlp):
    rows = x2d.shape[0]
    tm = OUT_TM
    row = lambda i: (i, 0)
    const = lambda i: (0, 0)
    single = pl.Buffered(1)
    return pl.pallas_call(
        _out_kernel,
        grid=(rows // tm,),
        in_specs=[
            pl.BlockSpec((tm, D_MODEL), row),
            pl.BlockSpec((tm, DA_WIDTH), row),
            pl.BlockSpec((tm, MLA_WIDTH), row),
            pl.BlockSpec((D_MODEL, D_MODEL), const, pipeline_mode=single),
            pl.BlockSpec((1, D_MODEL), const),
            pl.BlockSpec((1, D_MODEL), const),
            pl.BlockSpec((D_MODEL, D_FF), const, pipeline_mode=single),
            pl.BlockSpec((D_FF, D_MODEL), const, pipeline_mode=single),
            pl.BlockSpec((1, D_MODEL), const),
        ],
        out_specs=pl.BlockSpec((tm, D_MODEL), row),
        out_shape=jax.ShapeDtypeStruct((rows, D_MODEL), F32),
        compiler_params=pltpu.CompilerParams(
            dimension_semantics=("arbitrary",), vmem_limit_bytes=VMEM_LIMIT),
        name="out_mlp",
    )(x2d, yda, ymla, w_o, g_post, g_pre, w1, w2, g_mlp)


def _position_tables(n_pos):
    inv_freq = 1.0 / (ROPE_THETA ** (np.arange(0, MLA_ROPE, 2, dtype=np.float64) / MLA_ROPE))
    ang = np.arange(n_pos, dtype=np.float64)[:, None] * inv_freq[None, :]
    cos, sin = np.cos(ang), np.sin(ang)
    zeros = np.zeros((n_pos, LANES - MLA_ROPE))
    ct = np.concatenate([cos, cos, zeros], axis=1).astype(np.float32)
    st = np.concatenate([-sin, sin, zeros], axis=1).astype(np.float32)
    ipos = np.arange(n_pos)
    hi = ((ipos // CHUNK) * CHUNK)[:, None]
    lo = (ipos % CHUNK)[:, None]
    posf = np.concatenate([hi, lo] * ALIBI_PIECES
                          + [np.zeros((n_pos, LANES - 2 * ALIBI_PIECES), np.int64)], axis=1)
    return ct, st, posf.astype(ml_dtypes.bfloat16)


def _swap_halves(w):
    half = w.shape[-1] // 2
    return jnp.concatenate([w[..., half:], w[..., :half]], axis=-1)


def kernel(x, meta_tokens, g_attn_pre, w_in, da_lambda_q1, da_lambda_k1, da_lambda_q2,
           da_lambda_k2, g_da_sub, g_mla_q, w_mla_q_up, g_mla_kv, w_mla_kv_up, w_o,
           g_attn_post, g_mlp_pre, w_ff1, w_ff2, g_mlp_post):
    nb, seq, d = x.shape
    assert d == D_MODEL and w_in.shape[0] == 1, "single-layer block only"
    assert seq % ATT_TQ == 0 and seq % PROJ_TM == 0 and ATT_TQ == ATT_TK

    w_in0 = w_in[0].astype(BF16)
    n_da = 2 * DA_WIDTH
    v_cols = slice(n_da, n_da + DA_WIDTH)
    rest = w_in0[:, n_da + DA_WIDTH:]
    kr = rest[:, MLA_Q_RANK + MLA_KV_RANK:]
    zpad = jnp.zeros((D_MODEL, LANES - MLA_ROPE), BF16)
    w_in_ext = jnp.concatenate(
        [w_in0[:, :n_da], rest[:, :MLA_Q_RANK + MLA_KV_RANK], kr, zpad, _swap_halves(kr), zpad],
        axis=1)
    w_vt = w_in0[:, v_cols].T
    wq = w_mla_q_up[0].astype(BF16).reshape(MLA_Q_RANK, MLA_HEADS, MLA_NOPE + MLA_ROPE)
    wq_rope = wq[..., MLA_NOPE:]
    zq = jnp.zeros((MLA_Q_RANK, MLA_HEADS, LANES - MLA_ROPE), BF16)
    wq_ext = jnp.concatenate(
        [wq[..., :MLA_NOPE], wq_rope, zq, _swap_halves(wq_rope), zq], axis=-1
    ).reshape(MLA_Q_RANK, MLA_HEADS * Q_UP_W)
    wkv = w_mla_kv_up[0].astype(BF16).reshape(MLA_KV_RANK, MLA_HEADS, MLA_NOPE + MLA_V)
    wkv_k = wkv[..., :MLA_NOPE].reshape(MLA_KV_RANK, MLA_HEADS * MLA_NOPE)
    wkv_vt = wkv[..., MLA_NOPE:].reshape(MLA_KV_RANK, MLA_WIDTH).T

    ct, st, posf = _position_tables(N_META + seq)
    lam_rows = jnp.concatenate(
        [da_lambda_q1, da_lambda_k1, da_lambda_q2, da_lambda_k2], axis=0).astype(F32)
    lam_params = jnp.pad(lam_rows, ((0, 8 - lam_rows.shape[0]), (0, LANES - DA_HEAD_DIM)))

    proj = functools.partial(
        _proj_call, g_pre=g_attn_pre, w_in=w_in_ext, w_vt=w_vt, g_q=g_mla_q, w_q=wq_ext,
        g_kv=g_mla_kv, w_kv=wkv_k, w_kvt=wkv_vt)
    x2d = x.reshape(nb * seq, D_MODEL)
    qda, kda, vtda, qm, km, vtm = proj(
        x2d, PROJ_TM, seq // PROJ_TM, ct=ct[N_META:], st=st[N_META:], posf=posf[N_META:],
        name="proj_tokens")
    meta_rows = jnp.pad(meta_tokens.astype(F32), ((0, META_PAD - N_META), (0, 0)))
    _, kda_meta, vtda_meta, _, km_meta, vtm_meta = proj(
        meta_rows, META_PAD, 1, ct=ct[:META_PAD], st=st[:META_PAD], posf=posf[:META_PAD],
        name="proj_meta")

    def per_batch(a):
        return a.reshape(nb, seq, a.shape[-1])

    yda = _da_call(per_batch(qda), per_batch(kda), vtda, kda_meta, vtda_meta,
                   lam_params, g_da_sub.reshape(DA_V_DIM, 1))
    ymla = _mla_call(per_batch(qm), per_batch(km), vtm, km_meta, vtm_meta)

    out = _out_call(x2d, yda.reshape(nb * seq, DA_WIDTH), ymla.reshape(nb * seq, MLA_WIDTH),
                    w_o[0].astype(BF16), g_attn_post, g_mlp_pre,
                    w_ff1[0].astype(BF16), w_ff2[0].astype(BF16), g_mlp_post)
    return out.reshape(nb, seq, D_MODEL)
```

```python
import functools
import math
from typing import Any, Callable, NamedTuple

import jax
import jax.numpy as jnp
import ml_dtypes
import numpy as np
from jax import lax
from jax.experimental import pallas as pl
from jax.experimental.pallas import tpu as pltpu

F32 = jnp.float32
BF16 = jnp.bfloat16

D_MODEL = 1024
N_META = 16
CHUNK = 64
EPS = 1e-6
NEG = -1e30
ROPE_THETA = 10000.0
LOG2E = math.log2(math.e)

DA_HEADS = 4
DA_HEAD_DIM = 64
DA_V_DIM = 128
DA_WIDTH = DA_HEADS * DA_V_DIM
MLA_HEADS = 4
MLA_NOPE = 128
MLA_ROPE = 64
MLA_V = 128
MLA_WIDTH = MLA_HEADS * MLA_V
MLA_Q_RANK = 256
MLA_KV_RANK = 128
D_FF = 4 * D_MODEL
LAMBDA_INIT = 0.8 - 0.6 * math.exp(-0.3 * 0)

LANES = 128
BF16_SUBLANES = 16
KEY_W = 2 * LANES
VT_ROWS = DA_V_DIM + BF16_SUBLANES
VMEM_LIMIT = 58 * 1024 * 1024

PROJ_TM = 1024
PROJ_ROW_GROUPS = 4
ATT_TQ = 512
ATT_TK = 512
OUT_TM = 1024
FF_CHUNK = 1024
OUT_ROW_GROUPS = 4
META_PAD = 128
DA_HEAD_GROUP = 4
MLA_HEAD_GROUP = 4

C_QDA, C_KDA, C_CQ, C_CKV, C_KR, C_KRS, C_END = (0, 512, 1024, 1280, 1408, 1536, 1664)
Q_UP_W = 3 * LANES
ALIBI_PIECES = 3


def _rms_scale(x):
    return x * lax.rsqrt(jnp.mean(x * x, axis=-1, keepdims=True) + EPS)


def _dot(a, b):
    return jnp.dot(a, b, preferred_element_type=F32)


def _dot_nt(a, b):
    return lax.dot_general(a, b, (((1,), (1,)), ((), ())), preferred_element_type=F32)


def _alibi_pieces(h):
    c = np.float32(2.0 ** (-8.0 * (h + 1) / DA_HEADS)) * np.float32(LOG2E)
    pieces, rest = [], c
    for _ in range(ALIBI_PIECES):
        piece = np.float32(ml_dtypes.bfloat16(rest))
        pieces.append(float(piece))
        rest = np.float32(rest - piece)
    assert rest == 0.0
    return float(c), pieces


def _proj_kernel(x_ref, g_ref, win_ref, wvt_ref, gq_ref, wq_ref, gkv_ref, wkv_ref, wkvt_ref,
                 ct_ref, st_ref,
                 qda_ref, kda_ref, vtda_ref, qm_ref, km_ref, vtm_ref):
    tm = x_ref.shape[0]
    n_groups = PROJ_ROW_GROUPS if tm % (PROJ_ROW_GROUPS * LANES) == 0 else 1
    groups = [slice(r, r + tm // n_groups) for r in range(0, tm, tm // n_groups)]
    ones = jnp.ones((BF16_SUBLANES, tm // n_groups), BF16)
    scale = (MLA_NOPE + MLA_ROPE) ** -0.5 * LOG2E

    u = [(_rms_scale(x_ref[r, :]) * g_ref[...]).astype(BF16) for r in groups]
    p = [_dot(ug, win_ref[...]) for ug in u]
    vt = [_dot_nt(wvt_ref[...], ug) for ug in u]

    for r, pg, vtg in zip(groups, p, vt):
        qda_ref[r, :] = (pg[:, C_QDA:C_KDA] * (DA_HEAD_DIM ** -0.5 * LOG2E)).astype(BF16)
        kda_ref[r, :] = pg[:, C_KDA:C_CQ].astype(BF16)
        for h in range(DA_HEADS):
            vtda_ref[VT_ROWS * h:VT_ROWS * h + DA_V_DIM, r] = (
                vtg[DA_V_DIM * h:DA_V_DIM * (h + 1), :].astype(BF16))
            vtda_ref[VT_ROWS * h + DA_V_DIM:VT_ROWS * (h + 1), r] = ones

    cq = [(_rms_scale(pg[:, C_CQ:C_CKV]) * gq_ref[...]).astype(BF16) for pg in p]
    ckv = [(_rms_scale(pg[:, C_CKV:C_KR]) * gkv_ref[...]).astype(BF16) for pg in p]
    qu = [_dot(c, wq_ref[...]) for c in cq]
    kn = [_dot(c, wkv_ref[...]) for c in ckv]
    vtm = [_dot_nt(wkvt_ref[...], c) for c in ckv]
    for r, pg, qug, kng, vtmg in zip(groups, p, qu, kn, vtm):
        ct = ct_ref[r, :]
        st = st_ref[r, :]
        k_rope = (pg[:, C_KR:C_KRS] * ct + pg[:, C_KRS:C_END] * st).astype(BF16)
        for h in range(MLA_HEADS):
            b = Q_UP_W * h
            qm_ref[r, KEY_W * h:KEY_W * h + LANES] = (qug[:, b:b + LANES] * scale).astype(BF16)
            q_rope = (qug[:, b + LANES:b + 2 * LANES] * ct
                      + qug[:, b + 2 * LANES:b + 3 * LANES] * st)
            qm_ref[r, KEY_W * h + LANES:KEY_W * (h + 1)] = (q_rope * scale).astype(BF16)
            km_ref[r, KEY_W * h:KEY_W * h + LANES] = kng[:, LANES * h:LANES * (h + 1)].astype(BF16)
            km_ref[r, KEY_W * h + LANES:KEY_W * (h + 1)] = k_rope
            vtm_ref[VT_ROWS * h:VT_ROWS * h + MLA_V, r] = (
                vtmg[MLA_V * h:MLA_V * (h + 1), :].astype(BF16))
            vtm_ref[VT_ROWS * h + MLA_V:VT_ROWS * (h + 1), r] = ones


def _proj_call(x2d, tm, n_tab_blocks, g_pre, w_in, w_vt, g_q, w_q, g_kv, w_kv, w_kvt,
               ct, st, name):
    rows = x2d.shape[0]
    grid = (rows // tm,)
    row = lambda i: (i, 0)
    col = lambda i: (0, i)
    const = lambda i: (0, 0)
    tab = lambda i: (i % n_tab_blocks, 0)
    row_out = lambda w: (pl.BlockSpec((tm, w), row), jax.ShapeDtypeStruct((rows, w), BF16))
    col_out = lambda r: (pl.BlockSpec((r, tm), col), jax.ShapeDtypeStruct((r, rows), BF16))
    outs = [row_out(DA_WIDTH), row_out(DA_WIDTH), col_out(DA_HEADS * VT_ROWS),
            row_out(MLA_HEADS * KEY_W), row_out(MLA_HEADS * KEY_W), col_out(MLA_HEADS * VT_ROWS)]
    return pl.pallas_call(
        _proj_kernel,
        grid=grid,
        in_specs=[
            pl.BlockSpec((tm, D_MODEL), row),
            pl.BlockSpec((1, D_MODEL), const),
            pl.BlockSpec((D_MODEL, C_END), const),
            pl.BlockSpec((DA_WIDTH, D_MODEL), const),
            pl.BlockSpec((1, MLA_Q_RANK), const),
            pl.BlockSpec((MLA_Q_RANK, MLA_HEADS * Q_UP_W), const),
            pl.BlockSpec((1, MLA_KV_RANK), const),
            pl.BlockSpec((MLA_KV_RANK, MLA_HEADS * MLA_NOPE), const),
            pl.BlockSpec((MLA_WIDTH, MLA_KV_RANK), const),
            pl.BlockSpec((tm, LANES), tab),
            pl.BlockSpec((tm, LANES), tab),
        ],
        out_specs=[o[0] for o in outs],
        out_shape=[o[1] for o in outs],
        compiler_params=pltpu.CompilerParams(
            dimension_semantics=("arbitrary",), vmem_limit_bytes=VMEM_LIMIT),
        name=name,
    )(x2d, g_pre, w_in, w_vt, g_q, w_q, g_kv, w_kv, w_kvt, ct, st)


def _softmax_step(s_t, vt, m_ref, acc_ref):
    m_prev = m_ref[...]
    m_new = jnp.maximum(m_prev, jnp.max(s_t, axis=0, keepdims=True))
    p_t = jnp.exp2(s_t - m_new).astype(BF16)
    acc_ref[...] = jnp.exp2(m_prev - m_new) * acc_ref[...] + _dot(vt, p_t)
    m_ref[...] = m_new


def _diag_iotas(tk, tq):
    ik = lax.broadcasted_iota(jnp.int32, (tk, tq), 0)
    iq = lax.broadcasted_iota(jnp.int32, (tk, tq), 1)
    return ik, iq


def _normalised(acc_ref, dv):
    return acc_ref[0:dv, :] / acc_ref[dv:dv + 1, :]


class _Chain(NamedTuple):
    load_q: Callable[[Any], jax.Array]
    load_k: Callable[[Any], jax.Array]
    load_kmeta: Callable[[], jax.Array]
    vrows: Any
    diag_fn: Callable[[jax.Array, int], jax.Array]


def _attend(qi, chains, vtmeta_ref, vt_ref, s_bufs, smeta_ref, m_ref, acc_ref,
            on_chain_done):
    tk, tq = ATT_TK, ATT_TQ
    half = tq // 2
    buf_a, buf_b = s_bufs
    halves = ((0, half), (half, tk))

    m_ref[...] = jnp.full(m_ref.shape, NEG, F32)
    acc_ref[...] = jnp.zeros(acc_ref.shape, F32)

    def key_block(j, size=tk):
        start = j * tk
        return pl.ds(start if isinstance(j, int) else pl.multiple_of(start, tk), size)

    def scores_into(buf, j):
        for c, ch in enumerate(chains):
            buf[c] = _dot_nt(ch.load_k(key_block(j)), ch.load_q(slice(None)))

    def consume(buf, j):
        for c, ch in enumerate(chains):
            _softmax_step(buf[c], vt_ref[ch.vrows, key_block(j)], m_ref.at[c], acc_ref.at[c])

    def diagonal_scores_into(buf, jd):
        for c, ch in enumerate(chains):
            kmeta = ch.load_kmeta()
            for q0, n_keys in halves:
                keys = jnp.concatenate([ch.load_k(key_block(jd, n_keys)), kmeta], axis=0)
                s_t = _dot_nt(keys, ch.load_q(slice(q0, q0 + half)))
                buf[c, 0:n_keys, q0:q0 + half] = s_t[0:n_keys]
                smeta_ref[c, :, q0:q0 + half] = s_t[n_keys:n_keys + META_PAD]

    def finish(buf, jd):
        meta_valid = lax.broadcasted_iota(jnp.int32, (META_PAD, half), 0) < N_META
        for c, ch in enumerate(chains):
            for q0, n_keys in halves:
                lanes = pl.ds(q0, half)
                s_t = jnp.concatenate(
                    [ch.diag_fn(buf[c, 0:n_keys, q0:q0 + half], q0),
                     jnp.where(meta_valid, smeta_ref[c, :, q0:q0 + half], NEG)], axis=0)
                vt = jnp.concatenate(
                    [vt_ref[ch.vrows, key_block(jd, n_keys)], vtmeta_ref[ch.vrows, :]], axis=1)
                _softmax_step(s_t, vt, m_ref.at[c, :, lanes], acc_ref.at[c, :, lanes])
            on_chain_done(c)

    def pair(t):
        j = 2 * t
        scores_into(buf_b, j + 1)
        consume(buf_a, j)
        scores_into(buf_a, j + 2)
        consume(buf_b, j + 1)

    def odd_tail(n):
        diagonal_scores_into(buf_b, n)
        consume(buf_a, n - 1)
        finish(buf_b, n)

    def even_tail(n):
        scores_into(buf_b, n - 1)
        consume(buf_a, n - 2)
        diagonal_scores_into(buf_a, n)
        consume(buf_b, n - 1)
        finish(buf_a, n)

    @pl.when(qi == 0)
    def _():
        diagonal_scores_into(buf_a, 0)
        finish(buf_a, 0)

    @pl.when(qi == 1)
    def _():
        scores_into(buf_a, 0)
        odd_tail(1)

    @pl.when(qi == 2)
    def _():
        scores_into(buf_a, 0)
        even_tail(2)

    @pl.when(qi >= 3)
    def _():
        scores_into(buf_a, 0)
        pair(0)

        def body(t, carry):
            pair(t)
            return carry

        lax.fori_loop(1, (qi - 1) // 2, body, 0)

        @pl.when(qi % 2 == 1)
        def _():
            odd_tail(qi)

        @pl.when(qi % 2 == 0)
        def _():
            even_tail(qi)


def _attention_scratch(n_chains, tq):
    stat = pltpu.VMEM((n_chains, 1, tq), F32)
    scores = pltpu.VMEM((n_chains, ATT_TK, tq), F32)
    return [stat, pltpu.VMEM((n_chains, VT_ROWS, tq), F32), scores, scores,
            pltpu.VMEM((n_chains, META_PAD, tq), F32)]


def _da_kernel(q_ref, k_ref, posf_ref, vt_ref, kmeta_ref, posfmeta_ref, vtmeta_ref, lam_ref,
               gsub_ref, y_ref,
               qx_ref, m_ref, acc_ref, sa_ref, sb_ref, smeta_ref):
    qi = pl.program_id(1)
    tq, tk = ATT_TQ, ATT_TK
    lane = lax.broadcasted_iota(jnp.int32, (tq, LANES), 1)
    lp = lam_ref[...]
    lam = (jnp.exp(jnp.sum(lp[0:1] * lp[1:2], axis=-1, keepdims=True))
           - jnp.exp(jnp.sum(lp[2:3] * lp[3:4], axis=-1, keepdims=True)) + LAMBDA_INIT)

    for h0 in range(0, DA_HEADS, DA_HEAD_GROUP):
        chains = []
        for g in range(DA_HEAD_GROUP):
            h = h0 + g
            c, pieces = _alibi_pieces(h)
            q = q_ref[0, :, LANES * h:LANES * (h + 1)]
            pf = jnp.zeros((tq, LANES), F32)
            for i, piece in enumerate(pieces):
                pf = jnp.where((lane == 2 * i) | (lane == 2 * i + 1), piece, pf)
            pf = pf.astype(BF16)
            zero = jnp.zeros_like(q)
            qx_ref[2 * g, :, 0:LANES] = jnp.where(lane < DA_HEAD_DIM, q, zero)
            qx_ref[2 * g + 1, :, 0:LANES] = jnp.where(lane >= DA_HEAD_DIM, q, zero)
            qx_ref[2 * g, :, LANES:KEY_W] = pf
            qx_ref[2 * g + 1, :, LANES:KEY_W] = pf

            def diag_fn(s_t, q0, c=c):
                ik, iq = _diag_iotas(*s_t.shape)
                iq = iq + q0
                visible = (ik // CHUNK) <= (iq // CHUNK)
                corr = (2.0 * c) * jnp.minimum(iq - ik, 0).astype(F32)
                return jnp.where(visible, s_t + corr, NEG)

            def load_k(rows, h=h):
                return jnp.concatenate(
                    [k_ref[0, rows, LANES * h:LANES * (h + 1)], posf_ref[rows, :]], axis=1)

            def load_kmeta(h=h):
                return jnp.concatenate(
                    [kmeta_ref[:, LANES * h:LANES * (h + 1)], posfmeta_ref[...]], axis=1)

            for comp in range(2):
                chains.append(_Chain(
                    load_q=functools.partial(lambda i, rows: qx_ref[i, rows, :], 2 * g + comp),
                    load_k=load_k, load_kmeta=load_kmeta, vrows=pl.ds(VT_ROWS * h, VT_ROWS),
                    diag_fn=diag_fn))

        def finish_head(c, h0=h0):
            if c % 2 == 0:
                return
            h = h0 + c // 2
            o = (_normalised(acc_ref.at[c - 1], DA_V_DIM)
                 - lam * _normalised(acc_ref.at[c], DA_V_DIM))
            o = o * lax.rsqrt(jnp.mean(o * o, axis=0, keepdims=True) + EPS)
            o = o * gsub_ref[...] * (1.0 - LAMBDA_INIT)
            y_ref[0, :, LANES * h:LANES * (h + 1)] = o.T.astype(BF16)

        _attend(qi, chains, vtmeta_ref, vt_ref, (sa_ref, sb_ref), smeta_ref, m_ref, acc_ref,
                finish_head)


def _da_call(q, k, posf, vt, kmeta, posf_meta, vtmeta, lam_params, g_sub_col):
    nb, s, _ = q.shape
    tq = ATT_TQ
    n_chains = 2 * DA_HEAD_GROUP
    return pl.pallas_call(
        _da_kernel,
        grid=(nb, s // tq),
        in_specs=[
            pl.BlockSpec((1, tq, DA_WIDTH), lambda b, i: (b, i, 0)),
            pl.BlockSpec((1, s, DA_WIDTH), lambda b, i: (b, 0, 0)),
            pl.BlockSpec((s, LANES), lambda b, i: (0, 0), pipeline_mode=pl.Buffered(1)),
            pl.BlockSpec((DA_HEADS * VT_ROWS, s), lambda b, i: (0, b),
                         pipeline_mode=pl.Buffered(1)),
            pl.BlockSpec((META_PAD, DA_WIDTH), lambda b, i: (0, 0)),
            pl.BlockSpec((META_PAD, LANES), lambda b, i: (0, 0)),
            pl.BlockSpec((DA_HEADS * VT_ROWS, META_PAD), lambda b, i: (0, 0)),
            pl.BlockSpec((8, LANES), lambda b, i: (0, 0)),
            pl.BlockSpec((DA_V_DIM, 1), lambda b, i: (0, 0)),
        ],
        out_specs=pl.BlockSpec((1, tq, DA_WIDTH), lambda b, i: (b, i, 0)),
        out_shape=jax.ShapeDtypeStruct((nb, s, DA_WIDTH), BF16),
        scratch_shapes=[pltpu.VMEM((n_chains, tq, KEY_W), BF16)] + _attention_scratch(n_chains, tq),
        compiler_params=pltpu.CompilerParams(
            dimension_semantics=("arbitrary", "arbitrary"), vmem_limit_bytes=VMEM_LIMIT),
        name="diff_attention",
    )(q, k, posf, vt, kmeta, posf_meta, vtmeta, lam_params, g_sub_col)


def _mla_kernel(q_ref, k_ref, vt_ref, kmeta_ref, vtmeta_ref, y_ref, m_ref, acc_ref,
                sa_ref, sb_ref, smeta_ref):
    qi = pl.program_id(1)
    tq, tk = ATT_TQ, ATT_TK

    def diag_fn(s_t, q0):
        ik, iq = _diag_iotas(*s_t.shape)
        return jnp.where((ik // CHUNK) <= ((iq + q0) // CHUNK), s_t, NEG)

    for h0 in range(0, MLA_HEADS, MLA_HEAD_GROUP):
        heads = range(h0, h0 + MLA_HEAD_GROUP)
        chains = [
            _Chain(load_q=functools.partial(
                       lambda h, rows: q_ref[0, rows, KEY_W * h:KEY_W * (h + 1)], h),
                   load_k=functools.partial(
                       lambda h, rows: k_ref[0, rows, KEY_W * h:KEY_W * (h + 1)], h),
                   load_kmeta=functools.partial(
                       lambda h: kmeta_ref[:, KEY_W * h:KEY_W * (h + 1)], h),
                   vrows=pl.ds(VT_ROWS * h, VT_ROWS), diag_fn=diag_fn)
            for h in heads]
        def finish_head(c, h0=h0):
            h = h0 + c
            y_ref[0, :, LANES * h:LANES * (h + 1)] = (
                _normalised(acc_ref.at[c], MLA_V).T.astype(BF16))

        _attend(qi, chains, vtmeta_ref, vt_ref, (sa_ref, sb_ref), smeta_ref, m_ref, acc_ref,
                finish_head)


def _mla_call(q, k, vt, kmeta, vtmeta):
    nb, s, _ = q.shape
    tq = ATT_TQ
    return pl.pallas_call(
        _mla_kernel,
        grid=(nb, s // tq),
        in_specs=[
            pl.BlockSpec((1, tq, MLA_HEADS * KEY_W), lambda b, i: (b, i, 0)),
            pl.BlockSpec((1, s, MLA_HEADS * KEY_W), lambda b, i: (b, 0, 0)),
            pl.BlockSpec((MLA_HEADS * VT_ROWS, s), lambda b, i: (0, b)),
            pl.BlockSpec((META_PAD, MLA_HEADS * KEY_W), lambda b, i: (0, 0)),
            pl.BlockSpec((MLA_HEADS * VT_ROWS, META_PAD), lambda b, i: (0, 0)),
        ],
        out_specs=pl.BlockSpec((1, tq, MLA_WIDTH), lambda b, i: (b, i, 0)),
        out_shape=jax.ShapeDtypeStruct((nb, s, MLA_WIDTH), BF16),
        scratch_shapes=_attention_scratch(MLA_HEAD_GROUP, tq),
        compiler_params=pltpu.CompilerParams(
            dimension_semantics=("arbitrary", "arbitrary"), vmem_limit_bytes=VMEM_LIMIT),
        name="mla_attention",
    )(q, k, vt, kmeta, vtmeta)


def _out_kernel(x_ref, yda_ref, ymla_ref, wo_ref, gpost_ref, gpre_ref, w1_ref, w2_ref,
                gmlp_ref, o_ref):
    tm = x_ref.shape[0]
    groups = [slice(r, r + tm // OUT_ROW_GROUPS) for r in range(0, tm, tm // OUT_ROW_GROUPS)]
    mix = [_dot(yda_ref[r, :], wo_ref[0:DA_WIDTH, :])
           + _dot(ymla_ref[r, :], wo_ref[DA_WIDTH:DA_WIDTH + MLA_WIDTH, :]) for r in groups]
    h1 = [x_ref[r, :] + _rms_scale(m) * gpost_ref[...] for r, m in zip(groups, mix)]
    u = [(_rms_scale(h) * gpre_ref[...]).astype(BF16) for h in h1]
    for r, h, ug in zip(groups, h1, u):
        f = None
        for c in range(D_FF // FF_CHUNK):
            cols = slice(FF_CHUNK * c, FF_CHUNK * (c + 1))
            hid = jnp.square(jnp.maximum(_dot(ug, w1_ref[:, cols]), 0.0)).astype(BF16)
            part = _dot(hid, w2_ref[cols, :])
            f = part if f is None else f + part
        o_ref[r, :] = h + _rms_scale(f) * gmlp_ref[...]


def _out_call(x2d, yda, ymla, w_o, g_post, g_pre, w1, w2, g_mlp):
    rows = x2d.shape[0]
    tm = OUT_TM
    row = lambda i: (i, 0)
    const = lambda i: (0, 0)
    single = pl.Buffered(1)
    return pl.pallas_call(
        _out_kernel,
        grid=(rows // tm,),
        in_specs=[
            pl.BlockSpec((tm, D_MODEL), row),
            pl.BlockSpec((tm, DA_WIDTH), row),
            pl.BlockSpec((tm, MLA_WIDTH), row),
            pl.BlockSpec((D_MODEL, D_MODEL), const, pipeline_mode=single),
            pl.BlockSpec((1, D_MODEL), const),
            pl.BlockSpec((1, D_MODEL), const),
            pl.BlockSpec((D_MODEL, D_FF), const, pipeline_mode=single),
            pl.BlockSpec((D_FF, D_MODEL), const, pipeline_mode=single),
            pl.BlockSpec((1, D_MODEL), const),
        ],
        out_specs=pl.BlockSpec((tm, D_MODEL), row),
        out_shape=jax.ShapeDtypeStruct((rows, D_MODEL), F32),
        compiler_params=pltpu.CompilerParams(
            dimension_semantics=("arbitrary",), vmem_limit_bytes=VMEM_LIMIT),
        name="out_mlp",
    )(x2d, yda, ymla, w_o, g_post, g_pre, w1, w2, g_mlp)


def _position_tables(n_pos):
    inv_freq = 1.0 / (ROPE_THETA ** (np.arange(0, MLA_ROPE, 2, dtype=np.float64) / MLA_ROPE))
    ang = np.arange(n_pos, dtype=np.float64)[:, None] * inv_freq[None, :]
    cos, sin = np.cos(ang), np.sin(ang)
    zeros = np.zeros((n_pos, LANES - MLA_ROPE))
    ct = np.concatenate([cos, cos, zeros], axis=1).astype(np.float32)
    st = np.concatenate([-sin, sin, zeros], axis=1).astype(np.float32)
    ipos = np.arange(n_pos)
    hi = ((ipos // CHUNK) * CHUNK)[:, None]
    lo = (ipos % CHUNK)[:, None]
    posf = np.concatenate([hi, lo] * ALIBI_PIECES
                          + [np.zeros((n_pos, LANES - 2 * ALIBI_PIECES), np.int64)], axis=1)
    return ct, st, posf.astype(ml_dtypes.bfloat16)


def _swap_halves(w):
    half = w.shape[-1] // 2
    return jnp.concatenate([w[..., half:], w[..., :half]], axis=-1)


def kernel(x, meta_tokens, g_attn_pre, w_in, da_lambda_q1, da_lambda_k1, da_lambda_q2,
           da_lambda_k2, g_da_sub, g_mla_q, w_mla_q_up, g_mla_kv, w_mla_kv_up, w_o,
           g_attn_post, g_mlp_pre, w_ff1, w_ff2, g_mlp_post):
    nb, seq, d = x.shape
    assert d == D_MODEL and w_in.shape[0] == 1, "single-layer block only"
    assert seq % ATT_TQ == 0 and seq % PROJ_TM == 0 and ATT_TQ == ATT_TK

    w_in0 = w_in[0].astype(BF16)
    n_da = 2 * DA_WIDTH
    v_cols = slice(n_da, n_da + DA_WIDTH)
    rest = w_in0[:, n_da + DA_WIDTH:]
    kr = rest[:, MLA_Q_RANK + MLA_KV_RANK:]
    zpad = jnp.zeros((D_MODEL, LANES - MLA_ROPE), BF16)
    w_in_ext = jnp.concatenate(
        [w_in0[:, :n_da], rest[:, :MLA_Q_RANK + MLA_KV_RANK], kr, zpad, _swap_halves(kr), zpad],
        axis=1)
    w_vt = w_in0[:, v_cols].T
    wq = w_mla_q_up[0].astype(BF16).reshape(MLA_Q_RANK, MLA_HEADS, MLA_NOPE + MLA_ROPE)
    wq_rope = wq[..., MLA_NOPE:]
    zq = jnp.zeros((MLA_Q_RANK, MLA_HEADS, LANES - MLA_ROPE), BF16)
    wq_ext = jnp.concatenate(
        [wq[..., :MLA_NOPE], wq_rope, zq, _swap_halves(wq_rope), zq], axis=-1
    ).reshape(MLA_Q_RANK, MLA_HEADS * Q_UP_W)
    wkv = w_mla_kv_up[0].astype(BF16).reshape(MLA_KV_RANK, MLA_HEADS, MLA_NOPE + MLA_V)
    wkv_k = wkv[..., :MLA_NOPE].reshape(MLA_KV_RANK, MLA_HEADS * MLA_NOPE)
    wkv_vt = wkv[..., MLA_NOPE:].reshape(MLA_KV_RANK, MLA_WIDTH).T

    ct, st, posf = _position_tables(N_META + seq)
    lam_rows = jnp.concatenate(
        [da_lambda_q1, da_lambda_k1, da_lambda_q2, da_lambda_k2], axis=0).astype(F32)
    lam_params = jnp.pad(lam_rows, ((0, 8 - lam_rows.shape[0]), (0, LANES - DA_HEAD_DIM)))

    proj = functools.partial(
        _proj_call, g_pre=g_attn_pre, w_in=w_in_ext, w_vt=w_vt, g_q=g_mla_q, w_q=wq_ext,
        g_kv=g_mla_kv, w_kv=wkv_k, w_kvt=wkv_vt)
    x2d = x.reshape(nb * seq, D_MODEL)
    qda, kda, vtda, qm, km, vtm = proj(
        x2d, PROJ_TM, seq // PROJ_TM, ct=ct[N_META:], st=st[N_META:], name="proj_tokens")
    meta_rows = jnp.pad(meta_tokens.astype(F32), ((0, META_PAD - N_META), (0, 0)))
    _, kda_meta, vtda_meta, _, km_meta, vtm_meta = proj(
        meta_rows, META_PAD, 1, ct=ct[:META_PAD], st=st[:META_PAD], name="proj_meta")

    def per_batch(a):
        return a.reshape(nb, seq, a.shape[-1])

    yda = _da_call(per_batch(qda), per_batch(kda), posf[N_META:], vtda, kda_meta, posf[:META_PAD],
                   vtda_meta, lam_params, g_da_sub.reshape(DA_V_DIM, 1))
    ymla = _mla_call(per_batch(qm), per_batch(km), vtm, km_meta, vtm_meta)

    out = _out_call(x2d, yda.reshape(nb * seq, DA_WIDTH), ymla.reshape(nb * seq, MLA_WIDTH),
                    w_o[0].astype(BF16), g_attn_post, g_mlp_pre,
                    w_ff1[0].astype(BF16), w_ff2[0].astype(BF16), g_mlp_post)
    return out.reshape(nb, seq, D_MODEL)
```

```python
import functools
import math
from typing import Any, Callable, NamedTuple

import jax
import jax.numpy as jnp
import ml_dtypes
import numpy as np
from jax import lax
from jax.experimental import pallas as pl
from jax.experimental.pallas import tpu as pltpu

F32 = jnp.float32
BF16 = jnp.bfloat16

D_MODEL = 1024
N_META = 16
CHUNK = 64
EPS = 1e-6
NEG = -1e30
ROPE_THETA = 10000.0
LOG2E = math.log2(math.e)

DA_HEADS = 4
DA_HEAD_DIM = 64
DA_V_DIM = 128
DA_WIDTH = DA_HEADS * DA_V_DIM
MLA_HEADS = 4
MLA_NOPE = 128
MLA_ROPE = 64
MLA_V = 128
MLA_WIDTH = MLA_HEADS * MLA_V
MLA_Q_RANK = 256
MLA_KV_RANK = 128
D_FF = 4 * D_MODEL
LAMBDA_INIT = 0.8 - 0.6 * math.exp(-0.3 * 0)

LANES = 128
BF16_SUBLANES = 16
KEY_W = 2 * LANES
VT_ROWS = DA_V_DIM + BF16_SUBLANES
VMEM_LIMIT = 58 * 1024 * 1024

PROJ_TM = 1024
PROJ_ROW_GROUPS = 4
ATT_TQ = 512
ATT_TK = 512
OUT_TM = 1024
FF_CHUNK = 1024
OUT_ROW_GROUPS = 4
META_PAD = 128
DA_HEAD_GROUP = 4
MLA_HEAD_GROUP = 4

C_QDA, C_KDA, C_CQ, C_CKV, C_KR, C_END = (0, 512, 1024, 1280, 1408, 1536)
Q_ROPE0 = MLA_HEADS * MLA_NOPE
Q_UP_W = Q_ROPE0 + MLA_HEADS * MLA_ROPE
ALIBI_PIECES = 3


def _rms_scale(x):
    return x * lax.rsqrt(jnp.mean(x * x, axis=-1, keepdims=True) + EPS)


def _dot(a, b):
    return jnp.dot(a, b, preferred_element_type=F32)


def _dot_nt(a, b):
    return lax.dot_general(a, b, (((1,), (1,)), ((), ())), preferred_element_type=F32)


def _alibi_pieces(h):
    c = np.float32(2.0 ** (-8.0 * (h + 1) / DA_HEADS)) * np.float32(LOG2E)
    pieces, rest = [], c
    for _ in range(ALIBI_PIECES):
        piece = np.float32(ml_dtypes.bfloat16(rest))
        pieces.append(float(piece))
        rest = np.float32(rest - piece)
    assert rest == 0.0
    return float(c), pieces


def _proj_kernel(x_ref, g_ref, win_ref, wvt_ref, gq_ref, wq_ref, gkv_ref, wkv_ref, wkvt_ref,
                 ct_ref, st_ref,
                 qda_ref, kda_ref, vtda_ref, qm_ref, km_ref, vtm_ref):
    tm = x_ref.shape[0]
    n_groups = PROJ_ROW_GROUPS if tm % (PROJ_ROW_GROUPS * LANES) == 0 else 1
    groups = [slice(r, r + tm // n_groups) for r in range(0, tm, tm // n_groups)]
    ones = jnp.ones((BF16_SUBLANES, tm // n_groups), BF16)
    scale = (MLA_NOPE + MLA_ROPE) ** -0.5 * LOG2E

    u = [(_rms_scale(x_ref[r, :]) * g_ref[...]).astype(BF16) for r in groups]
    p = [_dot(ug, win_ref[...]) for ug in u]
    vt = [_dot_nt(wvt_ref[...], ug) for ug in u]

    for r, pg, vtg in zip(groups, p, vt):
        qda_ref[r, :] = (pg[:, C_QDA:C_KDA] * (DA_HEAD_DIM ** -0.5 * LOG2E)).astype(BF16)
        kda_ref[r, :] = pg[:, C_KDA:C_CQ].astype(BF16)
        for h in range(DA_HEADS):
            vtda_ref[VT_ROWS * h:VT_ROWS * h + DA_V_DIM, r] = (
                vtg[DA_V_DIM * h:DA_V_DIM * (h + 1), :].astype(BF16))
            vtda_ref[VT_ROWS * h + DA_V_DIM:VT_ROWS * (h + 1), r] = ones

    cq = [(_rms_scale(pg[:, C_CQ:C_CKV]) * gq_ref[...]).astype(BF16) for pg in p]
    ckv = [(_rms_scale(pg[:, C_CKV:C_KR]) * gkv_ref[...]).astype(BF16) for pg in p]
    qu = [_dot(c, wq_ref[...]) for c in cq]
    kn = [_dot(c, wkv_ref[...]) for c in ckv]
    vtm = [_dot_nt(wkvt_ref[...], c) for c in ckv]
    lane = lax.broadcasted_iota(jnp.int32, (tm // n_groups, LANES), 1)
    first_half = (lane % MLA_ROPE) < MLA_ROPE // 2
    low = lane < MLA_ROPE

    def rope(xr, ct, st):
        swapped = jnp.where(first_half, pltpu.roll(xr, LANES - MLA_ROPE // 2, 1),
                            pltpu.roll(xr, MLA_ROPE // 2, 1))
        return xr * ct + swapped * st

    for r, pg, qug, kng, vtmg in zip(groups, p, qu, kn, vtm):
        ct = ct_ref[r, :]
        st = st_ref[r, :]
        k_rope = jnp.where(low, rope(pg[:, C_KR:C_END], ct, st), 0.0).astype(BF16)
        for t in range(MLA_HEADS // 2):
            roped = rope(qug[:, Q_ROPE0 + LANES * t:Q_ROPE0 + LANES * (t + 1)], ct, st) * scale
            for h, part in ((2 * t, roped), (2 * t + 1, pltpu.roll(roped, MLA_ROPE, 1))):
                qm_ref[r, KEY_W * h + LANES:KEY_W * (h + 1)] = (
                    jnp.where(low, part, 0.0).astype(BF16))
        for h in range(MLA_HEADS):
            qm_ref[r, KEY_W * h:KEY_W * h + LANES] = (
                qug[:, MLA_NOPE * h:MLA_NOPE * (h + 1)] * scale).astype(BF16)
            km_ref[r, KEY_W * h:KEY_W * h + LANES] = kng[:, LANES * h:LANES * (h + 1)].astype(BF16)
            km_ref[r, KEY_W * h + LANES:KEY_W * (h + 1)] = k_rope
            vtm_ref[VT_ROWS * h:VT_ROWS * h + MLA_V, r] = (
                vtmg[MLA_V * h:MLA_V * (h + 1), :].astype(BF16))
            vtm_ref[VT_ROWS * h + MLA_V:VT_ROWS * (h + 1), r] = ones


def _proj_call(x2d, tm, n_tab_blocks, g_pre, w_in, w_vt, g_q, w_q, g_kv, w_kv, w_kvt,
               ct, st, name):
    rows = x2d.shape[0]
    grid = (rows // tm,)
    row = lambda i: (i, 0)
    col = lambda i: (0, i)
    const = lambda i: (0, 0)
    tab = lambda i: (i % n_tab_blocks, 0)
    row_out = lambda w: (pl.BlockSpec((tm, w), row), jax.ShapeDtypeStruct((rows, w), BF16))
    col_out = lambda r: (pl.BlockSpec((r, tm), col), jax.ShapeDtypeStruct((r, rows), BF16))
    outs = [row_out(DA_WIDTH), row_out(DA_WIDTH), col_out(DA_HEADS * VT_ROWS),
            row_out(MLA_HEADS * KEY_W), row_out(MLA_HEADS * KEY_W), col_out(MLA_HEADS * VT_ROWS)]
    return pl.pallas_call(
        _proj_kernel,
        grid=grid,
        in_specs=[
            pl.BlockSpec((tm, D_MODEL), row),
            pl.BlockSpec((1, D_MODEL), const),
            pl.BlockSpec((D_MODEL, C_END), const),
            pl.BlockSpec((DA_WIDTH, D_MODEL), const),
            pl.BlockSpec((1, MLA_Q_RANK), const),
            pl.BlockSpec((MLA_Q_RANK, Q_UP_W), const),
            pl.BlockSpec((1, MLA_KV_RANK), const),
            pl.BlockSpec((MLA_KV_RANK, MLA_HEADS * MLA_NOPE), const),
            pl.BlockSpec((MLA_WIDTH, MLA_KV_RANK), const),
            pl.BlockSpec((tm, LANES), tab),
            pl.BlockSpec((tm, LANES), tab),
        ],
        out_specs=[o[0] for o in outs],
        out_shape=[o[1] for o in outs],
        compiler_params=pltpu.CompilerParams(
            dimension_semantics=("arbitrary",), vmem_limit_bytes=VMEM_LIMIT),
        name=name,
    )(x2d, g_pre, w_in, w_vt, g_q, w_q, g_kv, w_kv, w_kvt, ct, st)


def _softmax_step(s_t, vt, m_ref, acc_ref):
    m_prev = m_ref[...]
    m_new = jnp.maximum(m_prev, jnp.max(s_t, axis=0, keepdims=True))
    p_t = jnp.exp2(s_t - m_new).astype(BF16)
    acc_ref[...] = jnp.exp2(m_prev - m_new) * acc_ref[...] + _dot(vt, p_t)
    m_ref[...] = m_new


def _diag_iotas(tk, tq):
    ik = lax.broadcasted_iota(jnp.int32, (tk, tq), 0)
    iq = lax.broadcasted_iota(jnp.int32, (tk, tq), 1)
    return ik, iq


def _normalised(acc_ref, dv):
    return acc_ref[0:dv, :] / acc_ref[dv:dv + 1, :]


class _Chain(NamedTuple):
    load_q: Callable[[Any], jax.Array]
    load_k: Callable[[Any], jax.Array]
    load_kmeta: Callable[[], jax.Array]
    vrows: Any
    diag_fn: Callable[[jax.Array, int], jax.Array]


def _attend(qi, chains, vtmeta_ref, vt_ref, s_bufs, smeta_ref, m_ref, acc_ref,
            on_chain_done):
    tk, tq = ATT_TK, ATT_TQ
    half = tq // 2
    buf_a, buf_b = s_bufs
    halves = ((0, half), (half, tk))

    m_ref[...] = jnp.full(m_ref.shape, NEG, F32)
    acc_ref[...] = jnp.zeros(acc_ref.shape, F32)

    def key_block(j, size=tk):
        start = j * tk
        return pl.ds(start if isinstance(j, int) else pl.multiple_of(start, tk), size)

    def scores_into(buf, j):
        for c, ch in enumerate(chains):
            buf[c] = _dot_nt(ch.load_k(key_block(j)), ch.load_q(slice(None)))

    def consume(buf, j):
        for c, ch in enumerate(chains):
            _softmax_step(buf[c], vt_ref[ch.vrows, key_block(j)], m_ref.at[c], acc_ref.at[c])

    def diagonal_scores_into(buf, jd):
        for c, ch in enumerate(chains):
            kmeta = ch.load_kmeta()
            for q0, n_keys in halves:
                keys = jnp.concatenate([ch.load_k(key_block(jd, n_keys)), kmeta], axis=0)
                s_t = _dot_nt(keys, ch.load_q(slice(q0, q0 + half)))
                buf[c, 0:n_keys, q0:q0 + half] = s_t[0:n_keys]
                smeta_ref[c, :, q0:q0 + half] = s_t[n_keys:n_keys + META_PAD]

    def finish(buf, jd):
        meta_valid = lax.broadcasted_iota(jnp.int32, (META_PAD, half), 0) < N_META
        for c, ch in enumerate(chains):
            for q0, n_keys in halves:
                lanes = pl.ds(q0, half)
                s_t = jnp.concatenate(
                    [ch.diag_fn(buf[c, 0:n_keys, q0:q0 + half], q0),
                     jnp.where(meta_valid, smeta_ref[c, :, q0:q0 + half], NEG)], axis=0)
                vt = jnp.concatenate(
                    [vt_ref[ch.vrows, key_block(jd, n_keys)], vtmeta_ref[ch.vrows, :]], axis=1)
                _softmax_step(s_t, vt, m_ref.at[c, :, lanes], acc_ref.at[c, :, lanes])
            on_chain_done(c)

    def pair(t):
        j = 2 * t
        scores_into(buf_b, j + 1)
        consume(buf_a, j)
        scores_into(buf_a, j + 2)
        consume(buf_b, j + 1)

    def odd_tail(n):
        diagonal_scores_into(buf_b, n)
        consume(buf_a, n - 1)
        finish(buf_b, n)

    def even_tail(n):
        scores_into(buf_b, n - 1)
        consume(buf_a, n - 2)
        diagonal_scores_into(buf_a, n)
        consume(buf_b, n - 1)
        finish(buf_a, n)

    @pl.when(qi == 0)
    def _():
        diagonal_scores_into(buf_a, 0)
        finish(buf_a, 0)

    @pl.when(qi == 1)
    def _():
        scores_into(buf_a, 0)
        odd_tail(1)

    @pl.when(qi == 2)
    def _():
        scores_into(buf_a, 0)
        even_tail(2)

    @pl.when(qi >= 3)
    def _():
        scores_into(buf_a, 0)
        pair(0)

        def body(t, carry):
            pair(t)
            return carry

        lax.fori_loop(1, (qi - 1) // 2, body, 0)

        @pl.when(qi % 2 == 1)
        def _():
            odd_tail(qi)

        @pl.when(qi % 2 == 0)
        def _():
            even_tail(qi)


def _attention_scratch(n_chains, tq):
    stat = pltpu.VMEM((n_chains, 1, tq), F32)
    scores = pltpu.VMEM((n_chains, ATT_TK, tq), F32)
    return [stat, pltpu.VMEM((n_chains, VT_ROWS, tq), F32), scores, scores,
            pltpu.VMEM((n_chains, META_PAD, tq), F32)]


def _da_kernel(q_ref, k_ref, posf_ref, vt_ref, kmeta_ref, posfmeta_ref, vtmeta_ref, lam_ref,
               gsub_ref, y_ref,
               qx_ref, m_ref, acc_ref, sa_ref, sb_ref, smeta_ref):
    qi = pl.program_id(1)
    tq, tk = ATT_TQ, ATT_TK
    lane = lax.broadcasted_iota(jnp.int32, (tq, LANES), 1)
    lp = lam_ref[...]
    lam = (jnp.exp(jnp.sum(lp[0:1] * lp[1:2], axis=-1, keepdims=True))
           - jnp.exp(jnp.sum(lp[2:3] * lp[3:4], axis=-1, keepdims=True)) + LAMBDA_INIT)

    for h0 in range(0, DA_HEADS, DA_HEAD_GROUP):
        chains = []
        for g in range(DA_HEAD_GROUP):
            h = h0 + g
            c, pieces = _alibi_pieces(h)
            q = q_ref[0, :, LANES * h:LANES * (h + 1)]
            pf = jnp.zeros((tq, LANES), F32)
            for i, piece in enumerate(pieces):
                pf = jnp.where((lane == 2 * i) | (lane == 2 * i + 1), piece, pf)
            pf = pf.astype(BF16)
            zero = jnp.zeros_like(q)
            qx_ref[2 * g, :, 0:LANES] = jnp.where(lane < DA_HEAD_DIM, q, zero)
            qx_ref[2 * g + 1, :, 0:LANES] = jnp.where(lane >= DA_HEAD_DIM, q, zero)
            qx_ref[2 * g, :, LANES:KEY_W] = pf
            qx_ref[2 * g + 1, :, LANES:KEY_W] = pf

            def diag_fn(s_t, q0, c=c):
                ik, iq = _diag_iotas(*s_t.shape)
                iq = iq + q0
                visible = (ik // CHUNK) <= (iq // CHUNK)
                corr = (2.0 * c) * jnp.minimum(iq - ik, 0).astype(F32)
                return jnp.where(visible, s_t + corr, NEG)

            def load_k(rows, h=h):
                return jnp.concatenate(
                    [k_ref[0, rows, LANES * h:LANES * (h + 1)], posf_ref[rows, :]], axis=1)

            def load_kmeta(h=h):
                return jnp.concatenate(
                    [kmeta_ref[:, LANES * h:LANES * (h + 1)], posfmeta_ref[...]], axis=1)

            for comp in range(2):
                chains.append(_Chain(
                    load_q=functools.partial(lambda i, rows: qx_ref[i, rows, :], 2 * g + comp),
                    load_k=load_k, load_kmeta=load_kmeta, vrows=pl.ds(VT_ROWS * h, VT_ROWS),
                    diag_fn=diag_fn))

        def finish_head(c, h0=h0):
            if c % 2 == 0:
                return
            h = h0 + c // 2
            o = (_normalised(acc_ref.at[c - 1], DA_V_DIM)
                 - lam * _normalised(acc_ref.at[c], DA_V_DIM))
            o = o * lax.rsqrt(jnp.mean(o * o, axis=0, keepdims=True) + EPS)
            o = o * gsub_ref[...] * (1.0 - LAMBDA_INIT)
            y_ref[0, :, LANES * h:LANES * (h + 1)] = o.T.astype(BF16)

        _attend(qi, chains, vtmeta_ref, vt_ref, (sa_ref, sb_ref), smeta_ref, m_ref, acc_ref,
                finish_head)


def _da_call(q, k, posf, vt, kmeta, posf_meta, vtmeta, lam_params, g_sub_col):
    nb, s, _ = q.shape
    tq = ATT_TQ
    n_chains = 2 * DA_HEAD_GROUP
    return pl.pallas_call(
        _da_kernel,
        grid=(nb, s // tq),
        in_specs=[
            pl.BlockSpec((1, tq, DA_WIDTH), lambda b, i: (b, i, 0)),
            pl.BlockSpec((1, s, DA_WIDTH), lambda b, i: (b, 0, 0)),
            pl.BlockSpec((s, LANES), lambda b, i: (0, 0), pipeline_mode=pl.Buffered(1)),
            pl.BlockSpec((DA_HEADS * VT_ROWS, s), lambda b, i: (0, b),
                         pipeline_mode=pl.Buffered(1)),
            pl.BlockSpec((META_PAD, DA_WIDTH), lambda b, i: (0, 0)),
            pl.BlockSpec((META_PAD, LANES), lambda b, i: (0, 0)),
            pl.BlockSpec((DA_HEADS * VT_ROWS, META_PAD), lambda b, i: (0, 0)),
            pl.BlockSpec((8, LANES), lambda b, i: (0, 0)),
            pl.BlockSpec((DA_V_DIM, 1), lambda b, i: (0, 0)),
        ],
        out_specs=pl.BlockSpec((1, tq, DA_WIDTH), lambda b, i: (b, i, 0)),
        out_shape=jax.ShapeDtypeStruct((nb, s, DA_WIDTH), BF16),
        scratch_shapes=[pltpu.VMEM((n_chains, tq, KEY_W), BF16)] + _attention_scratch(n_chains, tq),
        compiler_params=pltpu.CompilerParams(
            dimension_semantics=("arbitrary", "arbitrary"), vmem_limit_bytes=VMEM_LIMIT),
        name="diff_attention",
    )(q, k, posf, vt, kmeta, posf_meta, vtmeta, lam_params, g_sub_col)


def _mla_kernel(q_ref, k_ref, vt_ref, kmeta_ref, vtmeta_ref, y_ref, m_ref, acc_ref,
                sa_ref, sb_ref, smeta_ref):
    qi = pl.program_id(1)
    tq, tk = ATT_TQ, ATT_TK

    def diag_fn(s_t, q0):
        ik, iq = _diag_iotas(*s_t.shape)
        return jnp.where((ik // CHUNK) <= ((iq + q0) // CHUNK), s_t, NEG)

    for h0 in range(0, MLA_HEADS, MLA_HEAD_GROUP):
        heads = range(h0, h0 + MLA_HEAD_GROUP)
        chains = [
            _Chain(load_q=functools.partial(
                       lambda h, rows: q_ref[0, rows, KEY_W * h:KEY_W * (h + 1)], h),
                   load_k=functools.partial(
                       lambda h, rows: k_ref[0, rows, KEY_W * h:KEY_W * (h + 1)], h),
                   load_kmeta=functools.partial(
                       lambda h: kmeta_ref[:, KEY_W * h:KEY_W * (h + 1)], h),
                   vrows=pl.ds(VT_ROWS * h, VT_ROWS), diag_fn=diag_fn)
            for h in heads]
        def finish_head(c, h0=h0):
            h = h0 + c
            y_ref[0, :, LANES * h:LANES * (h + 1)] = (
                _normalised(acc_ref.at[c], MLA_V).T.astype(BF16))

        _attend(qi, chains, vtmeta_ref, vt_ref, (sa_ref, sb_ref), smeta_ref, m_ref, acc_ref,
                finish_head)


def _mla_call(q, k, vt, kmeta, vtmeta):
    nb, s, _ = q.shape
    tq = ATT_TQ
    return pl.pallas_call(
        _mla_kernel,
        grid=(nb, s // tq),
        in_specs=[
            pl.BlockSpec((1, tq, MLA_HEADS * KEY_W), lambda b, i: (b, i, 0)),
            pl.BlockSpec((1, s, MLA_HEADS * KEY_W), lambda b, i: (b, 0, 0)),
            pl.BlockSpec((MLA_HEADS * VT_ROWS, s), lambda b, i: (0, b)),
            pl.BlockSpec((META_PAD, MLA_HEADS * KEY_W), lambda b, i: (0, 0)),
            pl.BlockSpec((MLA_HEADS * VT_ROWS, META_PAD), lambda b, i: (0, 0)),
        ],
        out_specs=pl.BlockSpec((1, tq, MLA_WIDTH), lambda b, i: (b, i, 0)),
        out_shape=jax.ShapeDtypeStruct((nb, s, MLA_WIDTH), BF16),
        scratch_shapes=_attention_scratch(MLA_HEAD_GROUP, tq),
        compiler_params=pltpu.CompilerParams(
            dimension_semantics=("arbitrary", "arbitrary"), vmem_limit_bytes=VMEM_LIMIT),
        name="mla_attention",
    )(q, k, vt, kmeta, vtmeta)


def _out_kernel(x_ref, yda_ref, ymla_ref, wo_ref, gpost_ref, gpre_ref, w1_ref, w2_ref,
                gmlp_ref, o_ref):
    tm = x_ref.shape[0]
    groups = [slice(r, r + tm // OUT_ROW_GROUPS) for r in range(0, tm, tm // OUT_ROW_GROUPS)]
    mix = [_dot(yda_ref[r, :], wo_ref[0:DA_WIDTH, :])
           + _dot(ymla_ref[r, :], wo_ref[DA_WIDTH:DA_WIDTH + MLA_WIDTH, :]) for r in groups]
    h1 = [x_ref[r, :] + _rms_scale(m) * gpost_ref[...] for r, m in zip(groups, mix)]
    u = [(_rms_scale(h) * gpre_ref[...]).astype(BF16) for h in h1]
    for r, h, ug in zip(groups, h1, u):
        f = None
        for c in range(D_FF // FF_CHUNK):
            cols = slice(FF_CHUNK * c, FF_CHUNK * (c + 1))
            hid = jnp.square(jnp.maximum(_dot(ug, w1_ref[:, cols]), 0.0)).astype(BF16)
            part = _dot(hid, w2_ref[cols, :])
            f = part if f is None else f + part
        o_ref[r, :] = h + _rms_scale(f) * gmlp_ref[...]


def _out_call(x2d, yda, ymla, w_o, g_post, g_pre, w1, w2, g_mlp):
    rows = x2d.shape[0]
    tm = OUT_TM
    row = lambda i: (i, 0)
    const = lambda i: (0, 0)
    single = pl.Buffered(1)
    return pl.pallas_call(
        _out_kernel,
        grid=(rows // tm,),
        in_specs=[
            pl.BlockSpec((tm, D_MODEL), row),
            pl.BlockSpec((tm, DA_WIDTH), row),
            pl.BlockSpec((tm, MLA_WIDTH), row),
            pl.BlockSpec((D_MODEL, D_MODEL), const, pipeline_mode=single),
            pl.BlockSpec((1, D_MODEL), const),
            pl.BlockSpec((1, D_MODEL), const),
            pl.BlockSpec((D_MODEL, D_FF), const, pipeline_mode=single),
            pl.BlockSpec((D_FF, D_MODEL), const, pipeline_mode=single),
            pl.BlockSpec((1, D_MODEL), const),
        ],
        out_specs=pl.BlockSpec((tm, D_MODEL), row),
        out_shape=jax.ShapeDtypeStruct((rows, D_MODEL), F32),
        compiler_params=pltpu.CompilerParams(
            dimension_semantics=("arbitrary",), vmem_limit_bytes=VMEM_LIMIT),
        name="out_mlp",
    )(x2d, yda, ymla, w_o, g_post, g_pre, w1, w2, g_mlp)


def _position_tables(n_pos):
    inv_freq = 1.0 / (ROPE_THETA ** (np.arange(0, MLA_ROPE, 2, dtype=np.float64) / MLA_ROPE))
    ang = np.arange(n_pos, dtype=np.float64)[:, None] * inv_freq[None, :]
    cos, sin = np.cos(ang), np.sin(ang)
    ct = np.concatenate([cos, cos] * (LANES // MLA_ROPE), axis=1).astype(np.float32)
    st = np.concatenate([-sin, sin] * (LANES // MLA_ROPE), axis=1).astype(np.float32)
    ipos = np.arange(n_pos)
    hi = ((ipos // CHUNK) * CHUNK)[:, None]
    lo = (ipos % CHUNK)[:, None]
    posf = np.concatenate([hi, lo] * ALIBI_PIECES
                          + [np.zeros((n_pos, LANES - 2 * ALIBI_PIECES), np.int64)], axis=1)
    return ct, st, posf.astype(ml_dtypes.bfloat16)


def kernel(x, meta_tokens, g_attn_pre, w_in, da_lambda_q1, da_lambda_k1, da_lambda_q2,
           da_lambda_k2, g_da_sub, g_mla_q, w_mla_q_up, g_mla_kv, w_mla_kv_up, w_o,
           g_attn_post, g_mlp_pre, w_ff1, w_ff2, g_mlp_post):
    nb, seq, d = x.shape
    assert d == D_MODEL and w_in.shape[0] == 1, "single-layer block only"
    assert seq % ATT_TQ == 0 and seq % PROJ_TM == 0 and ATT_TQ == ATT_TK

    w_in0 = w_in[0].astype(BF16)
    n_da = 2 * DA_WIDTH
    v_cols = slice(n_da, n_da + DA_WIDTH)
    rest = w_in0[:, n_da + DA_WIDTH:]
    kr = rest[:, MLA_Q_RANK + MLA_KV_RANK:]
    zpad = jnp.zeros((D_MODEL, LANES - MLA_ROPE), BF16)
    w_in_ext = jnp.concatenate(
        [w_in0[:, :n_da], rest[:, :MLA_Q_RANK + MLA_KV_RANK], kr, zpad], axis=1)
    w_vt = w_in0[:, v_cols].T
    wq = w_mla_q_up[0].astype(BF16).reshape(MLA_Q_RANK, MLA_HEADS, MLA_NOPE + MLA_ROPE)
    wq_ext = jnp.concatenate(
        [wq[..., :MLA_NOPE].reshape(MLA_Q_RANK, Q_ROPE0),
         wq[..., MLA_NOPE:].reshape(MLA_Q_RANK, MLA_HEADS * MLA_ROPE)], axis=1)
    wkv = w_mla_kv_up[0].astype(BF16).reshape(MLA_KV_RANK, MLA_HEADS, MLA_NOPE + MLA_V)
    wkv_k = wkv[..., :MLA_NOPE].reshape(MLA_KV_RANK, MLA_HEADS * MLA_NOPE)
    wkv_vt = wkv[..., MLA_NOPE:].reshape(MLA_KV_RANK, MLA_WIDTH).T

    ct, st, posf = _position_tables(N_META + seq)
    lam_rows = jnp.concatenate(
        [da_lambda_q1, da_lambda_k1, da_lambda_q2, da_lambda_k2], axis=0).astype(F32)
    lam_params = jnp.pad(lam_rows, ((0, 8 - lam_rows.shape[0]), (0, LANES - DA_HEAD_DIM)))

    proj = functools.partial(
        _proj_call, g_pre=g_attn_pre, w_in=w_in_ext, w_vt=w_vt, g_q=g_mla_q, w_q=wq_ext,
        g_kv=g_mla_kv, w_kv=wkv_k, w_kvt=wkv_vt)
    x2d = x.reshape(nb * seq, D_MODEL)
    qda, kda, vtda, qm, km, vtm = proj(
        x2d, PROJ_TM, seq // PROJ_TM, ct=ct[N_META:], st=st[N_META:], name="proj_tokens")
    meta_rows = jnp.pad(meta_tokens.astype(F32), ((0, META_PAD - N_META), (0, 0)))
    _, kda_meta, vtda_meta, _, km_meta, vtm_meta = proj(
        meta_rows, META_PAD, 1, ct=ct[:META_PAD], st=st[:META_PAD], name="proj_meta")

    def per_batch(a):
        return a.reshape(nb, seq, a.shape[-1])

    yda = _da_call(per_batch(qda), per_batch(kda), posf[N_META:], vtda, kda_meta, posf[:META_PAD],
                   vtda_meta, lam_params, g_da_sub.reshape(DA_V_DIM, 1))
    ymla = _mla_call(per_batch(qm), per_batch(km), vtm, km_meta, vtm_meta)

    out = _out_call(x2d, yda.reshape(nb * seq, DA_WIDTH), ymla.reshape(nb * seq, MLA_WIDTH),
                    w_o[0].astype(BF16), g_attn_post, g_mlp_pre,
                    w_ff1[0].astype(BF16), w_ff2[0].astype(BF16), g_mlp_post)
    return out.reshape(nb, seq, D_MODEL)
```

```python
import functools
import math
from typing import Any, Callable, NamedTuple

import jax
import jax.numpy as jnp
import ml_dtypes
import numpy as np
from jax import lax
from jax.experimental import pallas as pl
from jax.experimental.pallas import tpu as pltpu

F32 = jnp.float32
BF16 = jnp.bfloat16

D_MODEL = 1024
N_META = 16
CHUNK = 64
EPS = 1e-6
NEG = -1e30
ROPE_THETA = 10000.0
LOG2E = math.log2(math.e)

DA_HEADS = 4
DA_HEAD_DIM = 64
DA_V_DIM = 128
DA_WIDTH = DA_HEADS * DA_V_DIM
MLA_HEADS = 4
MLA_NOPE = 128
MLA_ROPE = 64
MLA_V = 128
MLA_WIDTH = MLA_HEADS * MLA_V
MLA_Q_RANK = 256
MLA_KV_RANK = 128
D_FF = 4 * D_MODEL
LAMBDA_INIT = 0.8 - 0.6 * math.exp(-0.3 * 0)

LANES = 128
BF16_SUBLANES = 16
KEY_W = 2 * LANES
VT_ROWS = DA_V_DIM + BF16_SUBLANES
VMEM_LIMIT = 61 * 1024 * 1024

PROJ_TM = 1024
PROJ_ROW_GROUPS = 4
ATT_TQ = 512
ATT_TK = 512
OUT_TM = 1024
FF_CHUNK = 1024
OUT_ROW_GROUPS = 4
META_PAD = 128
DA_HEAD_GROUP = 4
MLA_HEAD_GROUP = 4

C_QDA, C_KDA, C_CQ, C_CKV, C_KR, C_END = (0, 512, 1024, 1280, 1408, 1536)
Q_ROPE0 = MLA_HEADS * MLA_NOPE
Q_UP_W = Q_ROPE0 + MLA_HEADS * MLA_ROPE
ALIBI_PIECES = 3


def _rms_scale(x):
    return x * lax.rsqrt(jnp.mean(x * x, axis=-1, keepdims=True) + EPS)


def _dot(a, b):
    return jnp.dot(a, b, preferred_element_type=F32)


def _dot_nt(a, b):
    return lax.dot_general(a, b, (((1,), (1,)), ((), ())), preferred_element_type=F32)


def _alibi_pieces(h):
    c = np.float32(2.0 ** (-8.0 * (h + 1) / DA_HEADS)) * np.float32(LOG2E)
    pieces, rest = [], c
    for _ in range(ALIBI_PIECES):
        piece = np.float32(ml_dtypes.bfloat16(rest))
        pieces.append(float(piece))
        rest = np.float32(rest - piece)
    assert rest == 0.0
    return float(c), pieces


def _proj_kernel(x_ref, g_ref, win_ref, wvt_ref, gq_ref, wq_ref, gkv_ref, wkv_ref, wkvt_ref,
                 ct_ref, st_ref,
                 qda_ref, kda_ref, vtda_ref, qm_ref, km_ref, vtm_ref):
    tm = x_ref.shape[0]
    n_groups = PROJ_ROW_GROUPS if tm % (PROJ_ROW_GROUPS * LANES) == 0 else 1
    groups = [slice(r, r + tm // n_groups) for r in range(0, tm, tm // n_groups)]
    ones = jnp.ones((BF16_SUBLANES, tm // n_groups), BF16)
    scale = (MLA_NOPE + MLA_ROPE) ** -0.5 * LOG2E

    u = [(_rms_scale(x_ref[r, :]) * g_ref[...]).astype(BF16) for r in groups]
    p = [_dot(ug, win_ref[...]) for ug in u]
    vt = [_dot_nt(wvt_ref[...], ug) for ug in u]

    for r, pg, vtg in zip(groups, p, vt):
        qda_ref[r, :] = (pg[:, C_QDA:C_KDA] * (DA_HEAD_DIM ** -0.5 * LOG2E)).astype(BF16)
        kda_ref[r, :] = pg[:, C_KDA:C_CQ].astype(BF16)
        for h in range(DA_HEADS):
            vtda_ref[VT_ROWS * h:VT_ROWS * h + DA_V_DIM, r] = (
                vtg[DA_V_DIM * h:DA_V_DIM * (h + 1), :].astype(BF16))
            vtda_ref[VT_ROWS * h + DA_V_DIM:VT_ROWS * (h + 1), r] = ones

    cq = [(_rms_scale(pg[:, C_CQ:C_CKV]) * gq_ref[...]).astype(BF16) for pg in p]
    ckv = [(_rms_scale(pg[:, C_CKV:C_KR]) * gkv_ref[...]).astype(BF16) for pg in p]
    qu = [_dot(c, wq_ref[...]) for c in cq]
    kn = [_dot(c, wkv_ref[...]) for c in ckv]
    vtm = [_dot_nt(wkvt_ref[...], c) for c in ckv]
    lane = lax.broadcasted_iota(jnp.int32, (tm // n_groups, LANES), 1)
    first_half = (lane % MLA_ROPE) < MLA_ROPE // 2
    low = lane < MLA_ROPE

    def rope(xr, ct, st):
        swapped = jnp.where(first_half, pltpu.roll(xr, LANES - MLA_ROPE // 2, 1),
                            pltpu.roll(xr, MLA_ROPE // 2, 1))
        return xr * ct + swapped * st

    for r, pg, qug, kng, vtmg in zip(groups, p, qu, kn, vtm):
        ct = ct_ref[r, :]
        st = st_ref[r, :]
        k_rope = jnp.where(low, rope(pg[:, C_KR:C_END], ct, st), 0.0).astype(BF16)
        for t in range(MLA_HEADS // 2):
            roped = rope(qug[:, Q_ROPE0 + LANES * t:Q_ROPE0 + LANES * (t + 1)], ct, st) * scale
            for h, part in ((2 * t, roped), (2 * t + 1, pltpu.roll(roped, MLA_ROPE, 1))):
                qm_ref[r, KEY_W * h + LANES:KEY_W * (h + 1)] = (
                    jnp.where(low, part, 0.0).astype(BF16))
        for h in range(MLA_HEADS):
            qm_ref[r, KEY_W * h:KEY_W * h + LANES] = (
                qug[:, MLA_NOPE * h:MLA_NOPE * (h + 1)] * scale).astype(BF16)
            km_ref[r, KEY_W * h:KEY_W * h + LANES] = kng[:, LANES * h:LANES * (h + 1)].astype(BF16)
            km_ref[r, KEY_W * h + LANES:KEY_W * (h + 1)] = k_rope
            vtm_ref[VT_ROWS * h:VT_ROWS * h + MLA_V, r] = (
                vtmg[MLA_V * h:MLA_V * (h + 1), :].astype(BF16))
            vtm_ref[VT_ROWS * h + MLA_V:VT_ROWS * (h + 1), r] = ones


def _proj_call(x2d, tm, n_tab_blocks, g_pre, w_in, w_vt, g_q, w_q, g_kv, w_kv, w_kvt,
               ct, st, name):
    rows = x2d.shape[0]
    grid = (rows // tm,)
    row = lambda i: (i, 0)
    col = lambda i: (0, i)
    const = lambda i: (0, 0)
    tab = lambda i: (i % n_tab_blocks, 0)
    row_out = lambda w: (pl.BlockSpec((tm, w), row), jax.ShapeDtypeStruct((rows, w), BF16))
    col_out = lambda r: (pl.BlockSpec((r, tm), col), jax.ShapeDtypeStruct((r, rows), BF16))
    outs = [row_out(DA_WIDTH), row_out(DA_WIDTH), col_out(DA_HEADS * VT_ROWS),
            row_out(MLA_HEADS * KEY_W), row_out(MLA_HEADS * KEY_W), col_out(MLA_HEADS * VT_ROWS)]
    return pl.pallas_call(
        _proj_kernel,
        grid=grid,
        in_specs=[
            pl.BlockSpec((tm, D_MODEL), row),
            pl.BlockSpec((1, D_MODEL), const),
            pl.BlockSpec((D_MODEL, C_END), const),
            pl.BlockSpec((DA_WIDTH, D_MODEL), const),
            pl.BlockSpec((1, MLA_Q_RANK), const),
            pl.BlockSpec((MLA_Q_RANK, Q_UP_W), const),
            pl.BlockSpec((1, MLA_KV_RANK), const),
            pl.BlockSpec((MLA_KV_RANK, MLA_HEADS * MLA_NOPE), const),
            pl.BlockSpec((MLA_WIDTH, MLA_KV_RANK), const),
            pl.BlockSpec((tm, LANES), tab),
            pl.BlockSpec((tm, LANES), tab),
        ],
        out_specs=[o[0] for o in outs],
        out_shape=[o[1] for o in outs],
        compiler_params=pltpu.CompilerParams(
            dimension_semantics=("arbitrary",), vmem_limit_bytes=VMEM_LIMIT),
        name=name,
    )(x2d, g_pre, w_in, w_vt, g_q, w_q, g_kv, w_kv, w_kvt, ct, st)


def _softmax_step(s_t, vt, m_ref, acc_ref):
    m_prev = m_ref[...]
    m_new = jnp.maximum(m_prev, jnp.max(s_t, axis=0, keepdims=True))
    p_t = jnp.exp2(s_t - m_new).astype(BF16)
    acc_ref[...] = jnp.exp2(m_prev - m_new) * acc_ref[...] + _dot(vt, p_t)
    m_ref[...] = m_new


def _diag_iotas(tk, tq):
    ik = lax.broadcasted_iota(jnp.int32, (tk, tq), 0)
    iq = lax.broadcasted_iota(jnp.int32, (tk, tq), 1)
    return ik, iq


def _normalised(acc_ref, dv):
    return acc_ref[0:dv, :] / acc_ref[dv:dv + 1, :]


class _Chain(NamedTuple):
    load_q: Callable[[Any], jax.Array]
    load_k: Callable[[Any], jax.Array]
    load_kmeta: Callable[[], jax.Array]
    vrows: Any
    diag_fn: Callable[[jax.Array, int], jax.Array]


def _attend(qi, chains, vtmeta_ref, vt_ref, s_bufs, smeta_ref, m_ref, acc_ref,
            on_chain_done):
    tk, tq = ATT_TK, ATT_TQ
    half = tq // 2
    buf_a, buf_b = s_bufs
    halves = ((0, half), (half, tk))

    m_ref[...] = jnp.full(m_ref.shape, NEG, F32)
    acc_ref[...] = jnp.zeros(acc_ref.shape, F32)

    def key_block(j, size=tk):
        start = j * tk
        return pl.ds(start if isinstance(j, int) else pl.multiple_of(start, tk), size)

    def scores_into(buf, j):
        for c, ch in enumerate(chains):
            buf[c] = _dot_nt(ch.load_k(key_block(j)), ch.load_q(slice(None)))

    def consume(buf, j):
        for c, ch in enumerate(chains):
            _softmax_step(buf[c], vt_ref[ch.vrows, key_block(j)], m_ref.at[c], acc_ref.at[c])

    def diagonal_scores_into(buf, jd):
        for c, ch in enumerate(chains):
            kmeta = ch.load_kmeta()
            for q0, n_keys in halves:
                keys = jnp.concatenate([ch.load_k(key_block(jd, n_keys)), kmeta], axis=0)
                s_t = _dot_nt(keys, ch.load_q(slice(q0, q0 + half)))
                buf[c, 0:n_keys, q0:q0 + half] = s_t[0:n_keys]
                smeta_ref[c, :, q0:q0 + half] = s_t[n_keys:n_keys + META_PAD]

    def finish(buf, jd):
        meta_valid = lax.broadcasted_iota(jnp.int32, (META_PAD, half), 0) < N_META
        for c, ch in enumerate(chains):
            for q0, n_keys in halves:
                lanes = pl.ds(q0, half)
                s_t = jnp.concatenate(
                    [ch.diag_fn(buf[c, 0:n_keys, q0:q0 + half], q0),
                     jnp.where(meta_valid, smeta_ref[c, :, q0:q0 + half], NEG)], axis=0)
                vt = jnp.concatenate(
                    [vt_ref[ch.vrows, key_block(jd, n_keys)], vtmeta_ref[ch.vrows, :]], axis=1)
                _softmax_step(s_t, vt, m_ref.at[c, :, lanes], acc_ref.at[c, :, lanes])
            on_chain_done(c)

    def pair(t):
        j = 2 * t
        scores_into(buf_b, j + 1)
        consume(buf_a, j)
        scores_into(buf_a, j + 2)
        consume(buf_b, j + 1)

    def odd_tail(n):
        diagonal_scores_into(buf_b, n)
        consume(buf_a, n - 1)
        finish(buf_b, n)

    def even_tail(n):
        scores_into(buf_b, n - 1)
        consume(buf_a, n - 2)
        diagonal_scores_into(buf_a, n)
        consume(buf_b, n - 1)
        finish(buf_a, n)

    @pl.when(qi == 0)
    def _():
        diagonal_scores_into(buf_a, 0)
        finish(buf_a, 0)

    @pl.when(qi == 1)
    def _():
        scores_into(buf_a, 0)
        odd_tail(1)

    @pl.when(qi == 2)
    def _():
        scores_into(buf_a, 0)
        even_tail(2)

    @pl.when(qi >= 3)
    def _():
        scores_into(buf_a, 0)
        pair(0)

        def body(t, carry):
            pair(t)
            return carry

        lax.fori_loop(1, (qi - 1) // 2, body, 0)

        @pl.when(qi % 2 == 1)
        def _():
            odd_tail(qi)

        @pl.when(qi % 2 == 0)
        def _():
            even_tail(qi)


def _attention_scratch(n_chains, tq):
    stat = pltpu.VMEM((n_chains, 1, tq), F32)
    scores = pltpu.VMEM((n_chains, ATT_TK, tq), F32)
    return [stat, pltpu.VMEM((n_chains, VT_ROWS, tq), F32), scores, scores,
            pltpu.VMEM((n_chains, META_PAD, tq), F32)]


def _da_kernel(q_ref, k_ref, posf_ref, vt_ref, kmeta_ref, posfmeta_ref, vtmeta_ref, lam_ref,
               gsub_ref, y_ref,
               qx_ref, m_ref, acc_ref, sa_ref, sb_ref, smeta_ref):
    qi = pl.program_id(1)
    tq, tk = ATT_TQ, ATT_TK
    lane = lax.broadcasted_iota(jnp.int32, (tq, LANES), 1)
    lp = lam_ref[...]
    lam = (jnp.exp(jnp.sum(lp[0:1] * lp[1:2], axis=-1, keepdims=True))
           - jnp.exp(jnp.sum(lp[2:3] * lp[3:4], axis=-1, keepdims=True)) + LAMBDA_INIT)

    for h0 in range(0, DA_HEADS, DA_HEAD_GROUP):
        chains = []
        for g in range(DA_HEAD_GROUP):
            h = h0 + g
            c, pieces = _alibi_pieces(h)
            q = q_ref[0, :, LANES * h:LANES * (h + 1)]
            pf = jnp.zeros((tq, LANES), F32)
            for i, piece in enumerate(pieces):
                pf = jnp.where((lane == 2 * i) | (lane == 2 * i + 1), piece, pf)
            pf = pf.astype(BF16)
            zero = jnp.zeros_like(q)
            qx_ref[2 * g, :, 0:LANES] = jnp.where(lane < DA_HEAD_DIM, q, zero)
            qx_ref[2 * g + 1, :, 0:LANES] = jnp.where(lane >= DA_HEAD_DIM, q, zero)
            qx_ref[2 * g, :, LANES:KEY_W] = pf
            qx_ref[2 * g + 1, :, LANES:KEY_W] = pf

            def diag_fn(s_t, q0, c=c):
                ik, iq = _diag_iotas(*s_t.shape)
                iq = iq + q0
                visible = (ik // CHUNK) <= (iq // CHUNK)
                corr = (2.0 * c) * jnp.minimum(iq - ik, 0).astype(F32)
                return jnp.where(visible, s_t + corr, NEG)

            def load_k(rows, h=h):
                return jnp.concatenate(
                    [k_ref[0, rows, LANES * h:LANES * (h + 1)], posf_ref[rows, :]], axis=1)

            def load_kmeta(h=h):
                return jnp.concatenate(
                    [kmeta_ref[:, LANES * h:LANES * (h + 1)], posfmeta_ref[...]], axis=1)

            for comp in range(2):
                chains.append(_Chain(
                    load_q=functools.partial(lambda i, rows: qx_ref[i, rows, :], 2 * g + comp),
                    load_k=load_k, load_kmeta=load_kmeta, vrows=pl.ds(VT_ROWS * h, VT_ROWS),
                    diag_fn=diag_fn))

        def finish_head(c, h0=h0):
            if c % 2 == 0:
                return
            h = h0 + c // 2
            o = (_normalised(acc_ref.at[c - 1], DA_V_DIM)
                 - lam * _normalised(acc_ref.at[c], DA_V_DIM))
            o = o * lax.rsqrt(jnp.mean(o * o, axis=0, keepdims=True) + EPS)
            o = o * gsub_ref[...] * (1.0 - LAMBDA_INIT)
            y_ref[0, :, LANES * h:LANES * (h + 1)] = o.T.astype(BF16)

        _attend(qi, chains, vtmeta_ref, vt_ref, (sa_ref, sb_ref), smeta_ref, m_ref, acc_ref,
                finish_head)


def _da_call(q, k, posf, vt, kmeta, posf_meta, vtmeta, lam_params, g_sub_col):
    nb, s, _ = q.shape
    tq = ATT_TQ
    n_chains = 2 * DA_HEAD_GROUP
    return pl.pallas_call(
        _da_kernel,
        grid=(nb, s // tq),
        in_specs=[
            pl.BlockSpec((1, tq, DA_WIDTH), lambda b, i: (b, i, 0)),
            pl.BlockSpec((1, s, DA_WIDTH), lambda b, i: (b, 0, 0)),
            pl.BlockSpec((s, LANES), lambda b, i: (0, 0), pipeline_mode=pl.Buffered(1)),
            pl.BlockSpec((DA_HEADS * VT_ROWS, s), lambda b, i: (0, b)),
            pl.BlockSpec((META_PAD, DA_WIDTH), lambda b, i: (0, 0)),
            pl.BlockSpec((META_PAD, LANES), lambda b, i: (0, 0)),
            pl.BlockSpec((DA_HEADS * VT_ROWS, META_PAD), lambda b, i: (0, 0)),
            pl.BlockSpec((8, LANES), lambda b, i: (0, 0)),
            pl.BlockSpec((DA_V_DIM, 1), lambda b, i: (0, 0)),
        ],
        out_specs=pl.BlockSpec((1, tq, DA_WIDTH), lambda b, i: (b, i, 0)),
        out_shape=jax.ShapeDtypeStruct((nb, s, DA_WIDTH), BF16),
        scratch_shapes=[pltpu.VMEM((n_chains, tq, KEY_W), BF16)] + _attention_scratch(n_chains, tq),
        compiler_params=pltpu.CompilerParams(
            dimension_semantics=("arbitrary", "arbitrary"), vmem_limit_bytes=VMEM_LIMIT),
        name="diff_attention",
    )(q, k, posf, vt, kmeta, posf_meta, vtmeta, lam_params, g_sub_col)


def _mla_kernel(q_ref, k_ref, vt_ref, kmeta_ref, vtmeta_ref, y_ref, m_ref, acc_ref,
                sa_ref, sb_ref, smeta_ref):
    qi = pl.program_id(1)
    tq, tk = ATT_TQ, ATT_TK

    def diag_fn(s_t, q0):
        ik, iq = _diag_iotas(*s_t.shape)
        return jnp.where((ik // CHUNK) <= ((iq + q0) // CHUNK), s_t, NEG)

    for h0 in range(0, MLA_HEADS, MLA_HEAD_GROUP):
        heads = range(h0, h0 + MLA_HEAD_GROUP)
        chains = [
            _Chain(load_q=functools.partial(
                       lambda h, rows: q_ref[0, rows, KEY_W * h:KEY_W * (h + 1)], h),
                   load_k=functools.partial(
                       lambda h, rows: k_ref[0, rows, KEY_W * h:KEY_W * (h + 1)], h),
                   load_kmeta=functools.partial(
                       lambda h: kmeta_ref[:, KEY_W * h:KEY_W * (h + 1)], h),
                   vrows=pl.ds(VT_ROWS * h, VT_ROWS), diag_fn=diag_fn)
            for h in heads]
        def finish_head(c, h0=h0):
            h = h0 + c
            y_ref[0, :, LANES * h:LANES * (h + 1)] = (
                _normalised(acc_ref.at[c], MLA_V).T.astype(BF16))

        _attend(qi, chains, vtmeta_ref, vt_ref, (sa_ref, sb_ref), smeta_ref, m_ref, acc_ref,
                finish_head)


def _mla_call(q, k, vt, kmeta, vtmeta):
    nb, s, _ = q.shape
    tq = ATT_TQ
    return pl.pallas_call(
        _mla_kernel,
        grid=(nb, s // tq),
        in_specs=[
            pl.BlockSpec((1, tq, MLA_HEADS * KEY_W), lambda b, i: (b, i, 0)),
            pl.BlockSpec((1, s, MLA_HEADS * KEY_W), lambda b, i: (b, 0, 0)),
            pl.BlockSpec((MLA_HEADS * VT_ROWS, s), lambda b, i: (0, b)),
            pl.BlockSpec((META_PAD, MLA_HEADS * KEY_W), lambda b, i: (0, 0)),
            pl.BlockSpec((MLA_HEADS * VT_ROWS, META_PAD), lambda b, i: (0, 0)),
        ],
        out_specs=pl.BlockSpec((1, tq, MLA_WIDTH), lambda b, i: (b, i, 0)),
        out_shape=jax.ShapeDtypeStruct((nb, s, MLA_WIDTH), BF16),
        scratch_shapes=_attention_scratch(MLA_HEAD_GROUP, tq),
        compiler_params=pltpu.CompilerParams(
            dimension_semantics=("arbitrary", "arbitrary"), vmem_limit_bytes=VMEM_LIMIT),
        name="mla_attention",
    )(q, k, vt, kmeta, vtmeta)


def _out_kernel(x_ref, yda_ref, ymla_ref, wo_ref, gpost_ref, gpre_ref, w1_ref, w2_ref,
                gmlp_ref, o_ref):
    tm = x_ref.shape[0]
    groups = [slice(r, r + tm // OUT_ROW_GROUPS) for r in range(0, tm, tm // OUT_ROW_GROUPS)]
    mix = [_dot(yda_ref[r, :], wo_ref[0:DA_WIDTH, :])
           + _dot(ymla_ref[r, :], wo_ref[DA_WIDTH:DA_WIDTH + MLA_WIDTH, :]) for r in groups]
    h1 = [x_ref[r, :] + _rms_scale(m) * gpost_ref[...] for r, m in zip(groups, mix)]
    u = [(_rms_scale(h) * gpre_ref[...]).astype(BF16) for h in h1]
    for r, h, ug in zip(groups, h1, u):
        f = None
        for c in range(D_FF // FF_CHUNK):
            cols = slice(FF_CHUNK * c, FF_CHUNK * (c + 1))
            hid = jnp.square(jnp.maximum(_dot(ug, w1_ref[:, cols]), 0.0)).astype(BF16)
            part = _dot(hid, w2_ref[cols, :])
            f = part if f is None else f + part
        o_ref[r, :] = h + _rms_scale(f) * gmlp_ref[...]


def _out_call(x2d, yda, ymla, w_o, g_post, g_pre, w1, w2, g_mlp):
    rows = x2d.shape[0]
    tm = OUT_TM
    row = lambda i: (i, 0)
    const = lambda i: (0, 0)
    single = pl.Buffered(1)
    return pl.pallas_call(
        _out_kernel,
        grid=(rows // tm,),
        in_specs=[
            pl.BlockSpec((tm, D_MODEL), row),
            pl.BlockSpec((tm, DA_WIDTH), row),
            pl.BlockSpec((tm, MLA_WIDTH), row),
            pl.BlockSpec((D_MODEL, D_MODEL), const, pipeline_mode=single),
            pl.BlockSpec((1, D_MODEL), const),
            pl.BlockSpec((1, D_MODEL), const),
            pl.BlockSpec((D_MODEL, D_FF), const, pipeline_mode=single),
            pl.BlockSpec((D_FF, D_MODEL), const, pipeline_mode=single),
            pl.BlockSpec((1, D_MODEL), const),
        ],
        out_specs=pl.BlockSpec((tm, D_MODEL), row),
        out_shape=jax.ShapeDtypeStruct((rows, D_MODEL), F32),
        compiler_params=pltpu.CompilerParams(
            dimension_semantics=("arbitrary",), vmem_limit_bytes=VMEM_LIMIT),
        name="out_mlp",
    )(x2d, yda, ymla, w_o, g_post, g_pre, w1, w2, g_mlp)


def _position_tables(n_pos):
    inv_freq = 1.0 / (ROPE_THETA ** (np.arange(0, MLA_ROPE, 2, dtype=np.float64) / MLA_ROPE))
    ang = np.arange(n_pos, dtype=np.float64)[:, None] * inv_freq[None, :]
    cos, sin = np.cos(ang), np.sin(ang)
    ct = np.concatenate([cos, cos] * (LANES // MLA_ROPE), axis=1).astype(np.float32)
    st = np.concatenate([-sin, sin] * (LANES // MLA_ROPE), axis=1).astype(np.float32)
    ipos = np.arange(n_pos)
    hi = ((ipos // CHUNK) * CHUNK)[:, None]
    lo = (ipos % CHUNK)[:, None]
    posf = np.concatenate([hi, lo] * ALIBI_PIECES
                          + [np.zeros((n_pos, LANES - 2 * ALIBI_PIECES), np.int64)], axis=1)
    return ct, st, posf.astype(ml_dtypes.bfloat16)


def kernel(x, meta_tokens, g_attn_pre, w_in, da_lambda_q1, da_lambda_k1, da_lambda_q2,
           da_lambda_k2, g_da_sub, g_mla_q, w_mla_q_up, g_mla_kv, w_mla_kv_up, w_o,
           g_attn_post, g_mlp_pre, w_ff1, w_ff2, g_mlp_post):
    nb, seq, d = x.shape
    assert d == D_MODEL and w_in.shape[0] == 1, "single-layer block only"
    assert seq % ATT_TQ == 0 and seq % PROJ_TM == 0 and ATT_TQ == ATT_TK

    w_in0 = w_in[0].astype(BF16)
    n_da = 2 * DA_WIDTH
    v_cols = slice(n_da, n_da + DA_WIDTH)
    rest = w_in0[:, n_da + DA_WIDTH:]
    kr = rest[:, MLA_Q_RANK + MLA_KV_RANK:]
    zpad = jnp.zeros((D_MODEL, LANES - MLA_ROPE), BF16)
    w_in_ext = jnp.concatenate(
        [w_in0[:, :n_da], rest[:, :MLA_Q_RANK + MLA_KV_RANK], kr, zpad], axis=1)
    w_vt = w_in0[:, v_cols].T
    wq = w_mla_q_up[0].astype(BF16).reshape(MLA_Q_RANK, MLA_HEADS, MLA_NOPE + MLA_ROPE)
    wq_ext = jnp.concatenate(
        [wq[..., :MLA_NOPE].reshape(MLA_Q_RANK, Q_ROPE0),
         wq[..., MLA_NOPE:].reshape(MLA_Q_RANK, MLA_HEADS * MLA_ROPE)], axis=1)
    wkv = w_mla_kv_up[0].astype(BF16).reshape(MLA_KV_RANK, MLA_HEADS, MLA_NOPE + MLA_V)
    wkv_k = wkv[..., :MLA_NOPE].reshape(MLA_KV_RANK, MLA_HEADS * MLA_NOPE)
    wkv_vt = wkv[..., MLA_NOPE:].reshape(MLA_KV_RANK, MLA_WIDTH).T

    ct, st, posf = _position_tables(N_META + seq)
    lam_rows = jnp.concatenate(
        [da_lambda_q1, da_lambda_k1, da_lambda_q2, da_lambda_k2], axis=0).astype(F32)
    lam_params = jnp.pad(lam_rows, ((0, 8 - lam_rows.shape[0]), (0, LANES - DA_HEAD_DIM)))

    proj = functools.partial(
        _proj_call, g_pre=g_attn_pre, w_in=w_in_ext, w_vt=w_vt, g_q=g_mla_q, w_q=wq_ext,
        g_kv=g_mla_kv, w_kv=wkv_k, w_kvt=wkv_vt)
    x2d = x.reshape(nb * seq, D_MODEL)
    qda, kda, vtda, qm, km, vtm = proj(
        x2d, PROJ_TM, seq // PROJ_TM, ct=ct[N_META:], st=st[N_META:], name="proj_tokens")
    meta_rows = jnp.pad(meta_tokens.astype(F32), ((0, META_PAD - N_META), (0, 0)))
    _, kda_meta, vtda_meta, _, km_meta, vtm_meta = proj(
        meta_rows, META_PAD, 1, ct=ct[:META_PAD], st=st[:META_PAD], name="proj_meta")

    def per_batch(a):
        return a.reshape(nb, seq, a.shape[-1])

    yda = _da_call(per_batch(qda), per_batch(kda), posf[N_META:], vtda, kda_meta, posf[:META_PAD],
                   vtda_meta, lam_params, g_da_sub.reshape(DA_V_DIM, 1))
    ymla = _mla_call(per_batch(qm), per_batch(km), vtm, km_meta, vtm_meta)

    out = _out_call(x2d, yda.reshape(nb * seq, DA_WIDTH), ymla.reshape(nb * seq, MLA_WIDTH),
                    w_o[0].astype(BF16), g_attn_post, g_mlp_pre,
                    w_ff1[0].astype(BF16), w_ff2[0].astype(BF16), g_mlp_post)
    return out.reshape(nb, seq, D_MODEL)
```

```python
import functools
import math
from typing import Any, Callable, NamedTuple

import jax
import jax.numpy as jnp
import ml_dtypes
import numpy as np
from jax import lax
from jax.experimental import pallas as pl
from jax.experimental.pallas import tpu as pltpu

F32 = jnp.float32
BF16 = jnp.bfloat16

D_MODEL = 1024
N_META = 16
CHUNK = 64
EPS = 1e-6
NEG = -1e30
ROPE_THETA = 10000.0
LOG2E = math.log2(math.e)

DA_HEADS = 4
DA_HEAD_DIM = 64
DA_V_DIM = 128
DA_WIDTH = DA_HEADS * DA_V_DIM
MLA_HEADS = 4
MLA_NOPE = 128
MLA_ROPE = 64
MLA_V = 128
MLA_WIDTH = MLA_HEADS * MLA_V
MLA_Q_RANK = 256
MLA_KV_RANK = 128
D_FF = 4 * D_MODEL
LAMBDA_INIT = 0.8 - 0.6 * math.exp(-0.3 * 0)

LANES = 128
BF16_SUBLANES = 16
KEY_W = 2 * LANES
VT_ROWS = DA_V_DIM + BF16_SUBLANES
VMEM_LIMIT = 61 * 1024 * 1024

PROJ_TM = 1024
PROJ_ROW_GROUPS = 4
ATT_TQ = 512
ATT_TK = 512
OUT_TM = 1024
FF_CHUNK = 1024
OUT_ROW_GROUPS = 4
META_PAD = 128
DA_HEAD_GROUP = 4
MLA_HEAD_GROUP = 4

C_QDA, C_KDA, C_CQ, C_CKV, C_KR, C_END = (0, 512, 1024, 1280, 1408, 1536)
Q_ROPE0 = MLA_HEADS * MLA_NOPE
Q_UP_W = Q_ROPE0 + MLA_HEADS * MLA_ROPE
ALIBI_PIECES = 3


def _rms_scale(x):
    return x * lax.rsqrt(jnp.mean(x * x, axis=-1, keepdims=True) + EPS)


def _dot(a, b):
    return jnp.dot(a, b, preferred_element_type=F32)


def _dot_nt(a, b):
    return lax.dot_general(a, b, (((1,), (1,)), ((), ())), preferred_element_type=F32)


def _alibi_pieces(h):
    c = np.float32(2.0 ** (-8.0 * (h + 1) / DA_HEADS)) * np.float32(LOG2E)
    pieces, rest = [], c
    for _ in range(ALIBI_PIECES):
        piece = np.float32(ml_dtypes.bfloat16(rest))
        pieces.append(float(piece))
        rest = np.float32(rest - piece)
    assert rest == 0.0
    return float(c), pieces


def _proj_kernel(x_ref, g_ref, win_ref, wvt_ref, gq_ref, wq_ref, gkv_ref, wkv_ref, wkvt_ref,
                 ct_ref, st_ref,
                 qda_ref, kda_ref, vtda_ref, qm_ref, km_ref, vtm_ref):
    tm = x_ref.shape[0]
    n_groups = PROJ_ROW_GROUPS if tm % (PROJ_ROW_GROUPS * LANES) == 0 else 1
    groups = [slice(r, r + tm // n_groups) for r in range(0, tm, tm // n_groups)]
    ones = jnp.ones((BF16_SUBLANES, tm // n_groups), BF16)
    scale = (MLA_NOPE + MLA_ROPE) ** -0.5 * LOG2E

    u = [(_rms_scale(x_ref[r, :]) * g_ref[...]).astype(BF16) for r in groups]
    p = [_dot(ug, win_ref[...]) for ug in u]
    vt = [_dot_nt(wvt_ref[...], ug) for ug in u]

    for r, pg, vtg in zip(groups, p, vt):
        qda_ref[r, :] = (pg[:, C_QDA:C_KDA] * (DA_HEAD_DIM ** -0.5 * LOG2E)).astype(BF16)
        kda_ref[r, :] = pg[:, C_KDA:C_CQ].astype(BF16)
        for h in range(DA_HEADS):
            vtda_ref[VT_ROWS * h:VT_ROWS * h + DA_V_DIM, r] = (
                vtg[DA_V_DIM * h:DA_V_DIM * (h + 1), :].astype(BF16))
            vtda_ref[VT_ROWS * h + DA_V_DIM:VT_ROWS * (h + 1), r] = ones

    cq = [(_rms_scale(pg[:, C_CQ:C_CKV]) * gq_ref[...]).astype(BF16) for pg in p]
    ckv = [(_rms_scale(pg[:, C_CKV:C_KR]) * gkv_ref[...]).astype(BF16) for pg in p]
    qu = [_dot(c, wq_ref[...]) for c in cq]
    kn = [_dot(c, wkv_ref[...]) for c in ckv]
    vtm = [_dot_nt(wkvt_ref[...], c) for c in ckv]
    lane = lax.broadcasted_iota(jnp.int32, (tm // n_groups, LANES), 1)
    first_half = (lane % MLA_ROPE) < MLA_ROPE // 2
    low = lane < MLA_ROPE

    def rope(xr, ct, st):
        swapped = jnp.where(first_half, pltpu.roll(xr, LANES - MLA_ROPE // 2, 1),
                            pltpu.roll(xr, MLA_ROPE // 2, 1))
        return xr * ct + swapped * st

    for r, pg, qug, kng, vtmg in zip(groups, p, qu, kn, vtm):
        ct = ct_ref[r, :]
        st = st_ref[r, :]
        k_rope = jnp.where(low, rope(pg[:, C_KR:C_END], ct, st), 0.0).astype(BF16)
        for t in range(MLA_HEADS // 2):
            roped = rope(qug[:, Q_ROPE0 + LANES * t:Q_ROPE0 + LANES * (t + 1)], ct, st) * scale
            for h, part in ((2 * t, roped), (2 * t + 1, pltpu.roll(roped, MLA_ROPE, 1))):
                qm_ref[r, KEY_W * h + LANES:KEY_W * (h + 1)] = (
                    jnp.where(low, part, 0.0).astype(BF16))
        for h in range(MLA_HEADS):
            qm_ref[r, KEY_W * h:KEY_W * h + LANES] = (
                qug[:, MLA_NOPE * h:MLA_NOPE * (h + 1)] * scale).astype(BF16)
            km_ref[r, KEY_W * h:KEY_W * h + LANES] = kng[:, LANES * h:LANES * (h + 1)].astype(BF16)
            km_ref[r, KEY_W * h + LANES:KEY_W * (h + 1)] = k_rope
            vtm_ref[VT_ROWS * h:VT_ROWS * h + MLA_V, r] = (
                vtmg[MLA_V * h:MLA_V * (h + 1), :].astype(BF16))
            vtm_ref[VT_ROWS * h + MLA_V:VT_ROWS * (h + 1), r] = ones


def _proj_call(x2d, tm, n_tab_blocks, g_pre, w_in, w_vt, g_q, w_q, g_kv, w_kv, w_kvt,
               ct, st, name):
    rows = x2d.shape[0]
    grid = (rows // tm,)
    row = lambda i: (i, 0)
    col = lambda i: (0, i)
    const = lambda i: (0, 0)
    tab = lambda i: (i % n_tab_blocks, 0)
    row_out = lambda w: (pl.BlockSpec((tm, w), row), jax.ShapeDtypeStruct((rows, w), BF16))
    col_out = lambda r: (pl.BlockSpec((r, tm), col), jax.ShapeDtypeStruct((r, rows), BF16))
    outs = [row_out(DA_WIDTH), row_out(DA_WIDTH), col_out(DA_HEADS * VT_ROWS),
            row_out(MLA_HEADS * KEY_W), row_out(MLA_HEADS * KEY_W), col_out(MLA_HEADS * VT_ROWS)]
    return pl.pallas_call(
        _proj_kernel,
        grid=grid,
        in_specs=[
            pl.BlockSpec((tm, D_MODEL), row),
            pl.BlockSpec((1, D_MODEL), const),
            pl.BlockSpec((D_MODEL, C_END), const),
            pl.BlockSpec((DA_WIDTH, D_MODEL), const),
            pl.BlockSpec((1, MLA_Q_RANK), const),
            pl.BlockSpec((MLA_Q_RANK, Q_UP_W), const),
            pl.BlockSpec((1, MLA_KV_RANK), const),
            pl.BlockSpec((MLA_KV_RANK, MLA_HEADS * MLA_NOPE), const),
            pl.BlockSpec((MLA_WIDTH, MLA_KV_RANK), const),
            pl.BlockSpec((tm, LANES), tab),
            pl.BlockSpec((tm, LANES), tab),
        ],
        out_specs=[o[0] for o in outs],
        out_shape=[o[1] for o in outs],
        compiler_params=pltpu.CompilerParams(
            dimension_semantics=("arbitrary",), vmem_limit_bytes=VMEM_LIMIT),
        name=name,
    )(x2d, g_pre, w_in, w_vt, g_q, w_q, g_kv, w_kv, w_kvt, ct, st)


def _softmax_step(s_t, vt, m_ref, acc_ref):
    m_prev = m_ref[...]
    m_new = jnp.maximum(m_prev, jnp.max(s_t, axis=0, keepdims=True))
    p_t = jnp.exp2(s_t - m_new).astype(BF16)
    acc_ref[...] = jnp.exp2(m_prev - m_new) * acc_ref[...] + _dot(vt, p_t)
    m_ref[...] = m_new


def _diag_iotas(tk, tq):
    ik = lax.broadcasted_iota(jnp.int32, (tk, tq), 0)
    iq = lax.broadcasted_iota(jnp.int32, (tk, tq), 1)
    return ik, iq


def _normalised(acc_ref, dv):
    return acc_ref[0:dv, :] / acc_ref[dv:dv + 1, :]


class _Chain(NamedTuple):
    load_q: Callable[[Any], jax.Array]
    load_k: Callable[[Any], jax.Array]
    load_kmeta: Callable[[], jax.Array]
    vrows: Any
    diag_fn: Callable[[jax.Array, int], jax.Array]


def _attend(qi, chains, vtmeta_ref, vt_ref, s_bufs, smeta_ref, m_ref, acc_ref,
            on_chain_done, unrolled_steps=0):
    tk, tq = ATT_TK, ATT_TQ
    half = tq // 2
    buf_a, buf_b = s_bufs
    halves = ((0, half), (half, tk))

    m_ref[...] = jnp.full(m_ref.shape, NEG, F32)
    acc_ref[...] = jnp.zeros(acc_ref.shape, F32)

    def key_block(j, size=tk):
        start = j * tk
        return pl.ds(start if isinstance(j, int) else pl.multiple_of(start, tk), size)

    def scores_into(buf, j):
        for c, ch in enumerate(chains):
            buf[c] = _dot_nt(ch.load_k(key_block(j)), ch.load_q(slice(None)))

    def consume(buf, j):
        for c, ch in enumerate(chains):
            _softmax_step(buf[c], vt_ref[ch.vrows, key_block(j)], m_ref.at[c], acc_ref.at[c])

    def diagonal_scores_into(buf, jd):
        for c, ch in enumerate(chains):
            kmeta = ch.load_kmeta()
            for q0, n_keys in halves:
                keys = jnp.concatenate([ch.load_k(key_block(jd, n_keys)), kmeta], axis=0)
                s_t = _dot_nt(keys, ch.load_q(slice(q0, q0 + half)))
                buf[c, 0:n_keys, q0:q0 + half] = s_t[0:n_keys]
                smeta_ref[c, :, q0:q0 + half] = s_t[n_keys:n_keys + META_PAD]

    def finish(buf, jd):
        meta_valid = lax.broadcasted_iota(jnp.int32, (META_PAD, half), 0) < N_META
        for c, ch in enumerate(chains):
            for q0, n_keys in halves:
                lanes = pl.ds(q0, half)
                s_t = jnp.concatenate(
                    [ch.diag_fn(buf[c, 0:n_keys, q0:q0 + half], q0),
                     jnp.where(meta_valid, smeta_ref[c, :, q0:q0 + half], NEG)], axis=0)
                vt = jnp.concatenate(
                    [vt_ref[ch.vrows, key_block(jd, n_keys)], vtmeta_ref[ch.vrows, :]], axis=1)
                _softmax_step(s_t, vt, m_ref.at[c, :, lanes], acc_ref.at[c, :, lanes])
            on_chain_done(c)

    def pair(t):
        j = 2 * t
        scores_into(buf_b, j + 1)
        consume(buf_a, j)
        scores_into(buf_a, j + 2)
        consume(buf_b, j + 1)

    def odd_tail(n):
        diagonal_scores_into(buf_b, n)
        consume(buf_a, n - 1)
        finish(buf_b, n)

    def even_tail(n):
        scores_into(buf_b, n - 1)
        consume(buf_a, n - 2)
        diagonal_scores_into(buf_a, n)
        consume(buf_b, n - 1)
        finish(buf_a, n)

    def whole_step(n):
        if n == 0:
            diagonal_scores_into(buf_a, 0)
            finish(buf_a, 0)
            return
        scores_into(buf_a, 0)
        for t in range((n - 1) // 2):
            pair(t)
        (odd_tail if n % 2 else even_tail)(n)

    if unrolled_steps:
        for n in range(unrolled_steps):
            pl.when(qi == n)(functools.partial(whole_step, n))
        return

    @pl.when(qi == 0)
    def _():
        whole_step(0)

    @pl.when(qi == 1)
    def _():
        scores_into(buf_a, 0)
        odd_tail(1)

    @pl.when(qi == 2)
    def _():
        scores_into(buf_a, 0)
        even_tail(2)

    @pl.when(qi >= 3)
    def _():
        scores_into(buf_a, 0)
        pair(0)

        def body(t, carry):
            pair(t)
            return carry

        lax.fori_loop(1, (qi - 1) // 2, body, 0)

        @pl.when(qi % 2 == 1)
        def _():
            odd_tail(qi)

        @pl.when(qi % 2 == 0)
        def _():
            even_tail(qi)


def _attention_scratch(n_chains, tq):
    stat = pltpu.VMEM((n_chains, 1, tq), F32)
    scores = pltpu.VMEM((n_chains, ATT_TK, tq), F32)
    return [stat, pltpu.VMEM((n_chains, VT_ROWS, tq), F32), scores, scores,
            pltpu.VMEM((n_chains, META_PAD, tq), F32)]


def _da_kernel(q_ref, k_ref, posf_ref, vt_ref, kmeta_ref, posfmeta_ref, vtmeta_ref, lam_ref,
               gsub_ref, y_ref,
               qx_ref, m_ref, acc_ref, sa_ref, sb_ref, smeta_ref):
    qi = pl.program_id(1)
    tq, tk = ATT_TQ, ATT_TK
    lane = lax.broadcasted_iota(jnp.int32, (tq, LANES), 1)
    lp = lam_ref[...]
    lam = (jnp.exp(jnp.sum(lp[0:1] * lp[1:2], axis=-1, keepdims=True))
           - jnp.exp(jnp.sum(lp[2:3] * lp[3:4], axis=-1, keepdims=True)) + LAMBDA_INIT)

    for h0 in range(0, DA_HEADS, DA_HEAD_GROUP):
        chains = []
        for g in range(DA_HEAD_GROUP):
            h = h0 + g
            c, pieces = _alibi_pieces(h)
            q = q_ref[0, :, LANES * h:LANES * (h + 1)]
            pf = jnp.zeros((tq, LANES), F32)
            for i, piece in enumerate(pieces):
                pf = jnp.where((lane == 2 * i) | (lane == 2 * i + 1), piece, pf)
            pf = pf.astype(BF16)
            zero = jnp.zeros_like(q)
            qx_ref[2 * g, :, 0:LANES] = jnp.where(lane < DA_HEAD_DIM, q, zero)
            qx_ref[2 * g + 1, :, 0:LANES] = jnp.where(lane >= DA_HEAD_DIM, q, zero)
            qx_ref[2 * g, :, LANES:KEY_W] = pf
            qx_ref[2 * g + 1, :, LANES:KEY_W] = pf

            def diag_fn(s_t, q0, c=c):
                ik, iq = _diag_iotas(*s_t.shape)
                iq = iq + q0
                visible = (ik // CHUNK) <= (iq // CHUNK)
                corr = (2.0 * c) * jnp.minimum(iq - ik, 0).astype(F32)
                return jnp.where(visible, s_t + corr, NEG)

            def load_k(rows, h=h):
                return jnp.concatenate(
                    [k_ref[0, rows, LANES * h:LANES * (h + 1)], posf_ref[rows, :]], axis=1)

            def load_kmeta(h=h):
                return jnp.concatenate(
                    [kmeta_ref[:, LANES * h:LANES * (h + 1)], posfmeta_ref[...]], axis=1)

            for comp in range(2):
                chains.append(_Chain(
                    load_q=functools.partial(lambda i, rows: qx_ref[i, rows, :], 2 * g + comp),
                    load_k=load_k, load_kmeta=load_kmeta, vrows=pl.ds(VT_ROWS * h, VT_ROWS),
                    diag_fn=diag_fn))

        def finish_head(c, h0=h0):
            if c % 2 == 0:
                return
            h = h0 + c // 2
            o = (_normalised(acc_ref.at[c - 1], DA_V_DIM)
                 - lam * _normalised(acc_ref.at[c], DA_V_DIM))
            o = o * lax.rsqrt(jnp.mean(o * o, axis=0, keepdims=True) + EPS)
            o = o * gsub_ref[...] * (1.0 - LAMBDA_INIT)
            y_ref[0, :, LANES * h:LANES * (h + 1)] = o.T.astype(BF16)

        _attend(qi, chains, vtmeta_ref, vt_ref, (sa_ref, sb_ref), smeta_ref, m_ref, acc_ref,
                finish_head)


def _da_call(q, k, posf, vt, kmeta, posf_meta, vtmeta, lam_params, g_sub_col):
    nb, s, _ = q.shape
    tq = ATT_TQ
    n_chains = 2 * DA_HEAD_GROUP
    return pl.pallas_call(
        _da_kernel,
        grid=(nb, s // tq),
        in_specs=[
            pl.BlockSpec((1, tq, DA_WIDTH), lambda b, i: (b, i, 0)),
            pl.BlockSpec((1, s, DA_WIDTH), lambda b, i: (b, 0, 0)),
            pl.BlockSpec((s, LANES), lambda b, i: (0, 0), pipeline_mode=pl.Buffered(1)),
            pl.BlockSpec((DA_HEADS * VT_ROWS, s), lambda b, i: (0, b)),
            pl.BlockSpec((META_PAD, DA_WIDTH), lambda b, i: (0, 0)),
            pl.BlockSpec((META_PAD, LANES), lambda b, i: (0, 0)),
            pl.BlockSpec((DA_HEADS * VT_ROWS, META_PAD), lambda b, i: (0, 0)),
            pl.BlockSpec((8, LANES), lambda b, i: (0, 0)),
            pl.BlockSpec((DA_V_DIM, 1), lambda b, i: (0, 0)),
        ],
        out_specs=pl.BlockSpec((1, tq, DA_WIDTH), lambda b, i: (b, i, 0)),
        out_shape=jax.ShapeDtypeStruct((nb, s, DA_WIDTH), BF16),
        scratch_shapes=[pltpu.VMEM((n_chains, tq, KEY_W), BF16)] + _attention_scratch(n_chains, tq),
        compiler_params=pltpu.CompilerParams(
            dimension_semantics=("arbitrary", "arbitrary"), vmem_limit_bytes=VMEM_LIMIT),
        name="diff_attention",
    )(q, k, posf, vt, kmeta, posf_meta, vtmeta, lam_params, g_sub_col)


def _mla_kernel(n_q_blocks, q_ref, k_ref, vt_ref, kmeta_ref, vtmeta_ref, y_ref, m_ref, acc_ref,
                sa_ref, sb_ref, smeta_ref):
    qi = pl.program_id(1)
    tq, tk = ATT_TQ, ATT_TK

    def diag_fn(s_t, q0):
        ik, iq = _diag_iotas(*s_t.shape)
        return jnp.where((ik // CHUNK) <= ((iq + q0) // CHUNK), s_t, NEG)

    for h0 in range(0, MLA_HEADS, MLA_HEAD_GROUP):
        heads = range(h0, h0 + MLA_HEAD_GROUP)
        chains = [
            _Chain(load_q=functools.partial(
                       lambda h, rows: q_ref[0, rows, KEY_W * h:KEY_W * (h + 1)], h),
                   load_k=functools.partial(
                       lambda h, rows: k_ref[0, rows, KEY_W * h:KEY_W * (h + 1)], h),
                   load_kmeta=functools.partial(
                       lambda h: kmeta_ref[:, KEY_W * h:KEY_W * (h + 1)], h),
                   vrows=pl.ds(VT_ROWS * h, VT_ROWS), diag_fn=diag_fn)
            for h in heads]
        def finish_head(c, h0=h0):
            h = h0 + c
            y_ref[0, :, LANES * h:LANES * (h + 1)] = (
                _normalised(acc_ref.at[c], MLA_V).T.astype(BF16))

        _attend(qi, chains, vtmeta_ref, vt_ref, (sa_ref, sb_ref), smeta_ref, m_ref, acc_ref,
                finish_head, unrolled_steps=n_q_blocks)


def _mla_call(q, k, vt, kmeta, vtmeta):
    nb, s, _ = q.shape
    tq = ATT_TQ
    return pl.pallas_call(
        functools.partial(_mla_kernel, s // tq),
        grid=(nb, s // tq),
        in_specs=[
            pl.BlockSpec((1, tq, MLA_HEADS * KEY_W), lambda b, i: (b, i, 0)),
            pl.BlockSpec((1, s, MLA_HEADS * KEY_W), lambda b, i: (b, 0, 0)),
            pl.BlockSpec((MLA_HEADS * VT_ROWS, s), lambda b, i: (0, b)),
            pl.BlockSpec((META_PAD, MLA_HEADS * KEY_W), lambda b, i: (0, 0)),
            pl.BlockSpec((MLA_HEADS * VT_ROWS, META_PAD), lambda b, i: (0, 0)),
        ],
        out_specs=pl.BlockSpec((1, tq, MLA_WIDTH), lambda b, i: (b, i, 0)),
        out_shape=jax.ShapeDtypeStruct((nb, s, MLA_WIDTH), BF16),
        scratch_shapes=_attention_scratch(MLA_HEAD_GROUP, tq),
        compiler_params=pltpu.CompilerParams(
            dimension_semantics=("arbitrary", "arbitrary"), vmem_limit_bytes=VMEM_LIMIT),
        name="mla_attention",
    )(q, k, vt, kmeta, vtmeta)


def _out_kernel(x_ref, yda_ref, ymla_ref, wo_ref, gpost_ref, gpre_ref, w1_ref, w2_ref,
                gmlp_ref, o_ref):
    tm = x_ref.shape[0]
    groups = [slice(r, r + tm // OUT_ROW_GROUPS) for r in range(0, tm, tm // OUT_ROW_GROUPS)]
    mix = [_dot(yda_ref[r, :], wo_ref[0:DA_WIDTH, :])
           + _dot(ymla_ref[r, :], wo_ref[DA_WIDTH:DA_WIDTH + MLA_WIDTH, :]) for r in groups]
    h1 = [x_ref[r, :] + _rms_scale(m) * gpost_ref[...] for r, m in zip(groups, mix)]
    u = [(_rms_scale(h) * gpre_ref[...]).astype(BF16) for h in h1]
    for r, h, ug in zip(groups, h1, u):
        f = None
        for c in range(D_FF // FF_CHUNK):
            cols = slice(FF_CHUNK * c, FF_CHUNK * (c + 1))
            hid = jnp.square(jnp.maximum(_dot(ug, w1_ref[:, cols]), 0.0)).astype(BF16)
            part = _dot(hid, w2_ref[cols, :])
            f = part if f is None else f + part
        o_ref[r, :] = h + _rms_scale(f) * gmlp_ref[...]


def _out_call(x2d, yda, ymla, w_o, g_post, g_pre, w1, w2, g_mlp):
    rows = x2d.shape[0]
    tm = OUT_TM
    row = lambda i: (i, 0)
    const = lambda i: (0, 0)
    single = pl.Buffered(1)
    return pl.pallas_call(
        _out_kernel,
        grid=(rows // tm,),
        in_specs=[
            pl.BlockSpec((tm, D_MODEL), row),
            pl.BlockSpec((tm, DA_WIDTH), row),
            pl.BlockSpec((tm, MLA_WIDTH), row),
            pl.BlockSpec((D_MODEL, D_MODEL), const, pipeline_mode=single),
            pl.BlockSpec((1, D_MODEL), const),
            pl.BlockSpec((1, D_MODEL), const),
            pl.BlockSpec((D_MODEL, D_FF), const, pipeline_mode=single),
            pl.BlockSpec((D_FF, D_MODEL), const, pipeline_mode=single),
            pl.BlockSpec((1, D_MODEL), const),
        ],
        out_specs=pl.BlockSpec((tm, D_MODEL), row),
        out_shape=jax.ShapeDtypeStruct((rows, D_MODEL), F32),
        compiler_params=pltpu.CompilerParams(
            dimension_semantics=("arbitrary",), vmem_limit_bytes=VMEM_LIMIT),
        name="out_mlp",
    )(x2d, yda, ymla, w_o, g_post, g_pre, w1, w2, g_mlp)


def _position_tables(n_pos):
    inv_freq = 1.0 / (ROPE_THETA ** (np.arange(0, MLA_ROPE, 2, dtype=np.float64) / MLA_ROPE))
    ang = np.arange(n_pos, dtype=np.float64)[:, None] * inv_freq[None, :]
    cos, sin = np.cos(ang), np.sin(ang)
    ct = np.concatenate([cos, cos] * (LANES // MLA_ROPE), axis=1).astype(np.float32)
    st = np.concatenate([-sin, sin] * (LANES // MLA_ROPE), axis=1).astype(np.float32)
    ipos = np.arange(n_pos)
    hi = ((ipos // CHUNK) * CHUNK)[:, None]
    lo = (ipos % CHUNK)[:, None]
    posf = np.concatenate([hi, lo] * ALIBI_PIECES
                          + [np.zeros((n_pos, LANES - 2 * ALIBI_PIECES), np.int64)], axis=1)
    return ct, st, posf.astype(ml_dtypes.bfloat16)


def kernel(x, meta_tokens, g_attn_pre, w_in, da_lambda_q1, da_lambda_k1, da_lambda_q2,
           da_lambda_k2, g_da_sub, g_mla_q, w_mla_q_up, g_mla_kv, w_mla_kv_up, w_o,
           g_attn_post, g_mlp_pre, w_ff1, w_ff2, g_mlp_post):
    nb, seq, d = x.shape
    assert d == D_MODEL and w_in.shape[0] == 1, "single-layer block only"
    assert seq % ATT_TQ == 0 and seq % PROJ_TM == 0 and ATT_TQ == ATT_TK

    w_in0 = w_in[0].astype(BF16)
    n_da = 2 * DA_WIDTH
    v_cols = slice(n_da, n_da + DA_WIDTH)
    rest = w_in0[:, n_da + DA_WIDTH:]
    kr = rest[:, MLA_Q_RANK + MLA_KV_RANK:]
    zpad = jnp.zeros((D_MODEL, LANES - MLA_ROPE), BF16)
    w_in_ext = jnp.concatenate(
        [w_in0[:, :n_da], rest[:, :MLA_Q_RANK + MLA_KV_RANK], kr, zpad], axis=1)
    w_vt = w_in0[:, v_cols].T
    wq = w_mla_q_up[0].astype(BF16).reshape(MLA_Q_RANK, MLA_HEADS, MLA_NOPE + MLA_ROPE)
    wq_ext = jnp.concatenate(
        [wq[..., :MLA_NOPE].reshape(MLA_Q_RANK, Q_ROPE0),
         wq[..., MLA_NOPE:].reshape(MLA_Q_RANK, MLA_HEADS * MLA_ROPE)], axis=1)
    wkv = w_mla_kv_up[0].astype(BF16).reshape(MLA_KV_RANK, MLA_HEADS, MLA_NOPE + MLA_V)
    wkv_k = wkv[..., :MLA_NOPE].reshape(MLA_KV_RANK, MLA_HEADS * MLA_NOPE)
    wkv_vt = wkv[..., MLA_NOPE:].reshape(MLA_KV_RANK, MLA_WIDTH).T

    ct, st, posf = _position_tables(N_META + seq)
    lam_rows = jnp.concatenate(
        [da_lambda_q1, da_lambda_k1, da_lambda_q2, da_lambda_k2], axis=0).astype(F32)
    lam_params = jnp.pad(lam_rows, ((0, 8 - lam_rows.shape[0]), (0, LANES - DA_HEAD_DIM)))

    proj = functools.partial(
        _proj_call, g_pre=g_attn_pre, w_in=w_in_ext, w_vt=w_vt, g_q=g_mla_q, w_q=wq_ext,
        g_kv=g_mla_kv, w_kv=wkv_k, w_kvt=wkv_vt)
    x2d = x.reshape(nb * seq, D_MODEL)
    qda, kda, vtda, qm, km, vtm = proj(
        x2d, PROJ_TM, seq // PROJ_TM, ct=ct[N_META:], st=st[N_META:], name="proj_tokens")
    meta_rows = jnp.pad(meta_tokens.astype(F32), ((0, META_PAD - N_META), (0, 0)))
    _, kda_meta, vtda_meta, _, km_meta, vtm_meta = proj(
        meta_rows, META_PAD, 1, ct=ct[:META_PAD], st=st[:META_PAD], name="proj_meta")

    def per_batch(a):
        return a.reshape(nb, seq, a.shape[-1])

    yda = _da_call(per_batch(qda), per_batch(kda), posf[N_META:], vtda, kda_meta, posf[:META_PAD],
                   vtda_meta, lam_params, g_da_sub.reshape(DA_V_DIM, 1))
    ymla = _mla_call(per_batch(qm), per_batch(km), vtm, km_meta, vtm_meta)

    out = _out_call(x2d, yda.reshape(nb * seq, DA_WIDTH), ymla.reshape(nb * seq, MLA_WIDTH),
                    w_o[0].astype(BF16), g_attn_post, g_mlp_pre,
                    w_ff1[0].astype(BF16), w_ff2[0].astype(BF16), g_mlp_post)
    return out.reshape(nb, seq, D_MODEL)
```

```python
import functools
import math
from typing import Any, Callable, NamedTuple

import jax
import jax.numpy as jnp
import ml_dtypes
import numpy as np
from jax import lax
from jax.experimental import pallas as pl
from jax.experimental.pallas import tpu as pltpu

F32 = jnp.float32
BF16 = jnp.bfloat16

D_MODEL = 1024
N_META = 16
CHUNK = 64
EPS = 1e-6
NEG = -1e30
ROPE_THETA = 10000.0
LOG2E = math.log2(math.e)

DA_HEADS = 4
DA_HEAD_DIM = 64
DA_V_DIM = 128
DA_WIDTH = DA_HEADS * DA_V_DIM
MLA_HEADS = 4
MLA_NOPE = 128
MLA_ROPE = 64
MLA_V = 128
MLA_WIDTH = MLA_HEADS * MLA_V
MLA_Q_RANK = 256
MLA_KV_RANK = 128
D_FF = 4 * D_MODEL
LAMBDA_INIT = 0.8 - 0.6 * math.exp(-0.3 * 0)

LANES = 128
BF16_SUBLANES = 16
KEY_W = 2 * LANES
VT_ROWS = DA_V_DIM + BF16_SUBLANES
VMEM_LIMIT = 61 * 1024 * 1024

PROJ_TM = 1024
PROJ_ROW_GROUPS = 4
ATT_TQ = 512
ATT_TK = 512
OUT_TM = 1024
FF_CHUNK = 1024
OUT_ROW_GROUPS = 4
META_PAD = 128
DA_HEAD_GROUP = 4
MLA_HEAD_GROUP = 4

C_QDA, C_KDA, C_CQ, C_CKV, C_KR, C_END = (0, 512, 1024, 1280, 1408, 1536)
Q_ROPE0 = MLA_HEADS * MLA_NOPE
Q_UP_W = Q_ROPE0 + MLA_HEADS * MLA_ROPE
ALIBI_PIECES = 3


def _rms_scale(x):
    return x * lax.rsqrt(jnp.mean(x * x, axis=-1, keepdims=True) + EPS)


def _dot(a, b):
    return jnp.dot(a, b, preferred_element_type=F32)


def _dot_nt(a, b):
    return lax.dot_general(a, b, (((1,), (1,)), ((), ())), preferred_element_type=F32)


def _alibi_pieces(h):
    c = np.float32(2.0 ** (-8.0 * (h + 1) / DA_HEADS)) * np.float32(LOG2E)
    pieces, rest = [], c
    for _ in range(ALIBI_PIECES):
        piece = np.float32(ml_dtypes.bfloat16(rest))
        pieces.append(float(piece))
        rest = np.float32(rest - piece)
    assert rest == 0.0
    return float(c), pieces


def _proj_kernel(x_ref, g_ref, win_ref, wvt_ref, gq_ref, wq_ref, gkv_ref, wkv_ref, wkvt_ref,
                 ct_ref, st_ref,
                 qda_ref, kda_ref, vtda_ref, qm_ref, km_ref, vtm_ref):
    tm = x_ref.shape[0]
    n_groups = PROJ_ROW_GROUPS if tm % (PROJ_ROW_GROUPS * LANES) == 0 else 1
    groups = [slice(r, r + tm // n_groups) for r in range(0, tm, tm // n_groups)]
    ones = jnp.ones((BF16_SUBLANES, tm // n_groups), BF16)
    scale = (MLA_NOPE + MLA_ROPE) ** -0.5 * LOG2E

    u = [(_rms_scale(x_ref[r, :]) * g_ref[...]).astype(BF16) for r in groups]
    p = [_dot(ug, win_ref[...]) for ug in u]
    vt = [_dot_nt(wvt_ref[...], ug) for ug in u]

    for r, pg, vtg in zip(groups, p, vt):
        qda_ref[r, :] = (pg[:, C_QDA:C_KDA] * (DA_HEAD_DIM ** -0.5 * LOG2E)).astype(BF16)
        kda_ref[r, :] = pg[:, C_KDA:C_CQ].astype(BF16)
        for h in range(DA_HEADS):
            vtda_ref[VT_ROWS * h:VT_ROWS * h + DA_V_DIM, r] = (
                vtg[DA_V_DIM * h:DA_V_DIM * (h + 1), :].astype(BF16))
            vtda_ref[VT_ROWS * h + DA_V_DIM:VT_ROWS * (h + 1), r] = ones

    cq = [(_rms_scale(pg[:, C_CQ:C_CKV]) * gq_ref[...]).astype(BF16) for pg in p]
    ckv = [(_rms_scale(pg[:, C_CKV:C_KR]) * gkv_ref[...]).astype(BF16) for pg in p]
    qu = [_dot(c, wq_ref[...]) for c in cq]
    kn = [_dot(c, wkv_ref[...]) for c in ckv]
    vtm = [_dot_nt(wkvt_ref[...], c) for c in ckv]
    lane = lax.broadcasted_iota(jnp.int32, (tm // n_groups, LANES), 1)
    first_half = (lane % MLA_ROPE) < MLA_ROPE // 2
    low = lane < MLA_ROPE

    def rope(xr, ct, st):
        swapped = jnp.where(first_half, pltpu.roll(xr, LANES - MLA_ROPE // 2, 1),
                            pltpu.roll(xr, MLA_ROPE // 2, 1))
        return xr * ct + swapped * st

    for r, pg, qug, kng, vtmg in zip(groups, p, qu, kn, vtm):
        ct = ct_ref[r, :]
        st = st_ref[r, :]
        k_rope = jnp.where(low, rope(pg[:, C_KR:C_END], ct, st), 0.0).astype(BF16)
        for t in range(MLA_HEADS // 2):
            roped = rope(qug[:, Q_ROPE0 + LANES * t:Q_ROPE0 + LANES * (t + 1)], ct, st) * scale
            for h, part in ((2 * t, roped), (2 * t + 1, pltpu.roll(roped, MLA_ROPE, 1))):
                qm_ref[r, KEY_W * h + LANES:KEY_W * (h + 1)] = (
                    jnp.where(low, part, 0.0).astype(BF16))
        for h in range(MLA_HEADS):
            qm_ref[r, KEY_W * h:KEY_W * h + LANES] = (
                qug[:, MLA_NOPE * h:MLA_NOPE * (h + 1)] * scale).astype(BF16)
            km_ref[r, KEY_W * h:KEY_W * h + LANES] = kng[:, LANES * h:LANES * (h + 1)].astype(BF16)
            km_ref[r, KEY_W * h + LANES:KEY_W * (h + 1)] = k_rope
            vtm_ref[VT_ROWS * h:VT_ROWS * h + MLA_V, r] = (
                vtmg[MLA_V * h:MLA_V * (h + 1), :].astype(BF16))
            vtm_ref[VT_ROWS * h + MLA_V:VT_ROWS * (h + 1), r] = ones


def _proj_call(x2d, tm, n_tab_blocks, g_pre, w_in, w_vt, g_q, w_q, g_kv, w_kv, w_kvt,
               ct, st, name):
    rows = x2d.shape[0]
    grid = (rows // tm,)
    row = lambda i: (i, 0)
    col = lambda i: (0, i)
    const = lambda i: (0, 0)
    tab = lambda i: (i % n_tab_blocks, 0)
    row_out = lambda w: (pl.BlockSpec((tm, w), row), jax.ShapeDtypeStruct((rows, w), BF16))
    col_out = lambda r: (pl.BlockSpec((r, tm), col), jax.ShapeDtypeStruct((r, rows), BF16))
    outs = [row_out(DA_WIDTH), row_out(DA_WIDTH), col_out(DA_HEADS * VT_ROWS),
            row_out(MLA_HEADS * KEY_W), row_out(MLA_HEADS * KEY_W), col_out(MLA_HEADS * VT_ROWS)]
    return pl.pallas_call(
        _proj_kernel,
        grid=grid,
        in_specs=[
            pl.BlockSpec((tm, D_MODEL), row),
            pl.BlockSpec((1, D_MODEL), const),
            pl.BlockSpec((D_MODEL, C_END), const),
            pl.BlockSpec((DA_WIDTH, D_MODEL), const),
            pl.BlockSpec((1, MLA_Q_RANK), const),
            pl.BlockSpec((MLA_Q_RANK, Q_UP_W), const),
            pl.BlockSpec((1, MLA_KV_RANK), const),
            pl.BlockSpec((MLA_KV_RANK, MLA_HEADS * MLA_NOPE), const),
            pl.BlockSpec((MLA_WIDTH, MLA_KV_RANK), const),
            pl.BlockSpec((tm, LANES), tab),
            pl.BlockSpec((tm, LANES), tab),
        ],
        out_specs=[o[0] for o in outs],
        out_shape=[o[1] for o in outs],
        compiler_params=pltpu.CompilerParams(
            dimension_semantics=("arbitrary",), vmem_limit_bytes=VMEM_LIMIT),
        name=name,
    )(x2d, g_pre, w_in, w_vt, g_q, w_q, g_kv, w_kv, w_kvt, ct, st)


def _softmax_step(s_t, vt, m_ref, acc_ref):
    m_prev = m_ref[...]
    m_new = jnp.maximum(m_prev, jnp.max(s_t, axis=0, keepdims=True))
    p_t = jnp.exp2(s_t - m_new).astype(BF16)
    acc_ref[...] = jnp.exp2(m_prev - m_new) * acc_ref[...] + _dot(vt, p_t)
    m_ref[...] = m_new


def _diag_iotas(tk, tq):
    ik = lax.broadcasted_iota(jnp.int32, (tk, tq), 0)
    iq = lax.broadcasted_iota(jnp.int32, (tk, tq), 1)
    return ik, iq


def _normalised(acc_ref, dv):
    return acc_ref[0:dv, :] / acc_ref[dv:dv + 1, :]


class _Chain(NamedTuple):
    load_q: Callable[[Any], jax.Array]
    load_k: Callable[[Any], jax.Array]
    load_kmeta: Callable[[], jax.Array]
    vrows: Any
    diag_fn: Callable[[jax.Array, int], jax.Array]


def _attend(qi, chains, vtmeta_ref, vt_ref, s_bufs, smeta_ref, m_ref, acc_ref,
            on_chain_done):
    tk, tq = ATT_TK, ATT_TQ
    half = tq // 2
    buf_a, buf_b = s_bufs
    halves = ((0, half), (half, tk))

    m_ref[...] = jnp.full(m_ref.shape, NEG, F32)
    acc_ref[...] = jnp.zeros(acc_ref.shape, F32)

    def key_block(j, size=tk):
        start = j * tk
        return pl.ds(start if isinstance(j, int) else pl.multiple_of(start, tk), size)

    def scores_into(buf, j):
        for c, ch in enumerate(chains):
            buf[c] = _dot_nt(ch.load_k(key_block(j)), ch.load_q(slice(None)))

    def consume(buf, j):
        for c, ch in enumerate(chains):
            _softmax_step(buf[c], vt_ref[ch.vrows, key_block(j)], m_ref.at[c], acc_ref.at[c])

    def diagonal_scores_into(buf, jd):
        for c, ch in enumerate(chains):
            kmeta = ch.load_kmeta()
            for q0, n_keys in halves:
                keys = jnp.concatenate([ch.load_k(key_block(jd, n_keys)), kmeta], axis=0)
                s_t = _dot_nt(keys, ch.load_q(slice(q0, q0 + half)))
                buf[c, 0:n_keys, q0:q0 + half] = s_t[0:n_keys]
                smeta_ref[c, :, q0:q0 + half] = s_t[n_keys:n_keys + META_PAD]

    def finish(buf, jd):
        meta_valid = lax.broadcasted_iota(jnp.int32, (META_PAD, half), 0) < N_META
        for c, ch in enumerate(chains):
            for q0, n_keys in halves:
                lanes = pl.ds(q0, half)
                s_t = jnp.concatenate(
                    [ch.diag_fn(buf[c, 0:n_keys, q0:q0 + half], q0),
                     jnp.where(meta_valid, smeta_ref[c, :, q0:q0 + half], NEG)], axis=0)
                vt = jnp.concatenate(
                    [vt_ref[ch.vrows, key_block(jd, n_keys)], vtmeta_ref[ch.vrows, :]], axis=1)
                _softmax_step(s_t, vt, m_ref.at[c, :, lanes], acc_ref.at[c, :, lanes])
            on_chain_done(c)

    def pair(t):
        j = 2 * t
        scores_into(buf_b, j + 1)
        consume(buf_a, j)
        scores_into(buf_a, j + 2)
        consume(buf_b, j + 1)

    def odd_tail(n):
        diagonal_scores_into(buf_b, n)
        consume(buf_a, n - 1)
        finish(buf_b, n)

    def even_tail(n):
        scores_into(buf_b, n - 1)
        consume(buf_a, n - 2)
        diagonal_scores_into(buf_a, n)
        consume(buf_b, n - 1)
        finish(buf_a, n)

    @pl.when(qi == 0)
    def _():
        diagonal_scores_into(buf_a, 0)
        finish(buf_a, 0)

    @pl.when(qi == 1)
    def _():
        scores_into(buf_a, 0)
        odd_tail(1)

    @pl.when(qi == 2)
    def _():
        scores_into(buf_a, 0)
        even_tail(2)

    @pl.when(qi >= 3)
    def _():
        scores_into(buf_a, 0)
        pair(0)

        def body(t, carry):
            pair(t)
            return carry

        lax.fori_loop(1, (qi - 1) // 2, body, 0)

        @pl.when(qi % 2 == 1)
        def _():
            odd_tail(qi)

        @pl.when(qi % 2 == 0)
        def _():
            even_tail(qi)


def _attention_scratch(n_chains, tq):
    stat = pltpu.VMEM((n_chains, 1, tq), F32)
    scores = pltpu.VMEM((n_chains, ATT_TK, tq), F32)
    return [stat, pltpu.VMEM((n_chains, VT_ROWS, tq), F32), scores, scores,
            pltpu.VMEM((n_chains, META_PAD, tq), F32)]


def _da_kernel(q_ref, k_ref, posf_ref, vt_ref, kmeta_ref, posfmeta_ref, vtmeta_ref, lam_ref,
               gsub_ref, y_ref,
               qx_ref, m_ref, acc_ref, sa_ref, sb_ref, smeta_ref):
    qi = pl.program_id(1)
    tq, tk = ATT_TQ, ATT_TK
    lane = lax.broadcasted_iota(jnp.int32, (tq, LANES), 1)
    lp = lam_ref[...]
    lam = (jnp.exp(jnp.sum(lp[0:1] * lp[1:2], axis=-1, keepdims=True))
           - jnp.exp(jnp.sum(lp[2:3] * lp[3:4], axis=-1, keepdims=True)) + LAMBDA_INIT)

    for h0 in range(0, DA_HEADS, DA_HEAD_GROUP):
        chains = []
        for g in range(DA_HEAD_GROUP):
            h = h0 + g
            c, pieces = _alibi_pieces(h)
            q = q_ref[0, :, LANES * h:LANES * (h + 1)]
            pf = jnp.zeros((tq, LANES), F32)
            for i, piece in enumerate(pieces):
                pf = jnp.where((lane == 2 * i) | (lane == 2 * i + 1), piece, pf)
            pf = pf.astype(BF16)
            zero = jnp.zeros_like(q)
            qx_ref[2 * g, :, 0:LANES] = jnp.where(lane < DA_HEAD_DIM, q, zero)
            qx_ref[2 * g + 1, :, 0:LANES] = jnp.where(lane >= DA_HEAD_DIM, q, zero)
            qx_ref[2 * g, :, LANES:KEY_W] = pf
            qx_ref[2 * g + 1, :, LANES:KEY_W] = pf

            def diag_fn(s_t, q0, c=c):
                ik, iq = _diag_iotas(*s_t.shape)
                iq = iq + q0
                visible = (ik // CHUNK) <= (iq // CHUNK)
                corr = (2.0 * c) * jnp.minimum(iq - ik, 0).astype(F32)
                return jnp.where(visible, s_t + corr, NEG)

            def load_k(rows, h=h):
                return jnp.concatenate(
                    [k_ref[0, rows, LANES * h:LANES * (h + 1)], posf_ref[rows, :]], axis=1)

            def load_kmeta(h=h):
                return jnp.concatenate(
                    [kmeta_ref[:, LANES * h:LANES * (h + 1)], posfmeta_ref[...]], axis=1)

            for comp in range(2):
                chains.append(_Chain(
                    load_q=functools.partial(lambda i, rows: qx_ref[i, rows, :], 2 * g + comp),
                    load_k=load_k, load_kmeta=load_kmeta, vrows=pl.ds(VT_ROWS * h, VT_ROWS),
                    diag_fn=diag_fn))

        def finish_head(c, h0=h0):
            if c % 2 == 0:
                return
            h = h0 + c // 2
            o = (_normalised(acc_ref.at[c - 1], DA_V_DIM)
                 - lam * _normalised(acc_ref.at[c], DA_V_DIM))
            o = o * lax.rsqrt(jnp.mean(o * o, axis=0, keepdims=True) + EPS)
            o = o * gsub_ref[...] * (1.0 - LAMBDA_INIT)
            y_ref[0, :, LANES * h:LANES * (h + 1)] = o.T.astype(BF16)

        _attend(qi, chains, vtmeta_ref, vt_ref, (sa_ref, sb_ref), smeta_ref, m_ref, acc_ref,
                finish_head)


def _da_call(q, k, posf, vt, kmeta, posf_meta, vtmeta, lam_params, g_sub_col):
    nb, s, _ = q.shape
    tq = ATT_TQ
    n_chains = 2 * DA_HEAD_GROUP
    return pl.pallas_call(
        _da_kernel,
        grid=(nb, s // tq),
        in_specs=[
            pl.BlockSpec((1, tq, DA_WIDTH), lambda b, i: (b, i, 0)),
            pl.BlockSpec((1, s, DA_WIDTH), lambda b, i: (b, 0, 0)),
            pl.BlockSpec((s, LANES), lambda b, i: (0, 0), pipeline_mode=pl.Buffered(1)),
            pl.BlockSpec((DA_HEADS * VT_ROWS, s), lambda b, i: (0, b)),
            pl.BlockSpec((META_PAD, DA_WIDTH), lambda b, i: (0, 0)),
            pl.BlockSpec((META_PAD, LANES), lambda b, i: (0, 0)),
            pl.BlockSpec((DA_HEADS * VT_ROWS, META_PAD), lambda b, i: (0, 0)),
            pl.BlockSpec((8, LANES), lambda b, i: (0, 0)),
            pl.BlockSpec((DA_V_DIM, 1), lambda b, i: (0, 0)),
        ],
        out_specs=pl.BlockSpec((1, tq, DA_WIDTH), lambda b, i: (b, i, 0)),
        out_shape=jax.ShapeDtypeStruct((nb, s, DA_WIDTH), BF16),
        scratch_shapes=[pltpu.VMEM((n_chains, tq, KEY_W), BF16)] + _attention_scratch(n_chains, tq),
        compiler_params=pltpu.CompilerParams(
            dimension_semantics=("arbitrary", "arbitrary"), vmem_limit_bytes=VMEM_LIMIT),
        name="diff_attention",
    )(q, k, posf, vt, kmeta, posf_meta, vtmeta, lam_params, g_sub_col)


def _mla_kernel(q_ref, k_ref, vt_ref, kmeta_ref, vtmeta_ref, y_ref, m_ref, acc_ref,
                sa_ref, sb_ref, smeta_ref):
    qi = pl.program_id(1)
    tq, tk = ATT_TQ, ATT_TK

    def diag_fn(s_t, q0):
        ik, iq = _diag_iotas(*s_t.shape)
        return jnp.where((ik // CHUNK) <= ((iq + q0) // CHUNK), s_t, NEG)

    for h0 in range(0, MLA_HEADS, MLA_HEAD_GROUP):
        heads = range(h0, h0 + MLA_HEAD_GROUP)
        chains = [
            _Chain(load_q=functools.partial(
                       lambda h, rows: q_ref[0, rows, KEY_W * h:KEY_W * (h + 1)], h),
                   load_k=functools.partial(
                       lambda h, rows: k_ref[0, rows, KEY_W * h:KEY_W * (h + 1)], h),
                   load_kmeta=functools.partial(
                       lambda h: kmeta_ref[:, KEY_W * h:KEY_W * (h + 1)], h),
                   vrows=pl.ds(VT_ROWS * h, VT_ROWS), diag_fn=diag_fn)
            for h in heads]
        def finish_head(c, h0=h0):
            h = h0 + c
            y_ref[0, :, LANES * h:LANES * (h + 1)] = (
                _normalised(acc_ref.at[c], MLA_V).T.astype(BF16))

        _attend(qi, chains, vtmeta_ref, vt_ref, (sa_ref, sb_ref), smeta_ref, m_ref, acc_ref,
                finish_head)


def _mla_call(q, k, vt, kmeta, vtmeta):
    nb, s, _ = q.shape
    tq = ATT_TQ
    return pl.pallas_call(
        _mla_kernel,
        grid=(nb, s // tq),
        in_specs=[
            pl.BlockSpec((1, tq, MLA_HEADS * KEY_W), lambda b, i: (b, i, 0)),
            pl.BlockSpec((1, s, MLA_HEADS * KEY_W), lambda b, i: (b, 0, 0)),
            pl.BlockSpec((MLA_HEADS * VT_ROWS, s), lambda b, i: (0, b)),
            pl.BlockSpec((META_PAD, MLA_HEADS * KEY_W), lambda b, i: (0, 0)),
            pl.BlockSpec((MLA_HEADS * VT_ROWS, META_PAD), lambda b, i: (0, 0)),
        ],
        out_specs=pl.BlockSpec((1, tq, MLA_WIDTH), lambda b, i: (b, i, 0)),
        out_shape=jax.ShapeDtypeStruct((nb, s, MLA_WIDTH), BF16),
        scratch_shapes=_attention_scratch(MLA_HEAD_GROUP, tq),
        compiler_params=pltpu.CompilerParams(
            dimension_semantics=("arbitrary", "arbitrary"), vmem_limit_bytes=VMEM_LIMIT),
        name="mla_attention",
    )(q, k, vt, kmeta, vtmeta)


def _out_kernel(x_ref, yda_ref, ymla_ref, wo_ref, gpost_ref, gpre_ref, w1_ref, w2_ref,
                gmlp_ref, o_ref):
    tm = x_ref.shape[0]
    groups = [slice(r, r + tm // OUT_ROW_GROUPS) for r in range(0, tm, tm // OUT_ROW_GROUPS)]
    mix = [_dot(yda_ref[r, :], wo_ref[0:DA_WIDTH, :])
           + _dot(ymla_ref[r, :], wo_ref[DA_WIDTH:DA_WIDTH + MLA_WIDTH, :]) for r in groups]
    h1 = [x_ref[r, :] + _rms_scale(m) * gpost_ref[...] for r, m in zip(groups, mix)]
    u = [(_rms_scale(h) * gpre_ref[...]).astype(BF16) for h in h1]
    for r, h, ug in zip(groups, h1, u):
        f = None
        for c in range(D_FF // FF_CHUNK):
            cols = slice(FF_CHUNK * c, FF_CHUNK * (c + 1))
            hid = jnp.square(jnp.maximum(_dot(ug, w1_ref[:, cols]), 0.0)).astype(BF16)
            part = _dot(hid, w2_ref[cols, :])
            f = part if f is None else f + part
        o_ref[r, :] = h + _rms_scale(f) * gmlp_ref[...]


def _out_call(x2d, yda, ymla, w_o, g_post, g_pre, w1, w2, g_mlp):
    rows = x2d.shape[0]
    tm = OUT_TM
    row = lambda i: (i, 0)
    const = lambda i: (0, 0)
    single = pl.Buffered(1)
    return pl.pallas_call(
        _out_kernel,
        grid=(rows // tm,),
        in_specs=[
            pl.BlockSpec((tm, D_MODEL), row),
            pl.BlockSpec((tm, DA_WIDTH), row),
            pl.BlockSpec((tm, MLA_WIDTH), row),
            pl.BlockSpec((D_MODEL, D_MODEL), const, pipeline_mode=single),
            pl.BlockSpec((1, D_MODEL), const),
            pl.BlockSpec((1, D_MODEL), const),
            pl.BlockSpec((D_MODEL, D_FF), const, pipeline_mode=single),
            pl.BlockSpec((D_FF, D_MODEL), const, pipeline_mode=single),
            pl.BlockSpec((1, D_MODEL), const),
        ],
        out_specs=pl.BlockSpec((tm, D_MODEL), row),
        out_shape=jax.ShapeDtypeStruct((rows, D_MODEL), F32),
        compiler_params=pltpu.CompilerParams(
            dimension_semantics=("arbitrary",), vmem_limit_bytes=VMEM_LIMIT),
        name="out_mlp",
    )(x2d, yda, ymla, w_o, g_post, g_pre, w1, w2, g_mlp)


def _position_tables(n_pos):
    inv_freq = 1.0 / (ROPE_THETA ** (np.arange(0, MLA_ROPE, 2, dtype=np.float64) / MLA_ROPE))
    ang = np.arange(n_pos, dtype=np.float64)[:, None] * inv_freq[None, :]
    cos, sin = np.cos(ang), np.sin(ang)
    ct = np.concatenate([cos, cos] * (LANES // MLA_ROPE), axis=1).astype(np.float32)
    st = np.concatenate([-sin, sin] * (LANES // MLA_ROPE), axis=1).astype(np.float32)
    ipos = np.arange(n_pos)
    hi = ((ipos // CHUNK) * CHUNK)[:, None]
    lo = (ipos % CHUNK)[:, None]
    posf = np.concatenate([hi, lo] * ALIBI_PIECES
                          + [np.zeros((n_pos, LANES - 2 * ALIBI_PIECES), np.int64)], axis=1)
    return ct, st, posf.astype(ml_dtypes.bfloat16)


def kernel(x, meta_tokens, g_attn_pre, w_in, da_lambda_q1, da_lambda_k1, da_lambda_q2,
           da_lambda_k2, g_da_sub, g_mla_q, w_mla_q_up, g_mla_kv, w_mla_kv_up, w_o,
           g_attn_post, g_mlp_pre, w_ff1, w_ff2, g_mlp_post):
    nb, seq, d = x.shape
    assert d == D_MODEL and w_in.shape[0] == 1, "single-layer block only"
    assert seq % ATT_TQ == 0 and seq % PROJ_TM == 0 and ATT_TQ == ATT_TK

    w_in0 = w_in[0].astype(BF16)
    n_da = 2 * DA_WIDTH
    v_cols = slice(n_da, n_da + DA_WIDTH)
    rest = w_in0[:, n_da + DA_WIDTH:]
    kr = rest[:, MLA_Q_RANK + MLA_KV_RANK:]
    zpad = jnp.zeros((D_MODEL, LANES - MLA_ROPE), BF16)
    w_in_ext = jnp.concatenate(
        [w_in0[:, :n_da], rest[:, :MLA_Q_RANK + MLA_KV_RANK], kr, zpad], axis=1)
    w_vt = w_in0[:, v_cols].T
    wq = w_mla_q_up[0].astype(BF16).reshape(MLA_Q_RANK, MLA_HEADS, MLA_NOPE + MLA_ROPE)
    wq_ext = jnp.concatenate(
        [wq[..., :MLA_NOPE].reshape(MLA_Q_RANK, Q_ROPE0),
         wq[..., MLA_NOPE:].reshape(MLA_Q_RANK, MLA_HEADS * MLA_ROPE)], axis=1)
    wkv = w_mla_kv_up[0].astype(BF16).reshape(MLA_KV_RANK, MLA_HEADS, MLA_NOPE + MLA_V)
    wkv_k = wkv[..., :MLA_NOPE].reshape(MLA_KV_RANK, MLA_HEADS * MLA_NOPE)
    wkv_vt = wkv[..., MLA_NOPE:].reshape(MLA_KV_RANK, MLA_WIDTH).T

    ct, st, posf = _position_tables(N_META + seq)
    lam_rows = jnp.concatenate(
        [da_lambda_q1, da_lambda_k1, da_lambda_q2, da_lambda_k2], axis=0).astype(F32)
    lam_params = jnp.pad(lam_rows, ((0, 8 - lam_rows.shape[0]), (0, LANES - DA_HEAD_DIM)))

    proj = functools.partial(
        _proj_call, g_pre=g_attn_pre, w_in=w_in_ext, w_vt=w_vt, g_q=g_mla_q, w_q=wq_ext,
        g_kv=g_mla_kv, w_kv=wkv_k, w_kvt=wkv_vt)
    x2d = x.reshape(nb * seq, D_MODEL)
    qda, kda, vtda, qm, km, vtm = proj(
        x2d, PROJ_TM, seq // PROJ_TM, ct=ct[N_META:], st=st[N_META:], name="proj_tokens")
    meta_rows = jnp.pad(meta_tokens.astype(F32), ((0, META_PAD - N_META), (0, 0)))
    _, kda_meta, vtda_meta, _, km_meta, vtm_meta = proj(
        meta_rows, META_PAD, 1, ct=ct[:META_PAD], st=st[:META_PAD], name="proj_meta")

    def per_batch(a):
        return a.reshape(nb, seq, a.shape[-1])

    yda = _da_call(per_batch(qda), per_batch(kda), posf[N_META:], vtda, kda_meta, posf[:META_PAD],
                   vtda_meta, lam_params, g_da_sub.reshape(DA_V_DIM, 1))
    ymla = _mla_call(per_batch(qm), per_batch(km), vtm, km_meta, vtm_meta)

    out = _out_call(x2d, yda.reshape(nb * seq, DA_WIDTH), ymla.reshape(nb * seq, MLA_WIDTH),
                    w_o[0].astype(BF16), g_attn_post, g_mlp_pre,
                    w_ff1[0].astype(BF16), w_ff2[0].astype(BF16), g_mlp_post)
    return out.reshape(nb, seq, D_MODEL)
```

```python
import functools
import math
from typing import Any, Callable, NamedTuple

import jax
import jax.numpy as jnp
import ml_dtypes
import numpy as np
from jax import lax
from jax.experimental import pallas as pl
from jax.experimental.pallas import tpu as pltpu

F32 = jnp.float32
BF16 = jnp.bfloat16

D_MODEL = 1024
N_META = 16
CHUNK = 64
EPS = 1e-6
NEG = -1e30
ROPE_THETA = 10000.0
LOG2E = math.log2(math.e)

DA_HEADS = 4
DA_HEAD_DIM = 64
DA_V_DIM = 128
DA_WIDTH = DA_HEADS * DA_V_DIM
MLA_HEADS = 4
MLA_NOPE = 128
MLA_ROPE = 64
MLA_V = 128
MLA_WIDTH = MLA_HEADS * MLA_V
MLA_Q_RANK = 256
MLA_KV_RANK = 128
D_FF = 4 * D_MODEL
LAMBDA_INIT = 0.8 - 0.6 * math.exp(-0.3 * 0)

LANES = 128
BF16_SUBLANES = 16
KEY_W = 2 * LANES
VT_ROWS = DA_V_DIM + BF16_SUBLANES
VMEM_LIMIT = 61 * 1024 * 1024

PROJ_TM = 1024
PROJ_ROW_GROUPS = 4
ATT_TQ = 512
ATT_TK = 512
OUT_TM = 1024
FF_CHUNK = 1024
OUT_ROW_GROUPS = 4
META_PAD = 128
DA_HEAD_GROUP = 4
MLA_HEAD_GROUP = 4
MLA_STATIC_STEPS = 5

C_QDA, C_KDA, C_CQ, C_CKV, C_KR, C_END = (0, 512, 1024, 1280, 1408, 1536)
Q_ROPE0 = MLA_HEADS * MLA_NOPE
Q_UP_W = Q_ROPE0 + MLA_HEADS * MLA_ROPE
ALIBI_PIECES = 3


def _rms_scale(x):
    return x * lax.rsqrt(jnp.mean(x * x, axis=-1, keepdims=True) + EPS)


def _dot(a, b):
    return jnp.dot(a, b, preferred_element_type=F32)


def _dot_nt(a, b):
    return lax.dot_general(a, b, (((1,), (1,)), ((), ())), preferred_element_type=F32)


def _alibi_pieces(h):
    c = np.float32(2.0 ** (-8.0 * (h + 1) / DA_HEADS)) * np.float32(LOG2E)
    pieces, rest = [], c
    for _ in range(ALIBI_PIECES):
        piece = np.float32(ml_dtypes.bfloat16(rest))
        pieces.append(float(piece))
        rest = np.float32(rest - piece)
    assert rest == 0.0
    return float(c), pieces


def _proj_kernel(x_ref, g_ref, win_ref, wvt_ref, gq_ref, wq_ref, gkv_ref, wkv_ref, wkvt_ref,
                 ct_ref, st_ref,
                 qda_ref, kda_ref, vtda_ref, qm_ref, km_ref, vtm_ref):
    tm = x_ref.shape[0]
    n_groups = PROJ_ROW_GROUPS if tm % (PROJ_ROW_GROUPS * LANES) == 0 else 1
    groups = [slice(r, r + tm // n_groups) for r in range(0, tm, tm // n_groups)]
    ones = jnp.ones((BF16_SUBLANES, tm // n_groups), BF16)
    scale = (MLA_NOPE + MLA_ROPE) ** -0.5 * LOG2E

    u = [(_rms_scale(x_ref[r, :]) * g_ref[...]).astype(BF16) for r in groups]
    p = [_dot(ug, win_ref[...]) for ug in u]
    vt = [_dot_nt(wvt_ref[...], ug) for ug in u]

    for r, pg, vtg in zip(groups, p, vt):
        qda_ref[r, :] = (pg[:, C_QDA:C_KDA] * (DA_HEAD_DIM ** -0.5 * LOG2E)).astype(BF16)
        kda_ref[r, :] = pg[:, C_KDA:C_CQ].astype(BF16)
        for h in range(DA_HEADS):
            vtda_ref[VT_ROWS * h:VT_ROWS * h + DA_V_DIM, r] = (
                vtg[DA_V_DIM * h:DA_V_DIM * (h + 1), :].astype(BF16))
            vtda_ref[VT_ROWS * h + DA_V_DIM:VT_ROWS * (h + 1), r] = ones

    cq = [(_rms_scale(pg[:, C_CQ:C_CKV]) * gq_ref[...]).astype(BF16) for pg in p]
    ckv = [(_rms_scale(pg[:, C_CKV:C_KR]) * gkv_ref[...]).astype(BF16) for pg in p]
    qu = [_dot(c, wq_ref[...]) for c in cq]
    kn = [_dot(c, wkv_ref[...]) for c in ckv]
    vtm = [_dot_nt(wkvt_ref[...], c) for c in ckv]
    lane = lax.broadcasted_iota(jnp.int32, (tm // n_groups, LANES), 1)
    first_half = (lane % MLA_ROPE) < MLA_ROPE // 2
    low = lane < MLA_ROPE

    def rope(xr, ct, st):
        swapped = jnp.where(first_half, pltpu.roll(xr, LANES - MLA_ROPE // 2, 1),
                            pltpu.roll(xr, MLA_ROPE // 2, 1))
        return xr * ct + swapped * st

    for r, pg, qug, kng, vtmg in zip(groups, p, qu, kn, vtm):
        ct = ct_ref[r, :]
        st = st_ref[r, :]
        k_rope = jnp.where(low, rope(pg[:, C_KR:C_END], ct, st), 0.0).astype(BF16)
        for t in range(MLA_HEADS // 2):
            roped = rope(qug[:, Q_ROPE0 + LANES * t:Q_ROPE0 + LANES * (t + 1)], ct, st) * scale
            for h, part in ((2 * t, roped), (2 * t + 1, pltpu.roll(roped, MLA_ROPE, 1))):
                qm_ref[r, KEY_W * h + LANES:KEY_W * (h + 1)] = (
                    jnp.where(low, part, 0.0).astype(BF16))
        for h in range(MLA_HEADS):
            qm_ref[r, KEY_W * h:KEY_W * h + LANES] = (
                qug[:, MLA_NOPE * h:MLA_NOPE * (h + 1)] * scale).astype(BF16)
            km_ref[r, KEY_W * h:KEY_W * h + LANES] = kng[:, LANES * h:LANES * (h + 1)].astype(BF16)
            km_ref[r, KEY_W * h + LANES:KEY_W * (h + 1)] = k_rope
            vtm_ref[VT_ROWS * h:VT_ROWS * h + MLA_V, r] = (
                vtmg[MLA_V * h:MLA_V * (h + 1), :].astype(BF16))
            vtm_ref[VT_ROWS * h + MLA_V:VT_ROWS * (h + 1), r] = ones


def _proj_call(x2d, tm, n_tab_blocks, g_pre, w_in, w_vt, g_q, w_q, g_kv, w_kv, w_kvt,
               ct, st, name):
    rows = x2d.shape[0]
    grid = (rows // tm,)
    row = lambda i: (i, 0)
    col = lambda i: (0, i)
    const = lambda i: (0, 0)
    tab = lambda i: (i % n_tab_blocks, 0)
    row_out = lambda w: (pl.BlockSpec((tm, w), row), jax.ShapeDtypeStruct((rows, w), BF16))
    col_out = lambda r: (pl.BlockSpec((r, tm), col), jax.ShapeDtypeStruct((r, rows), BF16))
    outs = [row_out(DA_WIDTH), row_out(DA_WIDTH), col_out(DA_HEADS * VT_ROWS),
            row_out(MLA_HEADS * KEY_W), row_out(MLA_HEADS * KEY_W), col_out(MLA_HEADS * VT_ROWS)]
    return pl.pallas_call(
        _proj_kernel,
        grid=grid,
        in_specs=[
            pl.BlockSpec((tm, D_MODEL), row),
            pl.BlockSpec((1, D_MODEL), const),
            pl.BlockSpec((D_MODEL, C_END), const),
            pl.BlockSpec((DA_WIDTH, D_MODEL), const),
            pl.BlockSpec((1, MLA_Q_RANK), const),
            pl.BlockSpec((MLA_Q_RANK, Q_UP_W), const),
            pl.BlockSpec((1, MLA_KV_RANK), const),
            pl.BlockSpec((MLA_KV_RANK, MLA_HEADS * MLA_NOPE), const),
            pl.BlockSpec((MLA_WIDTH, MLA_KV_RANK), const),
            pl.BlockSpec((tm, LANES), tab),
            pl.BlockSpec((tm, LANES), tab),
        ],
        out_specs=[o[0] for o in outs],
        out_shape=[o[1] for o in outs],
        compiler_params=pltpu.CompilerParams(
            dimension_semantics=("arbitrary",), vmem_limit_bytes=VMEM_LIMIT),
        name=name,
    )(x2d, g_pre, w_in, w_vt, g_q, w_q, g_kv, w_kv, w_kvt, ct, st)


def _softmax_step(s_t, vt, m_ref, acc_ref):
    m_prev = m_ref[...]
    m_new = jnp.maximum(m_prev, jnp.max(s_t, axis=0, keepdims=True))
    p_t = jnp.exp2(s_t - m_new).astype(BF16)
    acc_ref[...] = jnp.exp2(m_prev - m_new) * acc_ref[...] + _dot(vt, p_t)
    m_ref[...] = m_new


def _diag_iotas(tk, tq):
    ik = lax.broadcasted_iota(jnp.int32, (tk, tq), 0)
    iq = lax.broadcasted_iota(jnp.int32, (tk, tq), 1)
    return ik, iq


def _normalised(acc_ref, dv):
    return acc_ref[0:dv, :] / acc_ref[dv:dv + 1, :]


class _Chain(NamedTuple):
    load_q: Callable[[Any], jax.Array]
    load_k: Callable[[Any], jax.Array]
    load_kmeta: Callable[[], jax.Array]
    vrows: Any
    diag_fn: Callable[[jax.Array, int], jax.Array]


def _attend(qi, chains, vtmeta_ref, vt_ref, s_bufs, smeta_ref, m_ref, acc_ref,
            on_chain_done, static_steps=3):
    assert static_steps >= 3
    tk, tq = ATT_TK, ATT_TQ
    half = tq // 2
    buf_a, buf_b = s_bufs
    halves = ((0, half), (half, tk))

    m_ref[...] = jnp.full(m_ref.shape, NEG, F32)
    acc_ref[...] = jnp.zeros(acc_ref.shape, F32)

    def key_block(j, size=tk):
        start = j * tk
        return pl.ds(start if isinstance(j, int) else pl.multiple_of(start, tk), size)

    def scores_into(buf, j):
        for c, ch in enumerate(chains):
            buf[c] = _dot_nt(ch.load_k(key_block(j)), ch.load_q(slice(None)))

    def consume(buf, j):
        for c, ch in enumerate(chains):
            _softmax_step(buf[c], vt_ref[ch.vrows, key_block(j)], m_ref.at[c], acc_ref.at[c])

    def diagonal_scores_into(buf, jd):
        for c, ch in enumerate(chains):
            kmeta = ch.load_kmeta()
            for q0, n_keys in halves:
                keys = jnp.concatenate([ch.load_k(key_block(jd, n_keys)), kmeta], axis=0)
                s_t = _dot_nt(keys, ch.load_q(slice(q0, q0 + half)))
                buf[c, 0:n_keys, q0:q0 + half] = s_t[0:n_keys]
                smeta_ref[c, :, q0:q0 + half] = s_t[n_keys:n_keys + META_PAD]

    def finish(buf, jd):
        meta_valid = lax.broadcasted_iota(jnp.int32, (META_PAD, half), 0) < N_META
        for c, ch in enumerate(chains):
            for q0, n_keys in halves:
                lanes = pl.ds(q0, half)
                s_t = jnp.concatenate(
                    [ch.diag_fn(buf[c, 0:n_keys, q0:q0 + half], q0),
                     jnp.where(meta_valid, smeta_ref[c, :, q0:q0 + half], NEG)], axis=0)
                vt = jnp.concatenate(
                    [vt_ref[ch.vrows, key_block(jd, n_keys)], vtmeta_ref[ch.vrows, :]], axis=1)
                _softmax_step(s_t, vt, m_ref.at[c, :, lanes], acc_ref.at[c, :, lanes])
            on_chain_done(c)

    def pair(t):
        j = 2 * t
        scores_into(buf_b, j + 1)
        consume(buf_a, j)
        scores_into(buf_a, j + 2)
        consume(buf_b, j + 1)

    def odd_tail(n):
        diagonal_scores_into(buf_b, n)
        consume(buf_a, n - 1)
        finish(buf_b, n)

    def even_tail(n):
        scores_into(buf_b, n - 1)
        consume(buf_a, n - 2)
        diagonal_scores_into(buf_a, n)
        consume(buf_b, n - 1)
        finish(buf_a, n)

    def whole_step(n):
        if n == 0:
            diagonal_scores_into(buf_a, 0)
            finish(buf_a, 0)
            return
        scores_into(buf_a, 0)
        for t in range((n - 1) // 2):
            pair(t)
        (odd_tail if n % 2 else even_tail)(n)

    for n in range(static_steps):
        pl.when(qi == n)(functools.partial(whole_step, n))

    @pl.when(qi >= static_steps)
    def _():
        scores_into(buf_a, 0)
        pair(0)

        def body(t, carry):
            pair(t)
            return carry

        lax.fori_loop(1, (qi - 1) // 2, body, 0)

        @pl.when(qi % 2 == 1)
        def _():
            odd_tail(qi)

        @pl.when(qi % 2 == 0)
        def _():
            even_tail(qi)


def _attention_scratch(n_chains, tq):
    stat = pltpu.VMEM((n_chains, 1, tq), F32)
    scores = pltpu.VMEM((n_chains, ATT_TK, tq), F32)
    return [stat, pltpu.VMEM((n_chains, VT_ROWS, tq), F32), scores, scores,
            pltpu.VMEM((n_chains, META_PAD, tq), F32)]


def _da_kernel(q_ref, k_ref, posf_ref, vt_ref, kmeta_ref, posfmeta_ref, vtmeta_ref, lam_ref,
               gsub_ref, y_ref,
               qx_ref, m_ref, acc_ref, sa_ref, sb_ref, smeta_ref):
    qi = pl.program_id(1)
    tq, tk = ATT_TQ, ATT_TK
    lane = lax.broadcasted_iota(jnp.int32, (tq, LANES), 1)
    lp = lam_ref[...]
    lam = (jnp.exp(jnp.sum(lp[0:1] * lp[1:2], axis=-1, keepdims=True))
           - jnp.exp(jnp.sum(lp[2:3] * lp[3:4], axis=-1, keepdims=True)) + LAMBDA_INIT)

    for h0 in range(0, DA_HEADS, DA_HEAD_GROUP):
        chains = []
        for g in range(DA_HEAD_GROUP):
            h = h0 + g
            c, pieces = _alibi_pieces(h)
            q = q_ref[0, :, LANES * h:LANES * (h + 1)]
            pf = jnp.zeros((tq, LANES), F32)
            for i, piece in enumerate(pieces):
                pf = jnp.where((lane == 2 * i) | (lane == 2 * i + 1), piece, pf)
            pf = pf.astype(BF16)
            zero = jnp.zeros_like(q)
            qx_ref[2 * g, :, 0:LANES] = jnp.where(lane < DA_HEAD_DIM, q, zero)
            qx_ref[2 * g + 1, :, 0:LANES] = jnp.where(lane >= DA_HEAD_DIM, q, zero)
            qx_ref[2 * g, :, LANES:KEY_W] = pf
            qx_ref[2 * g + 1, :, LANES:KEY_W] = pf

            def diag_fn(s_t, q0, c=c):
                ik, iq = _diag_iotas(*s_t.shape)
                iq = iq + q0
                visible = (ik // CHUNK) <= (iq // CHUNK)
                corr = (2.0 * c) * jnp.minimum(iq - ik, 0).astype(F32)
                return jnp.where(visible, s_t + corr, NEG)

            def load_k(rows, h=h):
                return jnp.concatenate(
                    [k_ref[0, rows, LANES * h:LANES * (h + 1)], posf_ref[rows, :]], axis=1)

            def load_kmeta(h=h):
                return jnp.concatenate(
                    [kmeta_ref[:, LANES * h:LANES * (h + 1)], posfmeta_ref[...]], axis=1)

            for comp in range(2):
                chains.append(_Chain(
                    load_q=functools.partial(lambda i, rows: qx_ref[i, rows, :], 2 * g + comp),
                    load_k=load_k, load_kmeta=load_kmeta, vrows=pl.ds(VT_ROWS * h, VT_ROWS),
                    diag_fn=diag_fn))

        def finish_head(c, h0=h0):
            if c % 2 == 0:
                return
            h = h0 + c // 2
            o = (_normalised(acc_ref.at[c - 1], DA_V_DIM)
                 - lam * _normalised(acc_ref.at[c], DA_V_DIM))
            o = o * lax.rsqrt(jnp.mean(o * o, axis=0, keepdims=True) + EPS)
            o = o * gsub_ref[...] * (1.0 - LAMBDA_INIT)
            y_ref[0, :, LANES * h:LANES * (h + 1)] = o.T.astype(BF16)

        _attend(qi, chains, vtmeta_ref, vt_ref, (sa_ref, sb_ref), smeta_ref, m_ref, acc_ref,
                finish_head)


def _da_call(q, k, posf, vt, kmeta, posf_meta, vtmeta, lam_params, g_sub_col):
    nb, s, _ = q.shape
    tq = ATT_TQ
    n_chains = 2 * DA_HEAD_GROUP
    return pl.pallas_call(
        _da_kernel,
        grid=(nb, s // tq),
        in_specs=[
            pl.BlockSpec((1, tq, DA_WIDTH), lambda b, i: (b, i, 0)),
            pl.BlockSpec((1, s, DA_WIDTH), lambda b, i: (b, 0, 0)),
            pl.BlockSpec((s, LANES), lambda b, i: (0, 0), pipeline_mode=pl.Buffered(1)),
            pl.BlockSpec((DA_HEADS * VT_ROWS, s), lambda b, i: (0, b)),
            pl.BlockSpec((META_PAD, DA_WIDTH), lambda b, i: (0, 0)),
            pl.BlockSpec((META_PAD, LANES), lambda b, i: (0, 0)),
            pl.BlockSpec((DA_HEADS * VT_ROWS, META_PAD), lambda b, i: (0, 0)),
            pl.BlockSpec((8, LANES), lambda b, i: (0, 0)),
            pl.BlockSpec((DA_V_DIM, 1), lambda b, i: (0, 0)),
        ],
        out_specs=pl.BlockSpec((1, tq, DA_WIDTH), lambda b, i: (b, i, 0)),
        out_shape=jax.ShapeDtypeStruct((nb, s, DA_WIDTH), BF16),
        scratch_shapes=[pltpu.VMEM((n_chains, tq, KEY_W), BF16)] + _attention_scratch(n_chains, tq),
        compiler_params=pltpu.CompilerParams(
            dimension_semantics=("arbitrary", "arbitrary"), vmem_limit_bytes=VMEM_LIMIT),
        name="diff_attention",
    )(q, k, posf, vt, kmeta, posf_meta, vtmeta, lam_params, g_sub_col)


def _mla_kernel(q_ref, k_ref, vt_ref, kmeta_ref, vtmeta_ref, y_ref, m_ref, acc_ref,
                sa_ref, sb_ref, smeta_ref):
    qi = pl.program_id(1)
    tq, tk = ATT_TQ, ATT_TK

    def diag_fn(s_t, q0):
        ik, iq = _diag_iotas(*s_t.shape)
        return jnp.where((ik // CHUNK) <= ((iq + q0) // CHUNK), s_t, NEG)

    for h0 in range(0, MLA_HEADS, MLA_HEAD_GROUP):
        heads = range(h0, h0 + MLA_HEAD_GROUP)
        chains = [
            _Chain(load_q=functools.partial(
                       lambda h, rows: q_ref[0, rows, KEY_W * h:KEY_W * (h + 1)], h),
                   load_k=functools.partial(
                       lambda h, rows: k_ref[0, rows, KEY_W * h:KEY_W * (h + 1)], h),
                   load_kmeta=functools.partial(
                       lambda h: kmeta_ref[:, KEY_W * h:KEY_W * (h + 1)], h),
                   vrows=pl.ds(VT_ROWS * h, VT_ROWS), diag_fn=diag_fn)
            for h in heads]
        def finish_head(c, h0=h0):
            h = h0 + c
            y_ref[0, :, LANES * h:LANES * (h + 1)] = (
                _normalised(acc_ref.at[c], MLA_V).T.astype(BF16))

        _attend(qi, chains, vtmeta_ref, vt_ref, (sa_ref, sb_ref), smeta_ref, m_ref, acc_ref,
                finish_head, static_steps=MLA_STATIC_STEPS)


def _mla_call(q, k, vt, kmeta, vtmeta):
    nb, s, _ = q.shape
    tq = ATT_TQ
    return pl.pallas_call(
        _mla_kernel,
        grid=(nb, s // tq),
        in_specs=[
            pl.BlockSpec((1, tq, MLA_HEADS * KEY_W), lambda b, i: (b, i, 0)),
            pl.BlockSpec((1, s, MLA_HEADS * KEY_W), lambda b, i: (b, 0, 0)),
            pl.BlockSpec((MLA_HEADS * VT_ROWS, s), lambda b, i: (0, b)),
            pl.BlockSpec((META_PAD, MLA_HEADS * KEY_W), lambda b, i: (0, 0)),
            pl.BlockSpec((MLA_HEADS * VT_ROWS, META_PAD), lambda b, i: (0, 0)),
        ],
        out_specs=pl.BlockSpec((1, tq, MLA_WIDTH), lambda b, i: (b, i, 0)),
        out_shape=jax.ShapeDtypeStruct((nb, s, MLA_WIDTH), BF16),
        scratch_shapes=_attention_scratch(MLA_HEAD_GROUP, tq),
        compiler_params=pltpu.CompilerParams(
            dimension_semantics=("arbitrary", "arbitrary"), vmem_limit_bytes=VMEM_LIMIT),
        name="mla_attention",
    )(q, k, vt, kmeta, vtmeta)


def _out_kernel(x_ref, yda_ref, ymla_ref, wo_ref, gpost_ref, gpre_ref, w1_ref, w2_ref,
                gmlp_ref, o_ref):
    tm = x_ref.shape[0]
    groups = [slice(r, r + tm // OUT_ROW_GROUPS) for r in range(0, tm, tm // OUT_ROW_GROUPS)]
    mix = [_dot(yda_ref[r, :], wo_ref[0:DA_WIDTH, :])
           + _dot(ymla_ref[r, :], wo_ref[DA_WIDTH:DA_WIDTH + MLA_WIDTH, :]) for r in groups]
    h1 = [x_ref[r, :] + _rms_scale(m) * gpost_ref[...] for r, m in zip(groups, mix)]
    u = [(_rms_scale(h) * gpre_ref[...]).astype(BF16) for h in h1]
    for r, h, ug in zip(groups, h1, u):
        f = None
        for c in range(D_FF // FF_CHUNK):
            cols = slice(FF_CHUNK * c, FF_CHUNK * (c + 1))
            hid = jnp.square(jnp.maximum(_dot(ug, w1_ref[:, cols]), 0.0)).astype(BF16)
            part = _dot(hid, w2_ref[cols, :])
            f = part if f is None else f + part
        o_ref[r, :] = h + _rms_scale(f) * gmlp_ref[...]


def _out_call(x2d, yda, ymla, w_o, g_post, g_pre, w1, w2, g_mlp):
    rows = x2d.shape[0]
    tm = OUT_TM
    row = lambda i: (i, 0)
    const = lambda i: (0, 0)
    single = pl.Buffered(1)
    return pl.pallas_call(
        _out_kernel,
        grid=(rows // tm,),
        in_specs=[
            pl.BlockSpec((tm, D_MODEL), row),
            pl.BlockSpec((tm, DA_WIDTH), row),
            pl.BlockSpec((tm, MLA_WIDTH), row),
            pl.BlockSpec((D_MODEL, D_MODEL), const, pipeline_mode=single),
            pl.BlockSpec((1, D_MODEL), const),
            pl.BlockSpec((1, D_MODEL), const),
            pl.BlockSpec((D_MODEL, D_FF), const, pipeline_mode=single),
            pl.BlockSpec((D_FF, D_MODEL), const, pipeline_mode=single),
            pl.BlockSpec((1, D_MODEL), const),
        ],
        out_specs=pl.BlockSpec((tm, D_MODEL), row),
        out_shape=jax.ShapeDtypeStruct((rows, D_MODEL), F32),
        compiler_params=pltpu.CompilerParams(
            dimension_semantics=("arbitrary",), vmem_limit_bytes=VMEM_LIMIT),
        name="out_mlp",
    )(x2d, yda, ymla, w_o, g_post, g_pre, w1, w2, g_mlp)


def _position_tables(n_pos):
    inv_freq = 1.0 / (ROPE_THETA ** (np.arange(0, MLA_ROPE, 2, dtype=np.float64) / MLA_ROPE))
    ang = np.arange(n_pos, dtype=np.float64)[:, None] * inv_freq[None, :]
    cos, sin = np.cos(ang), np.sin(ang)
    ct = np.concatenate([cos, cos] * (LANES // MLA_ROPE), axis=1).astype(np.float32)
    st = np.concatenate([-sin, sin] * (LANES // MLA_ROPE), axis=1).astype(np.float32)
    ipos = np.arange(n_pos)
    hi = ((ipos // CHUNK) * CHUNK)[:, None]
    lo = (ipos % CHUNK)[:, None]
    posf = np.concatenate([hi, lo] * ALIBI_PIECES
                          + [np.zeros((n_pos, LANES - 2 * ALIBI_PIECES), np.int64)], axis=1)
    return ct, st, posf.astype(ml_dtypes.bfloat16)


def kernel(x, meta_tokens, g_attn_pre, w_in, da_lambda_q1, da_lambda_k1, da_lambda_q2,
           da_lambda_k2, g_da_sub, g_mla_q, w_mla_q_up, g_mla_kv, w_mla_kv_up, w_o,
           g_attn_post, g_mlp_pre, w_ff1, w_ff2, g_mlp_post):
    nb, seq, d = x.shape
    assert d == D_MODEL and w_in.shape[0] == 1, "single-layer block only"
    assert seq % ATT_TQ == 0 and seq % PROJ_TM == 0 and ATT_TQ == ATT_TK

    w_in0 = w_in[0].astype(BF16)
    n_da = 2 * DA_WIDTH
    v_cols = slice(n_da, n_da + DA_WIDTH)
    rest = w_in0[:, n_da + DA_WIDTH:]
    kr = rest[:, MLA_Q_RANK + MLA_KV_RANK:]
    zpad = jnp.zeros((D_MODEL, LANES - MLA_ROPE), BF16)
    w_in_ext = jnp.concatenate(
        [w_in0[:, :n_da], rest[:, :MLA_Q_RANK + MLA_KV_RANK], kr, zpad], axis=1)
    w_vt = w_in0[:, v_cols].T
    wq = w_mla_q_up[0].astype(BF16).reshape(MLA_Q_RANK, MLA_HEADS, MLA_NOPE + MLA_ROPE)
    wq_ext = jnp.concatenate(
        [wq[..., :MLA_NOPE].reshape(MLA_Q_RANK, Q_ROPE0),
         wq[..., MLA_NOPE:].reshape(MLA_Q_RANK, MLA_HEADS * MLA_ROPE)], axis=1)
    wkv = w_mla_kv_up[0].astype(BF16).reshape(MLA_KV_RANK, MLA_HEADS, MLA_NOPE + MLA_V)
    wkv_k = wkv[..., :MLA_NOPE].reshape(MLA_KV_RANK, MLA_HEADS * MLA_NOPE)
    wkv_vt = wkv[..., MLA_NOPE:].reshape(MLA_KV_RANK, MLA_WIDTH).T

    ct, st, posf = _position_tables(N_META + seq)
    lam_rows = jnp.concatenate(
        [da_lambda_q1, da_lambda_k1, da_lambda_q2, da_lambda_k2], axis=0).astype(F32)
    lam_params = jnp.pad(lam_rows, ((0, 8 - lam_rows.shape[0]), (0, LANES - DA_HEAD_DIM)))

    proj = functools.partial(
        _proj_call, g_pre=g_attn_pre, w_in=w_in_ext, w_vt=w_vt, g_q=g_mla_q, w_q=wq_ext,
        g_kv=g_mla_kv, w_kv=wkv_k, w_kvt=wkv_vt)
    x2d = x.reshape(nb * seq, D_MODEL)
    qda, kda, vtda, qm, km, vtm = proj(
        x2d, PROJ_TM, seq // PROJ_TM, ct=ct[N_META:], st=st[N_META:], name="proj_tokens")
    meta_rows = jnp.pad(meta_tokens.astype(F32), ((0, META_PAD - N_META), (0, 0)))
    _, kda_meta, vtda_meta, _, km_meta, vtm_meta = proj(
        meta_rows, META_PAD, 1, ct=ct[:META_PAD], st=st[:META_PAD], name="proj_meta")

    def per_batch(a):
        return a.reshape(nb, seq, a.shape[-1])

    yda = _da_call(per_batch(qda), per_batch(kda), posf[N_META:], vtda, kda_meta, posf[:META_PAD],
                   vtda_meta, lam_params, g_da_sub.reshape(DA_V_DIM, 1))
    ymla = _mla_call(per_batch(qm), per_batch(km), vtm, km_meta, vtm_meta)

    out = _out_call(x2d, yda.reshape(nb * seq, DA_WIDTH), ymla.reshape(nb * seq, MLA_WIDTH),
                    w_o[0].astype(BF16), g_attn_post, g_mlp_pre,
                    w_ff1[0].astype(BF16), w_ff2[0].astype(BF16), g_mlp_post)
    return out.reshape(nb, seq, D_MODEL)
```

```python
import functools
import math
from typing import Any, Callable, NamedTuple

import jax
import jax.numpy as jnp
import ml_dtypes
import numpy as np
from jax import lax
from jax.experimental import pallas as pl
from jax.experimental.pallas import tpu as pltpu

F32 = jnp.float32
BF16 = jnp.bfloat16

D_MODEL = 1024
N_META = 16
CHUNK = 64
EPS = 1e-6
NEG = -1e30
ROPE_THETA = 10000.0
LOG2E = math.log2(math.e)

DA_HEADS = 4
DA_HEAD_DIM = 64
DA_V_DIM = 128
DA_WIDTH = DA_HEADS * DA_V_DIM
MLA_HEADS = 4
MLA_NOPE = 128
MLA_ROPE = 64
MLA_V = 128
MLA_WIDTH = MLA_HEADS * MLA_V
MLA_Q_RANK = 256
MLA_KV_RANK = 128
D_FF = 4 * D_MODEL
LAMBDA_INIT = 0.8 - 0.6 * math.exp(-0.3 * 0)

LANES = 128
BF16_SUBLANES = 16
KEY_W = 2 * LANES
VT_ROWS = DA_V_DIM + BF16_SUBLANES
VMEM_LIMIT = 61 * 1024 * 1024

PROJ_TM = 1024
PROJ_ROW_GROUPS = 4
ATT_TQ = 512
ATT_TK = 512
OUT_TM = 1024
FF_CHUNK = 1024
OUT_ROW_GROUPS = 2
META_PAD = 128
DA_HEAD_GROUP = 4
MLA_HEAD_GROUP = 4

C_QDA, C_KDA, C_CQ, C_CKV, C_KR, C_END = (0, 512, 1024, 1280, 1408, 1536)
Q_ROPE0 = MLA_HEADS * MLA_NOPE
Q_UP_W = Q_ROPE0 + MLA_HEADS * MLA_ROPE
ALIBI_PIECES = 3


def _rms_scale(x):
    return x * lax.rsqrt(jnp.mean(x * x, axis=-1, keepdims=True) + EPS)


def _dot(a, b):
    return jnp.dot(a, b, preferred_element_type=F32)


def _dot_nt(a, b):
    return lax.dot_general(a, b, (((1,), (1,)), ((), ())), preferred_element_type=F32)


def _alibi_pieces(h):
    c = np.float32(2.0 ** (-8.0 * (h + 1) / DA_HEADS)) * np.float32(LOG2E)
    pieces, rest = [], c
    for _ in range(ALIBI_PIECES):
        piece = np.float32(ml_dtypes.bfloat16(rest))
        pieces.append(float(piece))
        rest = np.float32(rest - piece)
    assert rest == 0.0
    return float(c), pieces


def _proj_kernel(x_ref, g_ref, win_ref, wvt_ref, gq_ref, wq_ref, gkv_ref, wkv_ref, wkvt_ref,
                 ct_ref, st_ref,
                 qda_ref, kda_ref, vtda_ref, qm_ref, km_ref, vtm_ref):
    tm = x_ref.shape[0]
    n_groups = PROJ_ROW_GROUPS if tm % (PROJ_ROW_GROUPS * LANES) == 0 else 1
    groups = [slice(r, r + tm // n_groups) for r in range(0, tm, tm // n_groups)]
    ones = jnp.ones((BF16_SUBLANES, tm // n_groups), BF16)
    scale = (MLA_NOPE + MLA_ROPE) ** -0.5 * LOG2E

    u = [(_rms_scale(x_ref[r, :]) * g_ref[...]).astype(BF16) for r in groups]
    p = [_dot(ug, win_ref[...]) for ug in u]
    vt = [_dot_nt(wvt_ref[...], ug) for ug in u]

    for r, pg, vtg in zip(groups, p, vt):
        qda_ref[r, :] = (pg[:, C_QDA:C_KDA] * (DA_HEAD_DIM ** -0.5 * LOG2E)).astype(BF16)
        kda_ref[r, :] = pg[:, C_KDA:C_CQ].astype(BF16)
        for h in range(DA_HEADS):
            vtda_ref[VT_ROWS * h:VT_ROWS * h + DA_V_DIM, r] = (
                vtg[DA_V_DIM * h:DA_V_DIM * (h + 1), :].astype(BF16))
            vtda_ref[VT_ROWS * h + DA_V_DIM:VT_ROWS * (h + 1), r] = ones

    cq = [(_rms_scale(pg[:, C_CQ:C_CKV]) * gq_ref[...]).astype(BF16) for pg in p]
    ckv = [(_rms_scale(pg[:, C_CKV:C_KR]) * gkv_ref[...]).astype(BF16) for pg in p]
    qu = [_dot(c, wq_ref[...]) for c in cq]
    kn = [_dot(c, wkv_ref[...]) for c in ckv]
    vtm = [_dot_nt(wkvt_ref[...], c) for c in ckv]
    lane = lax.broadcasted_iota(jnp.int32, (tm // n_groups, LANES), 1)
    first_half = (lane % MLA_ROPE) < MLA_ROPE // 2
    low = lane < MLA_ROPE

    def rope(xr, ct, st):
        swapped = jnp.where(first_half, pltpu.roll(xr, LANES - MLA_ROPE // 2, 1),
                            pltpu.roll(xr, MLA_ROPE // 2, 1))
        return xr * ct + swapped * st

    for r, pg, qug, kng, vtmg in zip(groups, p, qu, kn, vtm):
        ct = ct_ref[r, :]
        st = st_ref[r, :]
        k_rope = jnp.where(low, rope(pg[:, C_KR:C_END], ct, st), 0.0).astype(BF16)
        for t in range(MLA_HEADS // 2):
            roped = rope(qug[:, Q_ROPE0 + LANES * t:Q_ROPE0 + LANES * (t + 1)], ct, st) * scale
            for h, part in ((2 * t, roped), (2 * t + 1, pltpu.roll(roped, MLA_ROPE, 1))):
                qm_ref[r, KEY_W * h + LANES:KEY_W * (h + 1)] = (
                    jnp.where(low, part, 0.0).astype(BF16))
        for h in range(MLA_HEADS):
            qm_ref[r, KEY_W * h:KEY_W * h + LANES] = (
                qug[:, MLA_NOPE * h:MLA_NOPE * (h + 1)] * scale).astype(BF16)
            km_ref[r, KEY_W * h:KEY_W * h + LANES] = kng[:, LANES * h:LANES * (h + 1)].astype(BF16)
            km_ref[r, KEY_W * h + LANES:KEY_W * (h + 1)] = k_rope
            vtm_ref[VT_ROWS * h:VT_ROWS * h + MLA_V, r] = (
                vtmg[MLA_V * h:MLA_V * (h + 1), :].astype(BF16))
            vtm_ref[VT_ROWS * h + MLA_V:VT_ROWS * (h + 1), r] = ones


def _proj_call(x2d, tm, n_tab_blocks, g_pre, w_in, w_vt, g_q, w_q, g_kv, w_kv, w_kvt,
               ct, st, name):
    rows = x2d.shape[0]
    grid = (rows // tm,)
    row = lambda i: (i, 0)
    col = lambda i: (0, i)
    const = lambda i: (0, 0)
    tab = lambda i: (i % n_tab_blocks, 0)
    row_out = lambda w: (pl.BlockSpec((tm, w), row), jax.ShapeDtypeStruct((rows, w), BF16))
    col_out = lambda r: (pl.BlockSpec((r, tm), col), jax.ShapeDtypeStruct((r, rows), BF16))
    outs = [row_out(DA_WIDTH), row_out(DA_WIDTH), col_out(DA_HEADS * VT_ROWS),
            row_out(MLA_HEADS * KEY_W), row_out(MLA_HEADS * KEY_W), col_out(MLA_HEADS * VT_ROWS)]
    return pl.pallas_call(
        _proj_kernel,
        grid=grid,
        in_specs=[
            pl.BlockSpec((tm, D_MODEL), row),
            pl.BlockSpec((1, D_MODEL), const),
            pl.BlockSpec((D_MODEL, C_END), const),
            pl.BlockSpec((DA_WIDTH, D_MODEL), const),
            pl.BlockSpec((1, MLA_Q_RANK), const),
            pl.BlockSpec((MLA_Q_RANK, Q_UP_W), const),
            pl.BlockSpec((1, MLA_KV_RANK), const),
            pl.BlockSpec((MLA_KV_RANK, MLA_HEADS * MLA_NOPE), const),
            pl.BlockSpec((MLA_WIDTH, MLA_KV_RANK), const),
            pl.BlockSpec((tm, LANES), tab),
            pl.BlockSpec((tm, LANES), tab),
        ],
        out_specs=[o[0] for o in outs],
        out_shape=[o[1] for o in outs],
        compiler_params=pltpu.CompilerParams(
            dimension_semantics=("arbitrary",), vmem_limit_bytes=VMEM_LIMIT),
        name=name,
    )(x2d, g_pre, w_in, w_vt, g_q, w_q, g_kv, w_kv, w_kvt, ct, st)


def _softmax_step(s_t, vt, m_ref, acc_ref):
    m_prev = m_ref[...]
    m_new = jnp.maximum(m_prev, jnp.max(s_t, axis=0, keepdims=True))
    p_t = jnp.exp2(s_t - m_new).astype(BF16)
    acc_ref[...] = jnp.exp2(m_prev - m_new) * acc_ref[...] + _dot(vt, p_t)
    m_ref[...] = m_new


def _diag_iotas(tk, tq):
    ik = lax.broadcasted_iota(jnp.int32, (tk, tq), 0)
    iq = lax.broadcasted_iota(jnp.int32, (tk, tq), 1)
    return ik, iq


def _normalised(acc_ref, dv):
    return acc_ref[0:dv, :] / acc_ref[dv:dv + 1, :]


class _Chain(NamedTuple):
    load_q: Callable[[Any], jax.Array]
    load_k: Callable[[Any], jax.Array]
    load_kmeta: Callable[[], jax.Array]
    vrows: Any
    diag_fn: Callable[[jax.Array, int], jax.Array]


def _attend(qi, chains, vtmeta_ref, vt_ref, s_bufs, smeta_ref, m_ref, acc_ref,
            on_chain_done):
    tk, tq = ATT_TK, ATT_TQ
    half = tq // 2
    buf_a, buf_b = s_bufs
    halves = ((0, half), (half, tk))

    m_ref[...] = jnp.full(m_ref.shape, NEG, F32)
    acc_ref[...] = jnp.zeros(acc_ref.shape, F32)

    def key_block(j, size=tk):
        start = j * tk
        return pl.ds(start if isinstance(j, int) else pl.multiple_of(start, tk), size)

    def scores_into(buf, j):
        for c, ch in enumerate(chains):
            buf[c] = _dot_nt(ch.load_k(key_block(j)), ch.load_q(slice(None)))

    def consume(buf, j):
        for c, ch in enumerate(chains):
            _softmax_step(buf[c], vt_ref[ch.vrows, key_block(j)], m_ref.at[c], acc_ref.at[c])

    def diagonal_scores_into(buf, jd):
        for c, ch in enumerate(chains):
            kmeta = ch.load_kmeta()
            for q0, n_keys in halves:
                keys = jnp.concatenate([ch.load_k(key_block(jd, n_keys)), kmeta], axis=0)
                s_t = _dot_nt(keys, ch.load_q(slice(q0, q0 + half)))
                buf[c, 0:n_keys, q0:q0 + half] = s_t[0:n_keys]
                smeta_ref[c, :, q0:q0 + half] = s_t[n_keys:n_keys + META_PAD]

    def finish(buf, jd):
        meta_valid = lax.broadcasted_iota(jnp.int32, (META_PAD, half), 0) < N_META
        for c, ch in enumerate(chains):
            for q0, n_keys in halves:
                lanes = pl.ds(q0, half)
                s_t = jnp.concatenate(
                    [ch.diag_fn(buf[c, 0:n_keys, q0:q0 + half], q0),
                     jnp.where(meta_valid, smeta_ref[c, :, q0:q0 + half], NEG)], axis=0)
                vt = jnp.concatenate(
                    [vt_ref[ch.vrows, key_block(jd, n_keys)], vtmeta_ref[ch.vrows, :]], axis=1)
                _softmax_step(s_t, vt, m_ref.at[c, :, lanes], acc_ref.at[c, :, lanes])
            on_chain_done(c)

    def pair(t):
        j = 2 * t
        scores_into(buf_b, j + 1)
        consume(buf_a, j)
        scores_into(buf_a, j + 2)
        consume(buf_b, j + 1)

    def odd_tail(n):
        diagonal_scores_into(buf_b, n)
        consume(buf_a, n - 1)
        finish(buf_b, n)

    def even_tail(n):
        scores_into(buf_b, n - 1)
        consume(buf_a, n - 2)
        diagonal_scores_into(buf_a, n)
        consume(buf_b, n - 1)
        finish(buf_a, n)

    @pl.when(qi == 0)
    def _():
        diagonal_scores_into(buf_a, 0)
        finish(buf_a, 0)

    @pl.when(qi == 1)
    def _():
        scores_into(buf_a, 0)
        odd_tail(1)

    @pl.when(qi == 2)
    def _():
        scores_into(buf_a, 0)
        even_tail(2)

    @pl.when(qi >= 3)
    def _():
        scores_into(buf_a, 0)
        pair(0)

        def body(t, carry):
            pair(t)
            return carry

        lax.fori_loop(1, (qi - 1) // 2, body, 0)

        @pl.when(qi % 2 == 1)
        def _():
            odd_tail(qi)

        @pl.when(qi % 2 == 0)
        def _():
            even_tail(qi)


def _attention_scratch(n_chains, tq):
    stat = pltpu.VMEM((n_chains, 1, tq), F32)
    scores = pltpu.VMEM((n_chains, ATT_TK, tq), F32)
    return [stat, pltpu.VMEM((n_chains, VT_ROWS, tq), F32), scores, scores,
            pltpu.VMEM((n_chains, META_PAD, tq), F32)]


def _da_kernel(q_ref, k_ref, posf_ref, vt_ref, kmeta_ref, posfmeta_ref, vtmeta_ref, lam_ref,
               gsub_ref, y_ref,
               qx_ref, m_ref, acc_ref, sa_ref, sb_ref, smeta_ref):
    qi = pl.program_id(1)
    tq, tk = ATT_TQ, ATT_TK
    lane = lax.broadcasted_iota(jnp.int32, (tq, LANES), 1)
    lp = lam_ref[...]
    lam = (jnp.exp(jnp.sum(lp[0:1] * lp[1:2], axis=-1, keepdims=True))
           - jnp.exp(jnp.sum(lp[2:3] * lp[3:4], axis=-1, keepdims=True)) + LAMBDA_INIT)

    for h0 in range(0, DA_HEADS, DA_HEAD_GROUP):
        chains = []
        for g in range(DA_HEAD_GROUP):
            h = h0 + g
            c, pieces = _alibi_pieces(h)
            q = q_ref[0, :, LANES * h:LANES * (h + 1)]
            pf = jnp.zeros((tq, LANES), F32)
            for i, piece in enumerate(pieces):
                pf = jnp.where((lane == 2 * i) | (lane == 2 * i + 1), piece, pf)
            pf = pf.astype(BF16)
            zero = jnp.zeros_like(q)
            qx_ref[2 * g, :, 0:LANES] = jnp.where(lane < DA_HEAD_DIM, q, zero)
            qx_ref[2 * g + 1, :, 0:LANES] = jnp.where(lane >= DA_HEAD_DIM, q, zero)
            qx_ref[2 * g, :, LANES:KEY_W] = pf
            qx_ref[2 * g + 1, :, LANES:KEY_W] = pf

            def diag_fn(s_t, q0, c=c):
                ik, iq = _diag_iotas(*s_t.shape)
                iq = iq + q0
                visible = (ik // CHUNK) <= (iq // CHUNK)
                corr = (2.0 * c) * jnp.minimum(iq - ik, 0).astype(F32)
                return jnp.where(visible, s_t + corr, NEG)

            def load_k(rows, h=h):
                return jnp.concatenate(
                    [k_ref[0, rows, LANES * h:LANES * (h + 1)], posf_ref[rows, :]], axis=1)

            def load_kmeta(h=h):
                return jnp.concatenate(
                    [kmeta_ref[:, LANES * h:LANES * (h + 1)], posfmeta_ref[...]], axis=1)

            for comp in range(2):
                chains.append(_Chain(
                    load_q=functools.partial(lambda i, rows: qx_ref[i, rows, :], 2 * g + comp),
                    load_k=load_k, load_kmeta=load_kmeta, vrows=pl.ds(VT_ROWS * h, VT_ROWS),
                    diag_fn=diag_fn))

        def finish_head(c, h0=h0):
            if c % 2 == 0:
                return
            h = h0 + c // 2
            o = (_normalised(acc_ref.at[c - 1], DA_V_DIM)
                 - lam * _normalised(acc_ref.at[c], DA_V_DIM))
            o = o * lax.rsqrt(jnp.mean(o * o, axis=0, keepdims=True) + EPS)
            o = o * gsub_ref[...] * (1.0 - LAMBDA_INIT)
            y_ref[0, :, LANES * h:LANES * (h + 1)] = o.T.astype(BF16)

        _attend(qi, chains, vtmeta_ref, vt_ref, (sa_ref, sb_ref), smeta_ref, m_ref, acc_ref,
                finish_head)


def _da_call(q, k, posf, vt, kmeta, posf_meta, vtmeta, lam_params, g_sub_col):
    nb, s, _ = q.shape
    tq = ATT_TQ
    n_chains = 2 * DA_HEAD_GROUP
    return pl.pallas_call(
        _da_kernel,
        grid=(nb, s // tq),
        in_specs=[
            pl.BlockSpec((1, tq, DA_WIDTH), lambda b, i: (b, i, 0)),
            pl.BlockSpec((1, s, DA_WIDTH), lambda b, i: (b, 0, 0)),
            pl.BlockSpec((s, LANES), lambda b, i: (0, 0), pipeline_mode=pl.Buffered(1)),
            pl.BlockSpec((DA_HEADS * VT_ROWS, s), lambda b, i: (0, b)),
            pl.BlockSpec((META_PAD, DA_WIDTH), lambda b, i: (0, 0)),
            pl.BlockSpec((META_PAD, LANES), lambda b, i: (0, 0)),
            pl.BlockSpec((DA_HEADS * VT_ROWS, META_PAD), lambda b, i: (0, 0)),
            pl.BlockSpec((8, LANES), lambda b, i: (0, 0)),
            pl.BlockSpec((DA_V_DIM, 1), lambda b, i: (0, 0)),
        ],
        out_specs=pl.BlockSpec((1, tq, DA_WIDTH), lambda b, i: (b, i, 0)),
        out_shape=jax.ShapeDtypeStruct((nb, s, DA_WIDTH), BF16),
        scratch_shapes=[pltpu.VMEM((n_chains, tq, KEY_W), BF16)] + _attention_scratch(n_chains, tq),
        compiler_params=pltpu.CompilerParams(
            dimension_semantics=("arbitrary", "arbitrary"), vmem_limit_bytes=VMEM_LIMIT),
        name="diff_attention",
    )(q, k, posf, vt, kmeta, posf_meta, vtmeta, lam_params, g_sub_col)


def _mla_kernel(q_ref, k_ref, vt_ref, kmeta_ref, vtmeta_ref, y_ref, m_ref, acc_ref,
                sa_ref, sb_ref, smeta_ref):
    qi = pl.program_id(1)
    tq, tk = ATT_TQ, ATT_TK

    def diag_fn(s_t, q0):
        ik, iq = _diag_iotas(*s_t.shape)
        return jnp.where((ik // CHUNK) <= ((iq + q0) // CHUNK), s_t, NEG)

    for h0 in range(0, MLA_HEADS, MLA_HEAD_GROUP):
        heads = range(h0, h0 + MLA_HEAD_GROUP)
        chains = [
            _Chain(load_q=functools.partial(
                       lambda h, rows: q_ref[0, rows, KEY_W * h:KEY_W * (h + 1)], h),
                   load_k=functools.partial(
                       lambda h, rows: k_ref[0, rows, KEY_W * h:KEY_W * (h + 1)], h),
                   load_kmeta=functools.partial(
                       lambda h: kmeta_ref[:, KEY_W * h:KEY_W * (h + 1)], h),
                   vrows=pl.ds(VT_ROWS * h, VT_ROWS), diag_fn=diag_fn)
            for h in heads]
        def finish_head(c, h0=h0):
            h = h0 + c
            y_ref[0, :, LANES * h:LANES * (h + 1)] = (
                _normalised(acc_ref.at[c], MLA_V).T.astype(BF16))

        _attend(qi, chains, vtmeta_ref, vt_ref, (sa_ref, sb_ref), smeta_ref, m_ref, acc_ref,
                finish_head)


def _mla_call(q, k, vt, kmeta, vtmeta):
    nb, s, _ = q.shape
    tq = ATT_TQ
    return pl.pallas_call(
        _mla_kernel,
        grid=(nb, s // tq),
        in_specs=[
            pl.BlockSpec((1, tq, MLA_HEADS * KEY_W), lambda b, i: (b, i, 0)),
            pl.BlockSpec((1, s, MLA_HEADS * KEY_W), lambda b, i: (b, 0, 0)),
            pl.BlockSpec((MLA_HEADS * VT_ROWS, s), lambda b, i: (0, b)),
            pl.BlockSpec((META_PAD, MLA_HEADS * KEY_W), lambda b, i: (0, 0)),
            pl.BlockSpec((MLA_HEADS * VT_ROWS, META_PAD), lambda b, i: (0, 0)),
        ],
        out_specs=pl.BlockSpec((1, tq, MLA_WIDTH), lambda b, i: (b, i, 0)),
        out_shape=jax.ShapeDtypeStruct((nb, s, MLA_WIDTH), BF16),
        scratch_shapes=_attention_scratch(MLA_HEAD_GROUP, tq),
        compiler_params=pltpu.CompilerParams(
            dimension_semantics=("arbitrary", "arbitrary"), vmem_limit_bytes=VMEM_LIMIT),
        name="mla_attention",
    )(q, k, vt, kmeta, vtmeta)


def _out_kernel(x_ref, yda_ref, ymla_ref, wo_ref, gpost_ref, gpre_ref, w1_ref, w2_ref,
                gmlp_ref, o_ref):
    tm = x_ref.shape[0]
    groups = [slice(r, r + tm // OUT_ROW_GROUPS) for r in range(0, tm, tm // OUT_ROW_GROUPS)]
    mix = [_dot(yda_ref[r, :], wo_ref[0:DA_WIDTH, :])
           + _dot(ymla_ref[r, :], wo_ref[DA_WIDTH:DA_WIDTH + MLA_WIDTH, :]) for r in groups]
    h1 = [x_ref[r, :] + _rms_scale(m) * gpost_ref[...] for r, m in zip(groups, mix)]
    u = [(_rms_scale(h) * gpre_ref[...]).astype(BF16) for h in h1]
    for r, h, ug in zip(groups, h1, u):
        f = None
        for c in range(D_FF // FF_CHUNK):
            cols = slice(FF_CHUNK * c, FF_CHUNK * (c + 1))
            hid = jnp.square(jnp.maximum(_dot(ug, w1_ref[:, cols]), 0.0)).astype(BF16)
            part = _dot(hid, w2_ref[cols, :])
            f = part if f is None else f + part
        o_ref[r, :] = h + _rms_scale(f) * gmlp_ref[...]


def _out_call(x2d, yda, ymla, w_o, g_post, g_pre, w1, w2, g_mlp):
    rows = x2d.shape[0]
    tm = OUT_TM
    row = lambda i: (i, 0)
    const = lambda i: (0, 0)
    single = pl.Buffered(1)
    return pl.pallas_call(
        _out_kernel,
        grid=(rows // tm,),
        in_specs=[
            pl.BlockSpec((tm, D_MODEL), row),
            pl.BlockSpec((tm, DA_WIDTH), row),
            pl.BlockSpec((tm, MLA_WIDTH), row),
            pl.BlockSpec((D_MODEL, D_MODEL), const, pipeline_mode=single),
            pl.BlockSpec((1, D_MODEL), const),
            pl.BlockSpec((1, D_MODEL), const),
            pl.BlockSpec((D_MODEL, D_FF), const, pipeline_mode=single),
            pl.BlockSpec((D_FF, D_MODEL), const, pipeline_mode=single),
            pl.BlockSpec((1, D_MODEL), const),
        ],
        out_specs=pl.BlockSpec((tm, D_MODEL), row),
        out_shape=jax.ShapeDtypeStruct((rows, D_MODEL), F32),
        compiler_params=pltpu.CompilerParams(
            dimension_semantics=("arbitrary",), vmem_limit_bytes=VMEM_LIMIT),
        name="out_mlp",
    )(x2d, yda, ymla, w_o, g_post, g_pre, w1, w2, g_mlp)


def _position_tables(n_pos):
    inv_freq = 1.0 / (ROPE_THETA ** (np.arange(0, MLA_ROPE, 2, dtype=np.float64) / MLA_ROPE))
    ang = np.arange(n_pos, dtype=np.float64)[:, None] * inv_freq[None, :]
    cos, sin = np.cos(ang), np.sin(ang)
    ct = np.concatenate([cos, cos] * (LANES // MLA_ROPE), axis=1).astype(np.float32)
    st = np.concatenate([-sin, sin] * (LANES // MLA_ROPE), axis=1).astype(np.float32)
    ipos = np.arange(n_pos)
    hi = ((ipos // CHUNK) * CHUNK)[:, None]
    lo = (ipos % CHUNK)[:, None]
    posf = np.concatenate([hi, lo] * ALIBI_PIECES
                          + [np.zeros((n_pos, LANES - 2 * ALIBI_PIECES), np.int64)], axis=1)
    return ct, st, posf.astype(ml_dtypes.bfloat16)


def kernel(x, meta_tokens, g_attn_pre, w_in, da_lambda_q1, da_lambda_k1, da_lambda_q2,
           da_lambda_k2, g_da_sub, g_mla_q, w_mla_q_up, g_mla_kv, w_mla_kv_up, w_o,
           g_attn_post, g_mlp_pre, w_ff1, w_ff2, g_mlp_post):
    nb, seq, d = x.shape
    assert d == D_MODEL and w_in.shape[0] == 1, "single-layer block only"
    assert seq % ATT_TQ == 0 and seq % PROJ_TM == 0 and ATT_TQ == ATT_TK

    w_in0 = w_in[0].astype(BF16)
    n_da = 2 * DA_WIDTH
    v_cols = slice(n_da, n_da + DA_WIDTH)
    rest = w_in0[:, n_da + DA_WIDTH:]
    kr = rest[:, MLA_Q_RANK + MLA_KV_RANK:]
    zpad = jnp.zeros((D_MODEL, LANES - MLA_ROPE), BF16)
    w_in_ext = jnp.concatenate(
        [w_in0[:, :n_da], rest[:, :MLA_Q_RANK + MLA_KV_RANK], kr, zpad], axis=1)
    w_vt = w_in0[:, v_cols].T
    wq = w_mla_q_up[0].astype(BF16).reshape(MLA_Q_RANK, MLA_HEADS, MLA_NOPE + MLA_ROPE)
    wq_ext = jnp.concatenate(
        [wq[..., :MLA_NOPE].reshape(MLA_Q_RANK, Q_ROPE0),
         wq[..., MLA_NOPE:].reshape(MLA_Q_RANK, MLA_HEADS * MLA_ROPE)], axis=1)
    wkv = w_mla_kv_up[0].astype(BF16).reshape(MLA_KV_RANK, MLA_HEADS, MLA_NOPE + MLA_V)
    wkv_k = wkv[..., :MLA_NOPE].reshape(MLA_KV_RANK, MLA_HEADS * MLA_NOPE)
    wkv_vt = wkv[..., MLA_NOPE:].reshape(MLA_KV_RANK, MLA_WIDTH).T

    ct, st, posf = _position_tables(N_META + seq)
    lam_rows = jnp.concatenate(
        [da_lambda_q1, da_lambda_k1, da_lambda_q2, da_lambda_k2], axis=0).astype(F32)
    lam_params = jnp.pad(lam_rows, ((0, 8 - lam_rows.shape[0]), (0, LANES - DA_HEAD_DIM)))

    proj = functools.partial(
        _proj_call, g_pre=g_attn_pre, w_in=w_in_ext, w_vt=w_vt, g_q=g_mla_q, w_q=wq_ext,
        g_kv=g_mla_kv, w_kv=wkv_k, w_kvt=wkv_vt)
    x2d = x.reshape(nb * seq, D_MODEL)
    qda, kda, vtda, qm, km, vtm = proj(
        x2d, PROJ_TM, seq // PROJ_TM, ct=ct[N_META:], st=st[N_META:], name="proj_tokens")
    meta_rows = jnp.pad(meta_tokens.astype(F32), ((0, META_PAD - N_META), (0, 0)))
    _, kda_meta, vtda_meta, _, km_meta, vtm_meta = proj(
        meta_rows, META_PAD, 1, ct=ct[:META_PAD], st=st[:META_PAD], name="proj_meta")

    def per_batch(a):
        return a.reshape(nb, seq, a.shape[-1])

    yda = _da_call(per_batch(qda), per_batch(kda), posf[N_META:], vtda, kda_meta, posf[:META_PAD],
                   vtda_meta, lam_params, g_da_sub.reshape(DA_V_DIM, 1))
    ymla = _mla_call(per_batch(qm), per_batch(km), vtm, km_meta, vtm_meta)

    out = _out_call(x2d, yda.reshape(nb * seq, DA_WIDTH), ymla.reshape(nb * seq, MLA_WIDTH),
                    w_o[0].astype(BF16), g_attn_post, g_mlp_pre,
                    w_ff1[0].astype(BF16), w_ff2[0].astype(BF16), g_mlp_post)
    return out.reshape(nb, seq, D_MODEL)
```
